```python
import math
import jax, jax.numpy as jnp
from jax import lax
import numpy as np

D_MODEL = 1024
BATCH = 8
SEQ = 4096
DEPTH = 4

CTX_LEN = 256
GRID_W = 64
HEAD_DIM = 64
A_HEADS = 8
A_KV_HEADS = 2
A_GROUP = A_HEADS // A_KV_HEADS
A_WINDOW = 128
A_BLOCK = 128
B_HEADS = 4
NA_ROWS = 8
NA_COLS = 16
NA_QCOLS = 16
C_HEADS = 4
C_Q_RANK = 256
C_KV_RANK = 128
C_NOPE = 64
C_ROPE = 32
C_V = 64
C_BLOCK = 128
MIX_WIDTH = A_HEADS * HEAD_DIM + B_HEADS * HEAD_DIM + C_HEADS * C_V
IN_WIDTHS = (A_HEADS * HEAD_DIM, A_KV_HEADS * HEAD_DIM, A_KV_HEADS * HEAD_DIM,
             B_HEADS * HEAD_DIM, B_HEADS * HEAD_DIM, B_HEADS * HEAD_DIM,
             C_Q_RANK, C_KV_RANK, C_ROPE)
IN_WIDTH = (A_HEADS + 2 * A_KV_HEADS + 3 * B_HEADS) * HEAD_DIM + C_Q_RANK + C_KV_RANK + C_ROPE
D_FF = 4 * D_MODEL
N_MOD = 6
ROPE_BASE = 10000.0
EPS = 1e-6
NEG_INF = -1e30

kernel_name = 'hybrid_dit_parallel_head_groups'


def rmsnorm(x, g):
    xf = x.astype(jnp.float32)
    y = xf * lax.rsqrt(jnp.mean(jnp.square(xf), axis=-1, keepdims=True) + EPS)
    return (y * g.astype(jnp.float32)).astype(x.dtype)


def modulate(h, shift, scale):
    return h * (1 + scale) + shift


def split_cols(p):
    outs, off = [], 0
    for w in IN_WIDTHS:
        outs.append(p[..., off:off + w])
        off += w
    return outs


def split_heads(t, h):
    return t.reshape(*t.shape[:-1], h, t.shape[-1] // h)


def rope_1d(x, pos):
    half = x.shape[-1] // 2
    freqs = ROPE_BASE ** (-jnp.arange(half, dtype=jnp.float32) / half)
    ang = pos.astype(jnp.float32)[:, None] * freqs
    cos = jnp.cos(ang)[:, None, :].astype(x.dtype)
    sin = jnp.sin(ang)[:, None, :].astype(x.dtype)
    x1, x2 = x[..., :half], x[..., half:]
    return jnp.concatenate([x1 * cos - x2 * sin, x1 * sin + x2 * cos], axis=-1)


def axial_rope(x, row, col):
    n = x.shape[-1] // 2
    return jnp.concatenate([rope_1d(x[..., :n], row), rope_1d(x[..., n:], col)], axis=-1)


def joint_softmax(parts, sink=None):
    parts = [p.astype(jnp.float32) for p in parts]
    m = parts[0].max(axis=-1, keepdims=True)
    for p in parts[1:]:
        m = jnp.maximum(m, p.max(axis=-1, keepdims=True))
    if sink is not None:
        sink = sink.astype(jnp.float32)
        m = jnp.maximum(m, sink)
    exps = [jnp.exp(p - m) for p in parts]
    denom = exps[0].sum(axis=-1, keepdims=True)
    for e in exps[1:]:
        denom = denom + e.sum(axis=-1, keepdims=True)
    if sink is not None:
        denom = denom + jnp.exp(sink - m)
    return [e / denom for e in exps]


def dense_attention(q, k, v, sink=None):
    scale = q.shape[-1] ** -0.5
    s = jnp.einsum('bqhd,bkhd->bhqk', q, k) * scale
    (p,) = joint_softmax([s], None if sink is None else sink[:, None, None])
    return jnp.einsum('bhqk,bkhd->bqhd', p.astype(v.dtype), v)


def windowed_gqa(q, k, v, k_ctx, v_ctx, sink):
    B, S, _, d = q.shape
    nb = S // A_BLOCK
    qb = q.reshape(B, nb, A_BLOCK, A_KV_HEADS, A_GROUP, d)
    pad = ((0, 0), (A_BLOCK, A_BLOCK), (0, 0), (0, 0))

    def band(t):
        tb = jnp.pad(t, pad).reshape(B, nb + 2, A_BLOCK, A_KV_HEADS, d)
        return jnp.concatenate([tb[:, :-2], tb[:, 1:-1], tb[:, 2:]], axis=2)

    kb, vb = band(k), band(v)
    qi = np.arange(A_BLOCK)[:, None]
    ks = np.arange(3 * A_BLOCK)[None, :]
    rel = ks - A_BLOCK - qi
    kpos = (np.arange(nb)[:, None, None] - 1) * A_BLOCK + ks[None]
    mask = (np.abs(rel)[None] <= A_WINDOW) & (kpos >= 0) & (kpos < S)
    scale = d ** -0.5
    s_loc = jnp.einsum('bnqkgd,bnskd->bnkgqs', qb, kb) * scale
    s_loc = jnp.where(mask[None, :, None, None], s_loc.astype(jnp.float32), NEG_INF)
    s_ctx = jnp.einsum('bnqkgd,bckd->bnkgqc', qb, k_ctx) * scale
    p_loc, p_ctx = joint_softmax([s_loc, s_ctx], sink.reshape(A_KV_HEADS, A_GROUP)[:, :, None, None])
    o = (jnp.einsum('bnkgqs,bnskd->bnqkgd', p_loc.astype(v.dtype), vb)
         + jnp.einsum('bnkgqc,bckd->bnqkgd', p_ctx.astype(v.dtype), v_ctx))
    return o.reshape(B, S, A_HEADS * d)


def na_layout(rows):
    kr, kc = min(NA_ROWS, rows), NA_COLS
    qr, qc = math.gcd(rows, NA_ROWS), NA_QCOLS
    krb, kcb = min(qr - 1 + kr, rows), min(qc - 1 + kc, GRID_W)
    nrb, ncb = rows // qr, GRID_W // qc
    q_r = np.arange(nrb)[:, None] * qr + np.arange(qr)[None, :]
    q_c = np.arange(ncb)[:, None] * qc + np.arange(qc)[None, :]
    w_r = np.clip(q_r - kr // 2, 0, rows - kr)
    w_c = np.clip(q_c - kc // 2, 0, GRID_W - kc)
    k_r = np.minimum(w_r[:, 0], rows - krb)[:, None] + np.arange(krb)[None, :]
    k_c = np.minimum(w_c[:, 0], GRID_W - kcb)[:, None] + np.arange(kcb)[None, :]
    qr6 = q_r[:, None, :, None, None, None]
    qc6 = q_c[None, :, None, :, None, None]
    wr6 = w_r[:, None, :, None, None, None]
    wc6 = w_c[None, :, None, :, None, None]
    kr6 = k_r[:, None, None, None, :, None]
    kc6 = k_c[None, :, None, None, None, :]
    shape6 = (nrb, ncb, qr, qc, krb, kcb)

    def flat(a):
        return np.broadcast_to(a, shape6).reshape(nrb, ncb, qr * qc, krb * kcb)

    mask = flat((kr6 >= wr6) & (kr6 < wr6 + kr) & (kc6 >= wc6) & (kc6 < wc6 + kc))
    d_r = flat(np.clip(kr6 - qr6, 1 - NA_ROWS, NA_ROWS - 1) + NA_ROWS - 1)
    d_c = flat(np.clip(kc6 - qc6, 1 - NA_COLS, NA_COLS - 1) + NA_COLS - 1)
    key_tok = (k_r[:, None, :, None] * GRID_W + k_c[None, :, None, :]).reshape(nrb, ncb, krb * kcb)
    return qr, qc, nrb, ncb, mask, d_r, d_c, key_tok


def neighbourhood_attention(q, k, v, k_ctx, v_ctx, rpb):
    B, S, H, d = q.shape
    rows = S // GRID_W
    qr, qc, nrb, ncb, mask, d_r, d_c, key_tok = na_layout(rows)
    qb = q.reshape(B, nrb, qr, ncb, qc, H, d).transpose(0, 1, 3, 2, 4, 5, 6).reshape(B, nrb, ncb, qr * qc, H, d)
    kg = k[:, key_tok]
    vg = v[:, key_tok]
    scale = d ** -0.5
    bias = rpb[:, d_r, d_c].transpose(1, 2, 0, 3, 4)
    s_loc = jnp.einsum('bijqhd,bijkhd->bijhqk', qb, kg) * scale
    s_loc = jnp.where(mask[:, :, None], s_loc.astype(jnp.float32) + bias.astype(jnp.float32), NEG_INF)
    s_ctx = jnp.einsum('bijqhd,bchd->bijhqc', qb, k_ctx) * scale
    p_loc, p_ctx = joint_softmax([s_loc, s_ctx])
    o = (jnp.einsum('bijhqk,bijkhd->bijqhd', p_loc.astype(v.dtype), vg)
         + jnp.einsum('bijhqc,bchd->bijqhd', p_ctx.astype(v.dtype), v_ctx))
    o = o.reshape(B, nrb, ncb, qr, qc, H, d).transpose(0, 1, 3, 2, 4, 5, 6)
    return o.reshape(B, S, H * d)


def mla_q(cq, g, w_uq):
    q = split_heads(rmsnorm(cq, g) @ w_uq, C_HEADS)
    return q[..., :C_NOPE], q[..., C_NOPE:]


def mla_kv(ckv, g, w_ukv):
    kv = split_heads(rmsnorm(ckv, g) @ w_ukv, C_HEADS)
    return kv[..., :C_NOPE], kv[..., C_NOPE:]


def mla_latent(q_nope, q_rope, k_nope, k_rope, v, kn_ctx, kr_ctx, v_ctx):
    B, S, H, _ = q_nope.shape
    kn_all = jnp.concatenate([k_nope, kn_ctx], axis=1)
    kr_all = jnp.concatenate([k_rope, kr_ctx], axis=1)
    v_all = jnp.concatenate([v, v_ctx], axis=1)
    scale = (C_NOPE + C_ROPE) ** -0.5
    nb = S // C_BLOCK

    def block(args):
        qn, qp = args
        s = jnp.einsum('bqhd,bkhd->bhqk', qn, kn_all) + jnp.einsum('bqhr,bkr->bhqk', qp, kr_all)
        (p,) = joint_softmax([s * scale])
        return jnp.einsum('bhqk,bkhd->bqhd', p.astype(v_all.dtype), v_all)

    def to_blocks(t):
        return jnp.moveaxis(t.reshape(B, nb, C_BLOCK, *t.shape[2:]), 1, 0)

    o = lax.map(block, (to_blocks(q_nope), to_blocks(q_rope)))
    return jnp.moveaxis(o, 0, 1).reshape(B, S, H * C_V)


def sq_relu_mlp(h, w1, w2):
    return jnp.square(jax.nn.relu(h @ w1)) @ w2


def setup_inputs(seed: int = 0) -> dict:
    key = jax.random.key(seed)
    ks = jax.random.split(key, 19)

    def nrm(k, shape, s):
        return jax.random.normal(k, shape, jnp.float32) * s

    def gain(k, shape):
        return 1.0 + 0.1 * jax.random.normal(k, shape, jnp.float32)

    return {
        'x': nrm(ks[0], (BATCH, SEQ, D_MODEL), 1.0),
        'c': nrm(ks[1], (BATCH, D_MODEL), 1.0),
        'ctx': nrm(ks[2], (BATCH, CTX_LEN, D_MODEL), 1.0),
        'c_ctx': nrm(ks[3], (D_MODEL,), 1.0),
        'w_ada': nrm(ks[4], (DEPTH, D_MODEL, N_MOD * D_MODEL), 0.5 * D_MODEL ** -0.5),
        'b_ada': nrm(ks[5], (DEPTH, N_MOD * D_MODEL), 0.02),
        'norm1_g': gain(ks[6], (DEPTH, D_MODEL)),
        'norm2_g': gain(ks[7], (DEPTH, D_MODEL)),
        'w_in': nrm(ks[8], (DEPTH, D_MODEL, IN_WIDTH), D_MODEL ** -0.5),
        'attn_sink': nrm(ks[9], (DEPTH, A_HEADS), 0.5),
        'na_rpb': nrm(ks[10], (DEPTH, B_HEADS, 2 * NA_ROWS - 1, 2 * NA_COLS - 1), 0.2),
        'mla_q_norm_g': gain(ks[11], (DEPTH, C_Q_RANK)),
        'mla_w_uq': nrm(ks[12], (DEPTH, C_Q_RANK, C_HEADS * (C_NOPE + C_ROPE)), C_Q_RANK ** -0.5),
        'mla_kv_norm_g': gain(ks[13], (DEPTH, C_KV_RANK)),
        'mla_w_ukv': nrm(ks[14], (DEPTH, C_KV_RANK, C_HEADS * (C_NOPE + C_V)), C_KV_RANK ** -0.5),
        'w_out': nrm(ks[15], (DEPTH, MIX_WIDTH, D_MODEL), MIX_WIDTH ** -0.5),
        'w_mlp_in': nrm(ks[16], (DEPTH, D_MODEL, D_FF), D_MODEL ** -0.5),
        'w_mlp_out': nrm(ks[17], (DEPTH, D_FF, D_MODEL), D_FF ** -0.5),
        'final_norm_g': gain(ks[18], (D_MODEL,)),
    }


def reference(x, c, ctx, c_ctx, w_ada, b_ada, norm1_g, norm2_g, w_in, attn_sink, na_rpb,
              mla_q_norm_g, mla_w_uq, mla_kv_norm_g, mla_w_ukv, w_out, w_mlp_in, w_mlp_out,
              final_norm_g):
    B, S, _ = x.shape
    C = ctx.shape[1]
    tok = jnp.arange(S)
    row, col = tok // GRID_W, tok % GRID_W
    silu_c = jax.nn.silu(c)
    silu_cc = jax.nn.silu(c_ctx)
    for l in range(DEPTH):
        last = l == DEPTH - 1
        mod_x = jnp.split((silu_c @ w_ada[l] + b_ada[l])[:, None, :], N_MOD, axis=-1)
        mod_c = jnp.split(silu_cc @ w_ada[l] + b_ada[l], N_MOD, axis=-1)

        hx = modulate(rmsnorm(x, norm1_g[l]), mod_x[0], mod_x[1])
        hc = modulate(rmsnorm(ctx, norm1_g[l]), mod_c[0], mod_c[1])
        xa_q, xa_k, xa_v, xb_q, xb_k, xb_v, xc_q, xc_kv, xc_kr = split_cols(hx @ w_in[l])
        ca_q, ca_k, ca_v, cb_q, cb_k, cb_v, cc_q, cc_kv, cc_kr = split_cols(hc @ w_in[l])

        ka_c, va_c = split_heads(ca_k, A_KV_HEADS), split_heads(ca_v, A_KV_HEADS)
        kb_c, vb_c = split_heads(cb_k, B_HEADS), split_heads(cb_v, B_HEADS)
        kn_c, vc_c = mla_kv(cc_kv, mla_kv_norm_g[l], mla_w_ukv[l])

        qa = axial_rope(split_heads(xa_q, A_HEADS), row, col)
        ka = axial_rope(split_heads(xa_k, A_KV_HEADS), row, col)
        out_a = windowed_gqa(qa, ka, split_heads(xa_v, A_KV_HEADS), ka_c, va_c, attn_sink[l])
        out_b = neighbourhood_attention(split_heads(xb_q, B_HEADS), split_heads(xb_k, B_HEADS),
                                        split_heads(xb_v, B_HEADS), kb_c, vb_c, na_rpb[l])
        qn, qp = mla_q(xc_q, mla_q_norm_g[l], mla_w_uq[l])
        kn, vc = mla_kv(xc_kv, mla_kv_norm_g[l], mla_w_ukv[l])
        qp = axial_rope(qp, row, col)
        kp = axial_rope(xc_kr[:, :, None, :], row, col)[:, :, 0]
        out_c = mla_latent(qn, qp, kn, kp, vc, kn_c, cc_kr, vc_c)

        x = x + mod_x[2] * (jnp.concatenate([out_a, out_b, out_c], axis=-1) @ w_out[l])

        if not last:
            oa = dense_attention(split_heads(ca_q, A_HEADS), jnp.repeat(ka_c, A_GROUP, axis=2),
                                 jnp.repeat(va_c, A_GROUP, axis=2), attn_sink[l])
            ob = dense_attention(split_heads(cb_q, B_HEADS), kb_c, vb_c)
            qn_c, qp_c = mla_q(cc_q, mla_q_norm_g[l], mla_w_uq[l])
            kp_c = jnp.broadcast_to(cc_kr[:, :, None, :], (B, C, C_HEADS, C_ROPE))
            oc = dense_attention(jnp.concatenate([qn_c, qp_c], axis=-1),
                                 jnp.concatenate([kn_c, kp_c], axis=-1), vc_c)
            mixed_c = jnp.concatenate([oa.reshape(B, C, -1), ob.reshape(B, C, -1),
                                       oc.reshape(B, C, -1)], axis=-1)
            ctx = ctx + mod_c[2] * (mixed_c @ w_out[l])

        x = x + mod_x[5] * sq_relu_mlp(modulate(rmsnorm(x, norm2_g[l]), mod_x[3], mod_x[4]),
                                       w_mlp_in[l], w_mlp_out[l])
        if not last:
            ctx = ctx + mod_c[5] * sq_relu_mlp(modulate(rmsnorm(ctx, norm2_g[l]), mod_c[3], mod_c[4]),
                                               w_mlp_in[l], w_mlp_out[l])
    return rmsnorm(x, final_norm_g)
```

```python
import functools

import jax
import jax.numpy as jnp
from jax import lax
from jax.experimental import pallas as pl
from jax.experimental.pallas import tpu as pltpu

F32 = jnp.float32
BF16 = jnp.bfloat16

GRID_W = 64
HEAD_DIM = 64
A_HEADS = 8
A_KV_HEADS = 2
A_BLOCK = 128
B_HEADS = 4
NA_ROWS = 8
NA_COLS = 16
C_HEADS = 4
C_Q_RANK = 256
C_KV_RANK = 128
C_NOPE = 64
C_ROPE = 32
C_V = 64
N_MOD = 6
ROPE_BASE = 10000.0
EPS = 1e-6
NEG_INF = -1e30

LANES = 128
MOD_ROWS = 16
IN_COLS = 2048
VMEM_LIMIT = 56 * 1024 * 1024

_OFF_QA, _OFF_KA, _OFF_VA = 0, 512, 640
_OFF_QB, _OFF_KB, _OFF_VB = 768, 1024, 1280
_OFF_CQ, _OFF_CKV, _OFF_KR = 1536, 1792, 1920


def _dot(a, b):
    return jnp.dot(a, b, preferred_element_type=F32)


def _dot_t(a, b):
    return lax.dot_general(a, b, (((1,), (1,)), ((), ())), preferred_element_type=F32)


def _params(*sem):
    return pltpu.CompilerParams(dimension_semantics=sem, vmem_limit_bytes=VMEM_LIMIT)


def _mod_kernel(c_ref, w_ref, b_ref, o_ref):
    c = c_ref[...]
    s = c * (1.0 / (1.0 + jnp.exp(-c)))
    o_ref[0] = jnp.dot(s, w_ref[0], preferred_element_type=F32,
                       precision=lax.Precision.HIGHEST) + b_ref[0]


def _modulation(cvec, w_ada, b_ada):
    depth, d, n = w_ada.shape
    tn = 1024
    return pl.pallas_call(
        _mod_kernel,
        grid=(depth, n // tn),
        in_specs=[
            pl.BlockSpec((MOD_ROWS, d), lambda l, j: (0, 0)),
            pl.BlockSpec((1, d, tn), lambda l, j: (l, 0, j)),
            pl.BlockSpec((1, 1, tn), lambda l, j: (l, 0, j)),
        ],
        out_specs=pl.BlockSpec((1, MOD_ROWS, tn), lambda l, j: (l, 0, j)),
        out_shape=jax.ShapeDtypeStruct((depth, MOD_ROWS, n), F32),
        compiler_params=_params("arbitrary", "arbitrary"),
        name="adaln_modulation",
    )(cvec, w_ada, b_ada.reshape(depth, 1, n))


def _rope(x, c, s1, s2, shift):
    return x * c + pltpu.roll(x, LANES - shift, 1) * s1 + pltpu.roll(x, shift, 1) * s2


def _rms(x, g):
    return x * lax.rsqrt(jnp.mean(x * x, axis=-1, keepdims=True) + EPS) * g


def _inproj_kernel(x_ref, mod_ref, g1_ref, win_ref, gq_ref, wuq_ref, gkv_ref, wuk_ref, wuv_ref,
                   ac_ref, as1_ref, as2_ref, cc_ref, cs1_ref, cs2_ref,
                   qa_ref, ka_ref, va_ref, qb_ref, kb_ref, vb_ref, qc_ref, kc_ref, vc_ref):
    x = x_ref[0]
    mod = mod_ref[0]
    h = _rms(x, g1_ref[...]) * (1.0 + mod[1:2]) + mod[0:1]
    p = _dot(h.astype(BF16), win_ref[...])

    ac, as1, as2 = ac_ref[...], as1_ref[...], as2_ref[...]
    cc, cs1, cs2 = cc_ref[...], cs1_ref[...], cs2_ref[...]
    scale_ab = HEAD_DIM ** -0.5
    scale_c = (C_NOPE + C_ROPE) ** -0.5

    for j in range(4):
        blk = p[:, _OFF_QA + LANES * j:_OFF_QA + LANES * (j + 1)]
        qa_ref[0, :, LANES * j:LANES * (j + 1)] = (_rope(blk, ac, as1, as2, 16) * scale_ab).astype(BF16)
    ka_ref[0] = _rope(p[:, _OFF_KA:_OFF_KA + LANES], ac, as1, as2, 16).astype(BF16)
    va_ref[0] = p[:, _OFF_VA:_OFF_VA + LANES].astype(BF16)
    qb_ref[0] = (p[:, _OFF_QB:_OFF_QB + 256] * scale_ab).astype(BF16)
    kb_ref[0] = p[:, _OFF_KB:_OFF_KB + 256].astype(BF16)
    vb_ref[0] = p[:, _OFF_VB:_OFF_VB + 256].astype(BF16)

    cq = _rms(p[:, _OFF_CQ:_OFF_CQ + C_Q_RANK], gq_ref[...]).astype(BF16)
    q_up = _dot(cq, wuq_ref[...])
    ckv = _rms(p[:, _OFF_CKV:_OFF_CKV + C_KV_RANK], gkv_ref[...]).astype(BF16)
    k_up = _dot(ckv, wuk_ref[...])
    vc_ref[0] = _dot(ckv, wuv_ref[...]).astype(BF16)
    kr = _rope(p[:, _OFF_KR:_OFF_KR + LANES], cc, cs1, cs2, 8)
    for hd in range(C_HEADS):
        sl = slice(LANES * hd, LANES * (hd + 1))
        qc_ref[0, :, sl] = (_rope(q_up[:, sl], cc, cs1, cs2, 8) * scale_c).astype(BF16)
        kc_ref[0, :, sl] = (k_up[:, sl] + kr).astype(BF16)


def _inproj(x, mod, g1, win, gq, wuq, gkv, wuk, wuv, tabs, tm):
    b, s, d = x.shape
    nt = s // tm
    per_batch_mod = mod.shape[0] > 1

    def xmap(j, bb):
        return (bb, j, 0)

    def modmap(j, bb):
        return ((bb if per_batch_mod else 0), 0, 0)

    def const2(j, bb):
        return (0, 0)

    def tabmap(j, bb):
        return (j, 0)

    widths = (512, 128, 128, 256, 256, 256, 512, 512, 256)
    in_specs = [
        pl.BlockSpec((1, tm, d), xmap),
        pl.BlockSpec((1, N_MOD, d), modmap),
        pl.BlockSpec((1, d), const2),
        pl.BlockSpec((d, IN_COLS), const2),
        pl.BlockSpec((1, C_Q_RANK), const2),
        pl.BlockSpec((C_Q_RANK, 512), const2),
        pl.BlockSpec((1, C_KV_RANK), const2),
        pl.BlockSpec((C_KV_RANK, 512), const2),
        pl.BlockSpec((C_KV_RANK, 256), const2),
    ] + [pl.BlockSpec((tm, LANES), tabmap) for _ in range(6)]
    return pl.pallas_call(
        _inproj_kernel,
        grid=(nt, b),
        in_specs=in_specs,
        out_specs=[pl.BlockSpec((1, tm, w), xmap) for w in widths],
        out_shape=[jax.ShapeDtypeStruct((b, s, w), BF16) for w in widths],
        compiler_params=_params("arbitrary", "arbitrary"),
        name="norm_inproj_rope",
    )(x, mod, g1, win, gq, wuq, gkv, wuk, wuv, *tabs)


def _split_pair(q2):
    lo = lax.broadcasted_iota(jnp.int32, q2.shape, 1) < HEAD_DIM
    zero = jnp.zeros_like(q2)
    return jnp.concatenate([jnp.where(lo, q2, zero), jnp.where(lo, zero, q2)], axis=0)


def _merge_pair(o):
    n = o.shape[0] // 2
    lo = lax.broadcasted_iota(jnp.int32, (n, o.shape[1]), 1) < HEAD_DIM
    return jnp.where(lo, o[:n], o[n:])


def _sink_col(sink_ref, h_lo, h_hi, n):
    return jnp.concatenate([jnp.broadcast_to(sink_ref[h_lo:h_lo + 1, 0:1], (n, 1)),
                            jnp.broadcast_to(sink_ref[h_hi:h_hi + 1, 0:1], (n, 1))], axis=0)


def _softmax_pv(parts, sink=None):
    m = parts[0][0].max(axis=-1, keepdims=True)
    for s, _ in parts[1:]:
        m = jnp.maximum(m, s.max(axis=-1, keepdims=True))
    if sink is not None:
        m = jnp.maximum(m, sink)
    denom = None
    acc = None
    for s, v in parts:
        e = jnp.exp(s - m)
        r = e.sum(axis=-1, keepdims=True)
        denom = r if denom is None else denom + r
        o = _dot(e.astype(BF16), v)
        acc = o if acc is None else acc + o
    if sink is not None:
        denom = denom + jnp.exp(sink - m)
    return acc / denom


def _attn_a_kernel(q_ref, kp_ref, kc_ref, kn_ref, kx_ref, vp_ref, vc_ref, vn_ref, vx_ref,
                   sink_ref, o_ref):
    i = pl.program_id(1)
    nb = pl.num_programs(1)
    n = A_BLOCK
    row = lax.broadcasted_iota(jnp.int32, (2 * n, n), 0) & (n - 1)
    col = lax.broadcasted_iota(jnp.int32, (2 * n, n), 1)
    keep_prev = col >= row + jnp.where(i > 0, 0, 2 * n)
    keep_next = col <= row - jnp.where(i < nb - 1, 0, 2 * n)
    kp, kc, kn, kx = kp_ref[0], kc_ref[0], kn_ref[0], kx_ref[0]
    vp, vc, vn, vx = vp_ref[0], vc_ref[0], vn_ref[0], vx_ref[0]
    for j in range(A_HEADS // 2):
        q = _split_pair(q_ref[0, :, LANES * j:LANES * (j + 1)])
        sp = jnp.where(keep_prev, _dot_t(q, kp), NEG_INF)
        sn = jnp.where(keep_next, _dot_t(q, kn), NEG_INF)
        sink = _sink_col(sink_ref, j, j + A_HEADS // 2, n)
        o = _softmax_pv([(sp, vp), (_dot_t(q, kc), vc), (sn, vn), (_dot_t(q, kx), vx)], sink)
        o_ref[0, :, LANES * j:LANES * (j + 1)] = _merge_pair(o).astype(BF16)


def _attn_a(qa, ka, va, kax, vax, sink):
    b, s, _ = qa.shape
    c = kax.shape[1]
    nb = s // A_BLOCK
    blk = (1, A_BLOCK, LANES)

    def prev(bb, i):
        return (bb, jnp.maximum(i - 1, 0), 0)

    def cur(bb, i):
        return (bb, i, 0)

    def nxt(bb, i):
        return (bb, jnp.minimum(i + 1, nb - 1), 0)

    def ctx(bb, i):
        return (bb, 0, 0)

    return pl.pallas_call(
        _attn_a_kernel,
        grid=(b, nb),
        in_specs=[
            pl.BlockSpec((1, A_BLOCK, 512), cur),
            pl.BlockSpec(blk, prev), pl.BlockSpec(blk, cur), pl.BlockSpec(blk, nxt),
            pl.BlockSpec((1, c, LANES), ctx),
            pl.BlockSpec(blk, prev), pl.BlockSpec(blk, cur), pl.BlockSpec(blk, nxt),
            pl.BlockSpec((1, c, LANES), ctx),
            pl.BlockSpec((A_HEADS, LANES), lambda bb, i: (0, 0)),
        ],
        out_specs=pl.BlockSpec((1, A_BLOCK, 512), cur),
        out_shape=jax.ShapeDtypeStruct((b, s, 512), BF16),
        compiler_params=_params("arbitrary", "arbitrary"),
        name="attn_window_gqa",
    )(qa, ka, ka, ka, kax, va, va, va, vax, sink)


def _attn_b_kernel(q_ref, k_ref, v_ref, kx_ref, vx_ref, bias_ref, o_ref, *, rows):
    rb = pl.program_id(2)
    kx, vx = kx_ref[0], vx_ref[0]

    def body(r, carry):
        qr = rb * NA_ROWS + r
        wr = jnp.clip(qr - NA_ROWS // 2, 0, rows - NA_ROWS)
        oi = wr - qr + (NA_ROWS - 1)
        start = pl.multiple_of(wr * GRID_W, GRID_W)
        qoff = pl.multiple_of(r * GRID_W, GRID_W)
        q = _split_pair(q_ref[0, pl.ds(qoff, GRID_W), :])
        ks = k_ref[0, pl.ds(start, NA_ROWS * GRID_W), :]
        vs = v_ref[0, pl.ds(start, NA_ROWS * GRID_W), :]
        bias = jnp.concatenate([bias_ref[0, oi], bias_ref[0, NA_ROWS + oi]], axis=0)
        o = _softmax_pv([(_dot_t(q, ks) + bias, vs), (_dot_t(q, kx), vx)])
        o_ref[0, pl.ds(qoff, GRID_W), :] = _merge_pair(o).astype(BF16)
        return carry

    lax.fori_loop(0, NA_ROWS, body, 0)


def _attn_b(qb, kb, vb, kbx, vbx, bias):
    b, s, _ = qb.shape
    c = kbx.shape[1]
    rows = s // GRID_W
    tq = NA_ROWS * GRID_W

    def qmap(bb, p, rb):
        return (bb, rb, p)

    def kmap(bb, p, rb):
        return (bb, 0, p)

    return pl.pallas_call(
        functools.partial(_attn_b_kernel, rows=rows),
        grid=(b, B_HEADS // 2, s // tq),
        in_specs=[
            pl.BlockSpec((1, tq, LANES), qmap),
            pl.BlockSpec((1, s, LANES), kmap),
            pl.BlockSpec((1, s, LANES), kmap),
            pl.BlockSpec((1, c, LANES), kmap),
            pl.BlockSpec((1, c, LANES), kmap),
            pl.BlockSpec((1, 2 * NA_ROWS, GRID_W, NA_ROWS * GRID_W), lambda bb, p, rb: (p, 0, 0, 0)),
        ],
        out_specs=pl.BlockSpec((1, tq, LANES), qmap),
        out_shape=jax.ShapeDtypeStruct((b, s, 256), BF16),
        compiler_params=_params("arbitrary", "arbitrary", "arbitrary"),
        name="attn_neighbourhood",
    )(qb, kb, vb, kbx, vbx, bias)


def _na_bias_table(rpb):
    h = rpb.shape[0]
    oi = jnp.arange(NA_ROWS)[:, None] + jnp.arange(NA_ROWS)[None, :]
    qc = jnp.arange(GRID_W)[:, None]
    kc = jnp.arange(GRID_W)[None, :]
    dc = jnp.clip(kc - qc, 1 - NA_COLS, NA_COLS - 1) + NA_COLS - 1
    wc = jnp.clip(qc - NA_COLS // 2, 0, GRID_W - NA_COLS)
    valid = (kc >= wc) & (kc < wc + NA_COLS)
    t = rpb[:, oi][:, :, :, dc]
    t = jnp.where(valid[None, None, None], t, NEG_INF)
    t = t.transpose(0, 1, 3, 2, 4).reshape(h // 2, 2 * NA_ROWS, GRID_W, NA_ROWS * GRID_W)
    return t.astype(F32)


def _attn_c_kernel(q_ref, k_ref, kx_ref, v_ref, vx_ref, o_ref):
    v, vx = v_ref[0], vx_ref[0]
    outs = []
    for hh in range(2):
        sl = slice(LANES * hh, LANES * (hh + 1))
        q = q_ref[0, :, sl]
        outs.append(_softmax_pv([(_dot_t(q, k_ref[0, :, sl]), v), (_dot_t(q, kx_ref[0, :, sl]), vx)]))
    o_ref[0] = _merge_pair(jnp.concatenate(outs, axis=0)).astype(BF16)


def _attn_c(qc, kc, vc, kcx, vcx, tq):
    b, s, _ = qc.shape
    c = kcx.shape[1]

    def qmap(bb, p, i):
        return (bb, i, p)

    def kmap(bb, p, i):
        return (bb, 0, p)

    return pl.pallas_call(
        _attn_c_kernel,
        grid=(b, C_HEADS // 2, s // tq),
        in_specs=[
            pl.BlockSpec((1, tq, 2 * LANES), qmap),
            pl.BlockSpec((1, s, 2 * LANES), kmap),
            pl.BlockSpec((1, c, 2 * LANES), kmap),
            pl.BlockSpec((1, s, LANES), kmap),
            pl.BlockSpec((1, c, LANES), kmap),
        ],
        out_specs=pl.BlockSpec((1, tq, LANES), qmap),
        out_shape=jax.ShapeDtypeStruct((b, s, 256), BF16),
        compiler_params=_params("arbitrary", "arbitrary", "arbitrary"),
        name="attn_latent",
    )(qc, kc, kcx, vc, vcx)


def _attn_ctx_kernel(qa_ref, ka_ref, va_ref, qb_ref, kb_ref, vb_ref, qc_ref, kc_ref, vc_ref,
                     sink_ref, oa_ref, ob_ref, oc_ref):
    c = qa_ref.shape[1]
    ka, va = ka_ref[0], va_ref[0]
    for j in range(A_HEADS // 2):
        sl = slice(LANES * j, LANES * (j + 1))
        q = _split_pair(qa_ref[0, :, sl])
        o = _softmax_pv([(_dot_t(q, ka), va)], _sink_col(sink_ref, j, j + A_HEADS // 2, c))
        oa_ref[0, :, sl] = _merge_pair(o).astype(BF16)
    for p in range(B_HEADS // 2):
        sl = slice(LANES * p, LANES * (p + 1))
        q = _split_pair(qb_ref[0, :, sl])
        o = _softmax_pv([(_dot_t(q, kb_ref[0, :, sl]), vb_ref[0, :, sl])])
        ob_ref[0, :, sl] = _merge_pair(o).astype(BF16)
    for p in range(C_HEADS // 2):
        v = vc_ref[0, :, LANES * p:LANES * (p + 1)]
        outs = []
        for hh in range(2):
            sl = slice(LANES * (2 * p + hh), LANES * (2 * p + hh + 1))
            outs.append(_softmax_pv([(_dot_t(qc_ref[0, :, sl], kc_ref[0, :, sl]), v)]))
        oc_ref[0, :, LANES * p:LANES * (p + 1)] = _merge_pair(jnp.concatenate(outs, axis=0)).astype(BF16)


def _attn_ctx(parts, sink):
    b, c, _ = parts[0].shape
    widths = [a.shape[2] for a in parts]

    def bmap(bb):
        return (bb, 0, 0)

    return pl.pallas_call(
        _attn_ctx_kernel,
        grid=(b,),
        in_specs=[pl.BlockSpec((1, c, w), bmap) for w in widths]
        + [pl.BlockSpec((A_HEADS, LANES), lambda bb: (0, 0))],
        out_specs=[pl.BlockSpec((1, c, w), bmap) for w in (512, 256, 256)],
        out_shape=[jax.ShapeDtypeStruct((b, c, w), BF16) for w in (512, 256, 256)],
        compiler_params=_params("arbitrary"),
        name="attn_context",
    )(*parts, sink)


def _outproj_mlp_kernel(x_ref, ma_ref, mb_ref, mc_ref, mod_ref, woa_ref, wob_ref, woc_ref,
                        g2_ref, w1_ref, w2_ref, gf_ref, o_ref, *, final, ff_chunk):
    mod = mod_ref[0]
    attn = _dot(ma_ref[0], woa_ref[...]) + _dot(mb_ref[0], wob_ref[...]) + _dot(mc_ref[0], woc_ref[...])
    x1 = x_ref[0] + mod[2:3] * attn
    h = (_rms(x1, g2_ref[...]) * (1.0 + mod[4:5]) + mod[3:4]).astype(BF16)
    d_ff = w1_ref.shape[1]
    y = None
    for c0 in range(0, d_ff, ff_chunk):
        u = jnp.maximum(_dot(h, w1_ref[:, c0:c0 + ff_chunk]), 0.0)
        part = _dot((u * u).astype(BF16), w2_ref[c0:c0 + ff_chunk, :])
        y = part if y is None else y + part
    x2 = x1 + mod[5:6] * y
    if final:
        x2 = _rms(x2, gf_ref[...])
    o_ref[0] = x2


def _outproj_mlp(x, ma, mb, mc, mod, woa, wob, woc, g2, w1, w2, gf, tm, final):
    b, s, d = x.shape
    d_ff = w1.shape[1]
    per_batch_mod = mod.shape[0] > 1

    def xmap(bb, j):
        return (bb, j, 0)

    def modmap(bb, j):
        return ((bb if per_batch_mod else 0), 0, 0)

    def const2(bb, j):
        return (0, 0)

    def resident(shape):
        return pl.BlockSpec(shape, const2, pipeline_mode=pl.Buffered(1))

    return pl.pallas_call(
        functools.partial(_outproj_mlp_kernel, final=final, ff_chunk=1024),
        grid=(b, s // tm),
        in_specs=[
            pl.BlockSpec((1, tm, d), xmap),
            pl.BlockSpec((1, tm, 512), xmap),
            pl.BlockSpec((1, tm, 256), xmap),
            pl.BlockSpec((1, tm, 256), xmap),
            pl.BlockSpec((1, N_MOD, d), modmap),
            resident((512, d)), resident((256, d)), resident((256, d)),
            resident((1, d)),
            resident((d, d_ff)), resident((d_ff, d)),
            resident((1, d)),
        ],
        out_specs=pl.BlockSpec((1, tm, d), xmap),
        out_shape=jax.ShapeDtypeStruct((b, s, d), F32),
        compiler_params=_params("arbitrary", "arbitrary"),
        name="outproj_mlp",
    )(x, ma, mb, mc, mod, woa, wob, woc, g2, w1, w2, gf)


def _rope_tables(s):
    tok = jnp.arange(s)
    row, col = (tok // GRID_W).astype(F32), (tok % GRID_W).astype(F32)

    def cs(pos, half):
        freqs = ROPE_BASE ** (-jnp.arange(half, dtype=F32) / half)
        ang = pos[:, None] * freqs
        return jnp.cos(ang), jnp.sin(ang)

    cr, sr = cs(row, 16)
    cc, sc = cs(col, 16)
    z = jnp.zeros_like(sr)
    a_c = jnp.tile(jnp.concatenate([cr, cr, cc, cc], axis=1), (1, 2))
    a_s1 = jnp.tile(jnp.concatenate([-sr, z, -sc, z], axis=1), (1, 2))
    a_s2 = jnp.tile(jnp.concatenate([z, sr, z, sc], axis=1), (1, 2))
    cr, sr = cs(row, 8)
    cc, sc = cs(col, 8)
    z = jnp.zeros_like(sr)
    one64, zero64 = jnp.ones((s, C_NOPE), F32), jnp.zeros((s, C_NOPE), F32)
    one32, zero32 = jnp.ones((s, 32), F32), jnp.zeros((s, 32), F32)
    c_c = jnp.concatenate([one64, cr, cr, cc, cc, one32], axis=1)
    c_s1 = jnp.concatenate([zero64, -sr, z, -sc, z, zero32], axis=1)
    c_s2 = jnp.concatenate([zero64, z, sr, z, sc, zero32], axis=1)
    return (a_c, a_s1, a_s2, c_c, c_s1, c_s2)


def _identity_tables(s):
    one, zero = jnp.ones((s, LANES), F32), jnp.zeros((s, LANES), F32)
    return (one, zero, zero, one, zero, zero)


def _pair_order():
    idx = []
    for j in range(A_HEADS // 2):
        for hd in (j, j + A_HEADS // 2):
            idx.extend(range(hd * HEAD_DIM, (hd + 1) * HEAD_DIM))
    return jnp.asarray(idx, dtype=jnp.int32)


def _layer_weights(w_in, w_uq, w_ukv, w_out):
    d = w_in.shape[0]
    perm = _pair_order()
    qa_w = w_in[:, :512][:, perm]
    z = lambda n: jnp.zeros((d, n), w_in.dtype)
    kr_w = jnp.concatenate([z(64), w_in[:, 1920:1952], z(32)], axis=1)
    win = jnp.concatenate([qa_w, w_in[:, 512:1920], kr_w], axis=1).astype(BF16)
    hq = C_NOPE + C_ROPE
    wuq = jnp.concatenate(
        [jnp.concatenate([w_uq[:, hq * h:hq * (h + 1)], jnp.zeros((C_Q_RANK, LANES - hq), w_uq.dtype)], axis=1)
         for h in range(C_HEADS)], axis=1).astype(BF16)
    hk = C_NOPE + C_V
    wuk = jnp.concatenate(
        [jnp.concatenate([w_ukv[:, hk * h:hk * h + C_NOPE], jnp.zeros((C_KV_RANK, LANES - C_NOPE), w_ukv.dtype)], axis=1)
         for h in range(C_HEADS)], axis=1).astype(BF16)
    wuv = jnp.concatenate([w_ukv[:, hk * h + C_NOPE:hk * (h + 1)] for h in range(C_HEADS)], axis=1).astype(BF16)
    woa = w_out[:512][perm].astype(BF16)
    wob = w_out[512:768].astype(BF16)
    woc = w_out[768:].astype(BF16)
    return win, wuq, wuk, wuv, woa, wob, woc


def kernel(x, c, ctx, c_ctx, w_ada, b_ada, norm1_g, norm2_g, w_in, attn_sink, na_rpb, mla_q_norm_g,
           mla_w_uq, mla_kv_norm_g, mla_w_ukv, w_out, w_mlp_in, w_mlp_out, final_norm_g):
    b, s, d = x.shape
    n_ctx = ctx.shape[1]
    depth = w_ada.shape[0]
    assert b + 1 <= MOD_ROWS and s % 512 == 0 and n_ctx % 128 == 0
    tm_x = 512
    tm_c = min(n_ctx, 256)
    tq_c = 256

    cvec = jnp.concatenate([c, c_ctx[None], jnp.zeros((MOD_ROWS - b - 1, d), c.dtype)], axis=0)
    mods = _modulation(cvec, w_ada, b_ada).reshape(depth, MOD_ROWS, N_MOD, d)
    tabs_x = _rope_tables(s)
    tabs_c = _identity_tables(n_ctx)
    gf = final_norm_g.reshape(1, d)

    for l in range(depth):
        last = l == depth - 1
        mod_x, mod_c = mods[l, :b], mods[l, b:b + 1]
        win, wuq, wuk, wuv, woa, wob, woc = _layer_weights(w_in[l], mla_w_uq[l], mla_w_ukv[l], w_out[l])
        g1, g2 = norm1_g[l].reshape(1, d), norm2_g[l].reshape(1, d)
        gq, gkv = mla_q_norm_g[l].reshape(1, -1), mla_kv_norm_g[l].reshape(1, -1)
        w1, w2 = w_mlp_in[l].astype(BF16), w_mlp_out[l].astype(BF16)
        sink = jnp.broadcast_to(attn_sink[l][:, None], (A_HEADS, LANES)).astype(F32)

        xs = _inproj(x, mod_x, g1, win, gq, wuq, gkv, wuk, wuv, tabs_x, tm_x)
        cs = _inproj(ctx, mod_c, g1, win, gq, wuq, gkv, wuk, wuv, tabs_c, tm_c)
        qa, ka, va, qb, kb, vb, qc, kc, vc = xs
        cqa, cka, cva, cqb, ckb, cvb, cqc, ckc, cvc = cs

        oa = _attn_a(qa, ka, va, cka, cva, sink)
        ob = _attn_b(qb, kb, vb, ckb, cvb, _na_bias_table(na_rpb[l]))
        oc = _attn_c(qc, kc, vc, ckc, cvc, tq_c)
        x = _outproj_mlp(x, oa, ob, oc, mod_x, woa, wob, woc, g2, w1, w2, gf, tm_x, last)
        if not last:
            coa, cob, coc = _attn_ctx(cs, sink)
            ctx = _outproj_mlp(ctx, coa, cob, coc, mod_c, woa, wob, woc, g2, w1, w2, gf, tm_c, False)
    return x
```

```python
import functools
import math

import jax
import jax.numpy as jnp
from jax import lax
from jax.experimental import pallas as pl
from jax.experimental.pallas import tpu as pltpu

F32 = jnp.float32
BF16 = jnp.bfloat16

GRID_W = 64
HEAD_DIM = 64
A_HEADS = 8
A_KV_HEADS = 2
A_GROUP = A_HEADS // A_KV_HEADS
A_WINDOW = 128
B_HEADS = 4
NA_ROWS = 8
NA_COLS = 16
C_HEADS = 4
C_Q_RANK = 256
C_KV_RANK = 128
C_NOPE = 64
C_ROPE = 32
C_V = 64
N_MOD = 6
ROPE_BASE = 10000.0
EPS = 1e-6
NEG_INF = -1e30
LOG2E = math.log2(math.e)

LANES = 128
MOD_ROWS = 16
IN_COLS = 2048
VMEM_LIMIT = 56 * 1024 * 1024

_OFF_QA, _OFF_KA, _OFF_VA = 0, 512, 640
_OFF_QB, _OFF_KB, _OFF_VB = 768, 1024, 1280
_OFF_CQ, _OFF_CKV, _OFF_KR = 1536, 1792, 1920


def _dot(a, b):
    return jnp.dot(a, b, preferred_element_type=F32)


def _dot_t(a, b):
    return lax.dot_general(a, b, (((1,), (1,)), ((), ())), preferred_element_type=F32)


def _params(*sem):
    return pltpu.CompilerParams(dimension_semantics=sem, vmem_limit_bytes=VMEM_LIMIT)


def _mod_kernel(c_ref, w_ref, b_ref, o_ref):
    c = c_ref[...]
    s = c * (1.0 / (1.0 + jnp.exp(-c)))
    o_ref[0] = jnp.dot(s, w_ref[0], preferred_element_type=F32,
                       precision=lax.Precision.HIGHEST) + b_ref[0]


def _modulation(cvec, w_ada, b_ada):
    depth, d, n = w_ada.shape
    tn = 1024
    return pl.pallas_call(
        _mod_kernel,
        grid=(depth, n // tn),
        in_specs=[
            pl.BlockSpec((MOD_ROWS, d), lambda l, j: (0, 0)),
            pl.BlockSpec((1, d, tn), lambda l, j: (l, 0, j)),
            pl.BlockSpec((1, 1, tn), lambda l, j: (l, 0, j)),
        ],
        out_specs=pl.BlockSpec((1, MOD_ROWS, tn), lambda l, j: (l, 0, j)),
        out_shape=jax.ShapeDtypeStruct((depth, MOD_ROWS, n), F32),
        compiler_params=_params("arbitrary", "arbitrary"),
        name="adaln_modulation",
    )(cvec, w_ada, b_ada.reshape(depth, 1, n))


def _rope(x, c, s1, s2, shift):
    return x * c + pltpu.roll(x, LANES - shift, 1) * s1 + pltpu.roll(x, shift, 1) * s2


def _rms(x, g):
    return x * lax.rsqrt(jnp.mean(x * x, axis=-1, keepdims=True) + EPS) * g


def _inproj_kernel(x_ref, mod_ref, g1_ref, win_ref, gq_ref, wuq_ref, gkv_ref, wuk_ref, wuv_ref,
                   ac_ref, as1_ref, as2_ref, cc_ref, cs1_ref, cs2_ref,
                   qa_ref, ka_ref, va_ref, qb_ref, kb_ref, vb_ref, qc_ref, kc_ref, vc_ref):
    x = x_ref[0]
    mod = mod_ref[0]
    h = _rms(x, g1_ref[...]) * (1.0 + mod[1:2]) + mod[0:1]
    p = _dot(h.astype(BF16), win_ref[...])

    ac, as1, as2 = ac_ref[...], as1_ref[...], as2_ref[...]
    cc, cs1, cs2 = cc_ref[...], cs1_ref[...], cs2_ref[...]
    scale_ab = HEAD_DIM ** -0.5 * LOG2E
    scale_c = (C_NOPE + C_ROPE) ** -0.5 * LOG2E

    for j in range(4):
        blk = p[:, _OFF_QA + LANES * j:_OFF_QA + LANES * (j + 1)]
        qa_ref[0, :, LANES * j:LANES * (j + 1)] = (_rope(blk, ac, as1, as2, 16) * scale_ab).astype(BF16)
    ka_ref[0] = _rope(p[:, _OFF_KA:_OFF_KA + LANES], ac, as1, as2, 16).astype(BF16)
    va = p[:, _OFF_VA:_OFF_VA + LANES]
    lane = lax.broadcasted_iota(jnp.int32, va.shape, 1)
    va_ref[0, :, 0:LANES] = jnp.where(lane < HEAD_DIM, va, jnp.where(lane == HEAD_DIM, 1.0, 0.0)).astype(BF16)
    va_ref[0, :, LANES:2 * LANES] = jnp.where(lane >= HEAD_DIM, va, jnp.where(lane == 0, 1.0, 0.0)).astype(BF16)
    qb_ref[0] = (p[:, _OFF_QB:_OFF_QB + 256] * scale_ab).astype(BF16)
    kb_ref[0] = p[:, _OFF_KB:_OFF_KB + 256].astype(BF16)
    vb_ref[0] = p[:, _OFF_VB:_OFF_VB + 256].astype(BF16)

    cq = _rms(p[:, _OFF_CQ:_OFF_CQ + C_Q_RANK], gq_ref[...]).astype(BF16)
    q_up = _dot(cq, wuq_ref[...])
    ckv = _rms(p[:, _OFF_CKV:_OFF_CKV + C_KV_RANK], gkv_ref[...]).astype(BF16)
    k_up = _dot(ckv, wuk_ref[...])
    v_up = _dot(ckv, wuv_ref[...])
    kr = _rope(p[:, _OFF_KR:_OFF_KR + LANES], cc, cs1, cs2, 8)
    ones_lane = (lax.broadcasted_iota(jnp.int32, (1, LANES), 1) == C_V).astype(F32)
    for hd in range(C_HEADS):
        sl = slice(LANES * hd, LANES * (hd + 1))
        qc_ref[0, :, sl] = (_rope(q_up[:, sl], cc, cs1, cs2, 8) * scale_c).astype(BF16)
        kc_ref[0, :, sl] = (k_up[:, sl] + kr).astype(BF16)
        vc_ref[0, :, sl] = (v_up[:, sl] + ones_lane).astype(BF16)


def _inproj(x, mod, g1, win, gq, wuq, gkv, wuk, wuv, tabs, tm):
    b, s, d = x.shape
    nt = s // tm
    per_batch_mod = mod.shape[0] > 1

    def xmap(j, bb):
        return (bb, j, 0)

    def modmap(j, bb):
        return ((bb if per_batch_mod else 0), 0, 0)

    def const2(j, bb):
        return (0, 0)

    def tabmap(j, bb):
        return (j, 0)

    widths = (512, 128, 256, 256, 256, 256, 512, 512, 512)
    in_specs = [
        pl.BlockSpec((1, tm, d), xmap),
        pl.BlockSpec((1, N_MOD, d), modmap),
        pl.BlockSpec((1, d), const2),
        pl.BlockSpec((d, IN_COLS), const2),
        pl.BlockSpec((1, C_Q_RANK), const2),
        pl.BlockSpec((C_Q_RANK, 512), const2),
        pl.BlockSpec((1, C_KV_RANK), const2),
        pl.BlockSpec((C_KV_RANK, 512), const2),
        pl.BlockSpec((C_KV_RANK, 512), const2),
    ] + [pl.BlockSpec((tm, LANES), tabmap) for _ in range(6)]
    return pl.pallas_call(
        _inproj_kernel,
        grid=(nt, b),
        in_specs=in_specs,
        out_specs=[pl.BlockSpec((1, tm, w), xmap) for w in widths],
        out_shape=[jax.ShapeDtypeStruct((b, s, w), BF16) for w in widths],
        compiler_params=_params("arbitrary", "arbitrary"),
        name="norm_inproj_rope",
    )(x, mod, g1, win, gq, wuq, gkv, wuk, wuv, *tabs)


def _lane_lo(shape):
    return lax.broadcasted_iota(jnp.int32, shape, 1) < HEAD_DIM


def _split_pair(q2):
    lo = _lane_lo(q2.shape)
    zero = jnp.zeros_like(q2)
    return jnp.concatenate([jnp.where(lo, q2, zero), jnp.where(lo, zero, q2)], axis=0)


def _merge_pair(o):
    n = o.shape[0] // 2
    return jnp.where(_lane_lo((n, o.shape[1])), o[:n], o[n:])


def _softmax_pv(parts, sink=None):
    m = parts[0][0].max(axis=-1, keepdims=True)
    for s, _ in parts[1:]:
        m = jnp.maximum(m, s.max(axis=-1, keepdims=True))
    if sink is not None:
        m = jnp.maximum(m, sink)
    denom = None
    acc = None
    for s, v in parts:
        e = jnp.exp2(s - m)
        r = e.sum(axis=-1, keepdims=True)
        denom = r if denom is None else denom + r
        o = _dot(e.astype(BF16), v)
        acc = o if acc is None else acc + o
    if sink is not None:
        denom = denom + jnp.exp2(sink - m)
    return acc / denom


def _attn_a_kernel(q_ref, k_ref, v_ref, kx_ref, vx_ref, sink_ref, o_ref, *, tq, seq):
    t = pl.program_id(1)
    nloc = tq + 2 * A_WINDOW
    qstart = t * tq
    start = pl.multiple_of(jnp.clip(qstart - A_WINDOW, 0, seq - nloc), A_WINDOW)
    ks = k_ref[0, pl.ds(start, nloc), :]
    kx = kx_ref[0]
    rel = (lax.broadcasted_iota(jnp.int32, (tq, nloc), 1) - lax.broadcasted_iota(jnp.int32, (tq, nloc), 0)
           + (start - qstart))
    band = jnp.where(jnp.abs(rel) <= A_WINDOW, 0.0, NEG_INF)
    lo = _lane_lo((tq, LANES))
    units = [(j, g) for j in range(A_GROUP) for g in range(A_KV_HEADS)]

    def scores(u):
        j, g = units[u]
        q2 = q_ref[0, :, LANES * j:LANES * (j + 1)]
        zero = jnp.zeros_like(q2)
        q = jnp.where(lo, q2, zero) if g == 0 else jnp.where(lo, zero, q2)
        return _dot_t(q, ks) + band, _dot_t(q, kx)

    outs = {}
    nxt = scores(0)
    for u, (j, g) in enumerate(units):
        s_loc, s_ctx = nxt
        if u + 1 < len(units):
            nxt = scores(u + 1)
        vsg = v_ref[0, pl.ds(start, nloc), LANES * g:LANES * (g + 1)]
        vxg = vx_ref[0, :, LANES * g:LANES * (g + 1)]
        hd = g * A_GROUP + j
        sink = jnp.broadcast_to(sink_ref[hd:hd + 1, 0:1], (tq, 1))
        m = jnp.maximum(jnp.maximum(s_loc.max(axis=-1, keepdims=True), s_ctx.max(axis=-1, keepdims=True)), sink)
        o = _dot(jnp.exp2(s_loc - m).astype(BF16), vsg) + _dot(jnp.exp2(s_ctx - m).astype(BF16), vxg)
        denom = (o[:, HEAD_DIM:HEAD_DIM + 1] if g == 0 else o[:, 0:1]) + jnp.exp2(sink - m)
        outs[(j, g)] = o / denom
        if g == A_KV_HEADS - 1:
            o_ref[0, :, LANES * j:LANES * (j + 1)] = jnp.where(lo, outs[(j, 0)], outs[(j, 1)]).astype(BF16)


def _attn_a(qa, ka, va, kax, vax, sink, tq):
    b, s, _ = qa.shape
    c = kax.shape[1]

    def qmap(bb, t):
        return (bb, t, 0)

    def kmap(bb, t):
        return (bb, 0, 0)

    return pl.pallas_call(
        functools.partial(_attn_a_kernel, tq=tq, seq=s),
        grid=(b, s // tq),
        in_specs=[
            pl.BlockSpec((1, tq, 512), qmap),
            pl.BlockSpec((1, s, LANES), kmap),
            pl.BlockSpec((1, s, 2 * LANES), kmap),
            pl.BlockSpec((1, c, LANES), kmap),
            pl.BlockSpec((1, c, 2 * LANES), kmap),
            pl.BlockSpec((A_HEADS, LANES), lambda bb, t: (0, 0)),
        ],
        out_specs=pl.BlockSpec((1, tq, 512), qmap),
        out_shape=jax.ShapeDtypeStruct((b, s, 512), BF16),
        compiler_params=_params("arbitrary", "arbitrary"),
        name="attn_window_gqa",
    )(qa, ka, va, kax, vax, sink)


def _attn_b_kernel(q_ref, k_ref, v_ref, kx_ref, vx_ref, bias_ref, o_ref, *, rows):
    rb = pl.program_id(2)
    kx, vx = kx_ref[0], vx_ref[0]

    def scores(r):
        qr = rb * NA_ROWS + r
        wr = jnp.clip(qr - NA_ROWS // 2, 0, rows - NA_ROWS)
        oi = wr - qr + (NA_ROWS - 1)
        start = pl.multiple_of(wr * GRID_W, GRID_W)
        q = _split_pair(q_ref[0, r * GRID_W:(r + 1) * GRID_W, :])
        ks = k_ref[0, pl.ds(start, NA_ROWS * GRID_W), :]
        bias = jnp.concatenate([bias_ref[0, oi], bias_ref[0, NA_ROWS + oi]], axis=0)
        return _dot_t(q, ks) + bias, _dot_t(q, kx), start

    nxt = scores(0)
    for r in range(NA_ROWS):
        s_loc, s_ctx, start = nxt
        if r + 1 < NA_ROWS:
            nxt = scores(r + 1)
        vs = v_ref[0, pl.ds(start, NA_ROWS * GRID_W), :]
        o = _softmax_pv([(s_loc, vs), (s_ctx, vx)])
        o_ref[0, r * GRID_W:(r + 1) * GRID_W, :] = _merge_pair(o).astype(BF16)


def _attn_b(qb, kb, vb, kbx, vbx, bias):
    b, s, _ = qb.shape
    c = kbx.shape[1]
    rows = s // GRID_W
    tq = NA_ROWS * GRID_W

    def qmap(bb, p, rb):
        return (bb, rb, p)

    def kmap(bb, p, rb):
        return (bb, 0, p)

    return pl.pallas_call(
        functools.partial(_attn_b_kernel, rows=rows),
        grid=(b, B_HEADS // 2, s // tq),
        in_specs=[
            pl.BlockSpec((1, tq, LANES), qmap),
            pl.BlockSpec((1, s, LANES), kmap),
            pl.BlockSpec((1, s, LANES), kmap),
            pl.BlockSpec((1, c, LANES), kmap),
            pl.BlockSpec((1, c, LANES), kmap),
            pl.BlockSpec((1, 2 * NA_ROWS, GRID_W, NA_ROWS * GRID_W), lambda bb, p, rb: (p, 0, 0, 0)),
        ],
        out_specs=pl.BlockSpec((1, tq, LANES), qmap),
        out_shape=jax.ShapeDtypeStruct((b, s, 256), BF16),
        compiler_params=_params("arbitrary", "arbitrary", "arbitrary"),
        name="attn_neighbourhood",
    )(qb, kb, vb, kbx, vbx, bias)


def _na_bias_table(rpb):
    h = rpb.shape[0]
    oi = jnp.arange(NA_ROWS)[:, None] + jnp.arange(NA_ROWS)[None, :]
    qc = jnp.arange(GRID_W)[:, None]
    kc = jnp.arange(GRID_W)[None, :]
    dc = jnp.clip(kc - qc, 1 - NA_COLS, NA_COLS - 1) + NA_COLS - 1
    wc = jnp.clip(qc - NA_COLS // 2, 0, GRID_W - NA_COLS)
    valid = (kc >= wc) & (kc < wc + NA_COLS)
    t = rpb[:, oi][:, :, :, dc] * LOG2E
    t = jnp.where(valid[None, None, None], t, NEG_INF)
    t = t.transpose(0, 1, 3, 2, 4).reshape(h // 2, 2 * NA_ROWS, GRID_W, NA_ROWS * GRID_W)
    return t.astype(F32)


def _attn_c_kernel(q_ref, k_ref, kx_ref, v_ref, vx_ref, o_ref, *, kc):
    s = k_ref.shape[1]
    chunks = [(k_ref, v_ref, c0, kc) for c0 in range(0, s, kc)] + [(kx_ref, vx_ref, 0, kx_ref.shape[1])]
    qs = [q_ref[0, :, LANES * hh:LANES * (hh + 1)] for hh in range(2)]
    m = [None, None]
    acc = [None, None]

    def scores(ci):
        kr, _, c0, n = chunks[ci]
        return [_dot_t(qs[hh], kr[0, c0:c0 + n, LANES * hh:LANES * (hh + 1)]) for hh in range(2)]

    nxt = scores(0)
    for ci, (_, vr, c0, n) in enumerate(chunks):
        cur = nxt
        if ci + 1 < len(chunks):
            nxt = scores(ci + 1)
        for hh in range(2):
            sl = slice(LANES * hh, LANES * (hh + 1))
            sc = cur[hh]
            cm = sc.max(axis=-1, keepdims=True)
            if m[hh] is None:
                m[hh] = cm
                acc[hh] = _dot(jnp.exp2(sc - cm).astype(BF16), vr[0, c0:c0 + n, sl])
            else:
                m_new = jnp.maximum(m[hh], cm)
                acc[hh] = (jnp.exp2(m[hh] - m_new) * acc[hh]
                           + _dot(jnp.exp2(sc - m_new).astype(BF16), vr[0, c0:c0 + n, sl]))
                m[hh] = m_new
    o0 = acc[0] / acc[0][:, C_V:C_V + 1]
    o1 = acc[1] / acc[1][:, C_V:C_V + 1]
    o_ref[0] = jnp.where(_lane_lo(o0.shape), o0, pltpu.roll(o1, C_V, 1)).astype(BF16)


def _attn_c(qc, kc, vc, kcx, vcx, tq):
    b, s, _ = qc.shape
    c = kcx.shape[1]

    def qmap(bb, p, i):
        return (bb, i, p)

    def kmap(bb, p, i):
        return (bb, 0, p)

    return pl.pallas_call(
        functools.partial(_attn_c_kernel, kc=512),
        grid=(b, C_HEADS // 2, s // tq),
        in_specs=[
            pl.BlockSpec((1, tq, 2 * LANES), qmap),
            pl.BlockSpec((1, s, 2 * LANES), kmap),
            pl.BlockSpec((1, c, 2 * LANES), kmap),
            pl.BlockSpec((1, s, 2 * LANES), kmap),
            pl.BlockSpec((1, c, 2 * LANES), kmap),
        ],
        out_specs=pl.BlockSpec((1, tq, LANES), qmap),
        out_shape=jax.ShapeDtypeStruct((b, s, 256), BF16),
        compiler_params=_params("arbitrary", "arbitrary", "arbitrary"),
        name="attn_latent",
    )(qc, kc, kcx, vc, vcx)


def _attn_ctx_kernel(qa_ref, ka_ref, va_ref, qb_ref, kb_ref, vb_ref, qc_ref, kc_ref, vc_ref,
                     sink_ref, oa_ref, ob_ref, oc_ref):
    c = qa_ref.shape[1]
    ka = ka_ref[0]
    half = A_HEADS // 2
    for j in range(half):
        sl = slice(LANES * j, LANES * (j + 1))
        q = _split_pair(qa_ref[0, :, sl])
        outs = []
        for g in range(A_KV_HEADS):
            sink = jnp.broadcast_to(sink_ref[j + half * g:j + half * g + 1, 0:1], (c, 1))
            outs.append(_softmax_pv([(_dot_t(q[c * g:c * (g + 1)], ka), va_ref[0, :, LANES * g:LANES * (g + 1)])], sink))
        oa_ref[0, :, sl] = jnp.where(_lane_lo(outs[0].shape), outs[0], outs[1]).astype(BF16)
    for p in range(B_HEADS // 2):
        sl = slice(LANES * p, LANES * (p + 1))
        q = _split_pair(qb_ref[0, :, sl])
        o = _softmax_pv([(_dot_t(q, kb_ref[0, :, sl]), vb_ref[0, :, sl])])
        ob_ref[0, :, sl] = _merge_pair(o).astype(BF16)
    for p in range(C_HEADS // 2):
        outs = []
        for hh in range(2):
            sl = slice(LANES * (2 * p + hh), LANES * (2 * p + hh + 1))
            outs.append(_softmax_pv([(_dot_t(qc_ref[0, :, sl], kc_ref[0, :, sl]), vc_ref[0, :, sl])]))
        oc_ref[0, :, LANES * p:LANES * (p + 1)] = jnp.where(
            _lane_lo(outs[0].shape), outs[0], pltpu.roll(outs[1], C_V, 1)).astype(BF16)


def _attn_ctx(parts, sink):
    b, c, _ = parts[0].shape
    widths = [a.shape[2] for a in parts]

    def bmap(bb):
        return (bb, 0, 0)

    return pl.pallas_call(
        _attn_ctx_kernel,
        grid=(b,),
        in_specs=[pl.BlockSpec((1, c, w), bmap) for w in widths]
        + [pl.BlockSpec((A_HEADS, LANES), lambda bb: (0, 0))],
        out_specs=[pl.BlockSpec((1, c, w), bmap) for w in (512, 256, 256)],
        out_shape=[jax.ShapeDtypeStruct((b, c, w), BF16) for w in (512, 256, 256)],
        compiler_params=_params("arbitrary"),
        name="attn_context",
    )(*parts, sink)


def _outproj_mlp_kernel(x_ref, ma_ref, mb_ref, mc_ref, mod_ref, woa_ref, wob_ref, woc_ref,
                        g2_ref, w1_ref, w2_ref, gf_ref, o_ref, *, final, ff_chunk):
    mod = mod_ref[0]
    attn = _dot(ma_ref[0], woa_ref[...]) + _dot(mb_ref[0], wob_ref[...]) + _dot(mc_ref[0], woc_ref[...])
    x1 = x_ref[0] + mod[2:3] * attn
    h = (_rms(x1, g2_ref[...]) * (1.0 + mod[4:5]) + mod[3:4]).astype(BF16)
    d_ff = w1_ref.shape[1]
    y = None
    for c0 in range(0, d_ff, ff_chunk):
        u = jnp.maximum(_dot(h, w1_ref[:, c0:c0 + ff_chunk]), 0.0)
        part = _dot((u * u).astype(BF16), w2_ref[c0:c0 + ff_chunk, :])
        y = part if y is None else y + part
    x2 = x1 + mod[5:6] * y
    if final:
        x2 = _rms(x2, gf_ref[...])
    o_ref[0] = x2


def _outproj_mlp(x, ma, mb, mc, mod, woa, wob, woc, g2, w1, w2, gf, tm, final):
    b, s, d = x.shape
    d_ff = w1.shape[1]
    per_batch_mod = mod.shape[0] > 1

    def xmap(bb, j):
        return (bb, j, 0)

    def modmap(bb, j):
        return ((bb if per_batch_mod else 0), 0, 0)

    def const2(bb, j):
        return (0, 0)

    def resident(shape):
        return pl.BlockSpec(shape, const2, pipeline_mode=pl.Buffered(1))

    return pl.pallas_call(
        functools.partial(_outproj_mlp_kernel, final=final, ff_chunk=1024),
        grid=(b, s // tm),
        in_specs=[
            pl.BlockSpec((1, tm, d), xmap),
            pl.BlockSpec((1, tm, 512), xmap),
            pl.BlockSpec((1, tm, 256), xmap),
            pl.BlockSpec((1, tm, 256), xmap),
            pl.BlockSpec((1, N_MOD, d), modmap),
            resident((512, d)), resident((256, d)), resident((256, d)),
            resident((1, d)),
            resident((d, d_ff)), resident((d_ff, d)),
            resident((1, d)),
        ],
        out_specs=pl.BlockSpec((1, tm, d), xmap),
        out_shape=jax.ShapeDtypeStruct((b, s, d), F32),
        compiler_params=_params("arbitrary", "arbitrary"),
        name="outproj_mlp",
    )(x, ma, mb, mc, mod, woa, wob, woc, g2, w1, w2, gf)


def _rope_tables(s):
    tok = jnp.arange(s)
    row, col = (tok // GRID_W).astype(F32), (tok % GRID_W).astype(F32)

    def cs(pos, half):
        freqs = ROPE_BASE ** (-jnp.arange(half, dtype=F32) / half)
        ang = pos[:, None] * freqs
        return jnp.cos(ang), jnp.sin(ang)

    cr, sr = cs(row, 16)
    cc, sc = cs(col, 16)
    z = jnp.zeros_like(sr)
    a_c = jnp.tile(jnp.concatenate([cr, cr, cc, cc], axis=1), (1, 2))
    a_s1 = jnp.tile(jnp.concatenate([-sr, z, -sc, z], axis=1), (1, 2))
    a_s2 = jnp.tile(jnp.concatenate([z, sr, z, sc], axis=1), (1, 2))
    cr, sr = cs(row, 8)
    cc, sc = cs(col, 8)
    z = jnp.zeros_like(sr)
    one64, zero64 = jnp.ones((s, C_NOPE), F32), jnp.zeros((s, C_NOPE), F32)
    one32, zero32 = jnp.ones((s, 32), F32), jnp.zeros((s, 32), F32)
    c_c = jnp.concatenate([one64, cr, cr, cc, cc, one32], axis=1)
    c_s1 = jnp.concatenate([zero64, -sr, z, -sc, z, zero32], axis=1)
    c_s2 = jnp.concatenate([zero64, z, sr, z, sc, zero32], axis=1)
    return (a_c, a_s1, a_s2, c_c, c_s1, c_s2)


def _identity_tables(s):
    one, zero = jnp.ones((s, LANES), F32), jnp.zeros((s, LANES), F32)
    return (one, zero, zero, one, zero, zero)


def _pair_order():
    idx = []
    for j in range(A_HEADS // 2):
        for hd in (j, j + A_HEADS // 2):
            idx.extend(range(hd * HEAD_DIM, (hd + 1) * HEAD_DIM))
    return jnp.asarray(idx, dtype=jnp.int32)


def _pad_heads(w, offs, width):
    pad = jnp.zeros((w.shape[0], LANES - width), w.dtype)
    return jnp.concatenate([jnp.concatenate([w[:, o:o + width], pad], axis=1) for o in offs], axis=1)


def _layer_weights(w_in, w_uq, w_ukv, w_out):
    d = w_in.shape[0]
    perm = _pair_order()
    qa_w = w_in[:, :512][:, perm]
    z = lambda n: jnp.zeros((d, n), w_in.dtype)
    kr_w = jnp.concatenate([z(64), w_in[:, 1920:1952], z(32)], axis=1)
    win = jnp.concatenate([qa_w, w_in[:, 512:1920], kr_w], axis=1).astype(BF16)
    hq, hk = C_NOPE + C_ROPE, C_NOPE + C_V
    wuq = _pad_heads(w_uq, [hq * h for h in range(C_HEADS)], hq).astype(BF16)
    wuk = _pad_heads(w_ukv, [hk * h for h in range(C_HEADS)], C_NOPE).astype(BF16)
    wuv = _pad_heads(w_ukv, [hk * h + C_NOPE for h in range(C_HEADS)], C_V).astype(BF16)
    woa = w_out[:512][perm].astype(BF16)
    wob = w_out[512:768].astype(BF16)
    woc = w_out[768:].astype(BF16)
    return win, wuq, wuk, wuv, woa, wob, woc


def kernel(x, c, ctx, c_ctx, w_ada, b_ada, norm1_g, norm2_g, w_in, attn_sink, na_rpb, mla_q_norm_g,
           mla_w_uq, mla_kv_norm_g, mla_w_ukv, w_out, w_mlp_in, w_mlp_out, final_norm_g):
    b, s, d = x.shape
    n_ctx = ctx.shape[1]
    depth = w_ada.shape[0]
    assert b + 1 <= MOD_ROWS and s % 512 == 0 and n_ctx % 128 == 0
    tm_x = 512
    tm_c = min(n_ctx, 256)
    tq_a = 256
    tq_c = 256

    cvec = jnp.concatenate([c, c_ctx[None], jnp.zeros((MOD_ROWS - b - 1, d), c.dtype)], axis=0)
    mods = _modulation(cvec, w_ada, b_ada).reshape(depth, MOD_ROWS, N_MOD, d)
    tabs_x = _rope_tables(s)
    tabs_c = _identity_tables(n_ctx)
    gf = final_norm_g.reshape(1, d)

    for l in range(depth):
        last = l == depth - 1
        mod_x, mod_c = mods[l, :b], mods[l, b:b + 1]
        win, wuq, wuk, wuv, woa, wob, woc = _layer_weights(w_in[l], mla_w_uq[l], mla_w_ukv[l], w_out[l])
        g1, g2 = norm1_g[l].reshape(1, d), norm2_g[l].reshape(1, d)
        gq, gkv = mla_q_norm_g[l].reshape(1, -1), mla_kv_norm_g[l].reshape(1, -1)
        w1, w2 = w_mlp_in[l].astype(BF16), w_mlp_out[l].astype(BF16)
        sink = jnp.broadcast_to((attn_sink[l] * LOG2E)[:, None], (A_HEADS, LANES)).astype(F32)

        xs = _inproj(x, mod_x, g1, win, gq, wuq, gkv, wuk, wuv, tabs_x, tm_x)
        cs = _inproj(ctx, mod_c, g1, win, gq, wuq, gkv, wuk, wuv, tabs_c, tm_c)
        qa, ka, va, qb, kb, vb, qc, kc, vc = xs
        cqa, cka, cva, cqb, ckb, cvb, cqc, ckc, cvc = cs

        oa = _attn_a(qa, ka, va, cka, cva, sink, tq_a)
        ob = _attn_b(qb, kb, vb, ckb, cvb, _na_bias_table(na_rpb[l]))
        oc = _attn_c(qc, kc, vc, ckc, cvc, tq_c)
        x = _outproj_mlp(x, oa, ob, oc, mod_x, woa, wob, woc, g2, w1, w2, gf, tm_x, last)
        if not last:
            coa, cob, coc = _attn_ctx(cs, sink)
            ctx = _outproj_mlp(ctx, coa, cob, coc, mod_c, woa, wob, woc, g2, w1, w2, gf, tm_c, False)
    return x
```

```python
import functools
import math

import jax
import jax.numpy as jnp
from jax import lax
from jax.experimental import pallas as pl
from jax.experimental.pallas import tpu as pltpu

F32 = jnp.float32
BF16 = jnp.bfloat16

GRID_W = 64
HEAD_DIM = 64
A_HEADS = 8
A_KV_HEADS = 2
A_GROUP = A_HEADS // A_KV_HEADS
A_WINDOW = 128
B_HEADS = 4
NA_ROWS = 8
NA_COLS = 16
C_HEADS = 4
C_Q_RANK = 256
C_KV_RANK = 128
C_NOPE = 64
C_ROPE = 32
C_V = 64
N_MOD = 6
ROPE_BASE = 10000.0
EPS = 1e-6
NEG_INF = -1e30
LOG2E = math.log2(math.e)

LANES = 128
MOD_ROWS = 16
IN_COLS = 2048
VMEM_LIMIT = 56 * 1024 * 1024

_OFF_QA, _OFF_KA, _OFF_VA = 0, 512, 640
_OFF_QB, _OFF_KB, _OFF_VB = 768, 1024, 1280
_OFF_CQ, _OFF_CKV, _OFF_KR = 1536, 1792, 1920


def _dot(a, b):
    return jnp.dot(a, b, preferred_element_type=F32)


def _dot_t(a, b):
    return lax.dot_general(a, b, (((1,), (1,)), ((), ())), preferred_element_type=F32)


def _params(*sem):
    return pltpu.CompilerParams(dimension_semantics=sem, vmem_limit_bytes=VMEM_LIMIT)


def _mod_kernel(c_ref, w_ref, b_ref, o_ref):
    c = c_ref[...]
    s = c * (1.0 / (1.0 + jnp.exp(-c)))
    o_ref[0] = jnp.dot(s, w_ref[0], preferred_element_type=F32,
                       precision=lax.Precision.HIGHEST) + b_ref[0]


def _modulation(cvec, w_ada, b_ada):
    depth, d, n = w_ada.shape
    tn = 1024
    return pl.pallas_call(
        _mod_kernel,
        grid=(depth, n // tn),
        in_specs=[
            pl.BlockSpec((MOD_ROWS, d), lambda l, j: (0, 0)),
            pl.BlockSpec((1, d, tn), lambda l, j: (l, 0, j)),
            pl.BlockSpec((1, 1, tn), lambda l, j: (l, 0, j)),
        ],
        out_specs=pl.BlockSpec((1, MOD_ROWS, tn), lambda l, j: (l, 0, j)),
        out_shape=jax.ShapeDtypeStruct((depth, MOD_ROWS, n), F32),
        compiler_params=_params("arbitrary", "arbitrary"),
        name="adaln_modulation",
    )(cvec, w_ada, b_ada.reshape(depth, 1, n))


def _rope(x, c, s1, s2, shift):
    return x * c + pltpu.roll(x, LANES - shift, 1) * s1 + pltpu.roll(x, shift, 1) * s2


def _rms(x, g):
    return x * lax.rsqrt(jnp.mean(x * x, axis=-1, keepdims=True) + EPS) * g


def _inproj_kernel(x_ref, mod_ref, g1_ref, win_ref, gq_ref, wuq_ref, gkv_ref, wuk_ref, wuv_ref,
                   ac_ref, as1_ref, as2_ref, cc_ref, cs1_ref, cs2_ref,
                   qa_ref, ka_ref, va_ref, qb_ref, kb_ref, vb_ref, qc_ref, kc_ref, vc_ref, *, sub):
    tm = x_ref.shape[1]
    mod = mod_ref[0]
    scale_ab = HEAD_DIM ** -0.5 * LOG2E
    scale_c = (C_NOPE + C_ROPE) ** -0.5 * LOG2E
    ones_lane = (lax.broadcasted_iota(jnp.int32, (1, LANES), 1) == C_V).astype(F32)

    subs = [slice(r0, r0 + sub) for r0 in range(0, tm, sub)]
    def up_project(p):
        cq = _rms(p[:, _OFF_CQ:_OFF_CQ + C_Q_RANK], gq_ref[...]).astype(BF16)
        ckv = _rms(p[:, _OFF_CKV:_OFF_CKV + C_KV_RANK], gkv_ref[...]).astype(BF16)
        return _dot(cq, wuq_ref[...]), _dot(ckv, wuk_ref[...]), _dot(ckv, wuv_ref[...])

    ps, ups = [], []
    for i, rs in enumerate(subs):
        h = _rms(x_ref[0, rs, :], g1_ref[...]) * (1.0 + mod[1:2]) + mod[0:1]
        ps.append(_dot(h.astype(BF16), win_ref[...]))
        if i >= 1:
            ups.append(up_project(ps[i - 1]))
    ups.append(up_project(ps[-1]))

    for rs, p, (q_up, k_up, v_up) in zip(subs, ps, ups):
        ac, as1, as2 = ac_ref[rs, :], as1_ref[rs, :], as2_ref[rs, :]
        cc, cs1, cs2 = cc_ref[rs, :], cs1_ref[rs, :], cs2_ref[rs, :]
        for j in range(4):
            blk = p[:, _OFF_QA + LANES * j:_OFF_QA + LANES * (j + 1)]
            qa_ref[0, rs, LANES * j:LANES * (j + 1)] = (_rope(blk, ac, as1, as2, 16) * scale_ab).astype(BF16)
        ka_ref[0, rs, :] = _rope(p[:, _OFF_KA:_OFF_KA + LANES], ac, as1, as2, 16).astype(BF16)
        va = p[:, _OFF_VA:_OFF_VA + LANES]
        lane = lax.broadcasted_iota(jnp.int32, va.shape, 1)
        va_ref[0, rs, 0:LANES] = jnp.where(lane < HEAD_DIM, va, jnp.where(lane == HEAD_DIM, 1.0, 0.0)).astype(BF16)
        va_ref[0, rs, LANES:2 * LANES] = jnp.where(lane >= HEAD_DIM, va, jnp.where(lane == 0, 1.0, 0.0)).astype(BF16)
        qb_ref[0, rs, :] = (p[:, _OFF_QB:_OFF_QB + 256] * scale_ab).astype(BF16)
        kb_ref[0, rs, :] = p[:, _OFF_KB:_OFF_KB + 256].astype(BF16)
        vb_ref[0, rs, :] = p[:, _OFF_VB:_OFF_VB + 256].astype(BF16)
        kr = _rope(p[:, _OFF_KR:_OFF_KR + LANES], cc, cs1, cs2, 8)
        for hd in range(C_HEADS):
            sl = slice(LANES * hd, LANES * (hd + 1))
            qc_ref[0, rs, sl] = (_rope(q_up[:, sl], cc, cs1, cs2, 8) * scale_c).astype(BF16)
            kc_ref[0, rs, sl] = (k_up[:, sl] + kr).astype(BF16)
            vc_ref[0, rs, sl] = (v_up[:, sl] + ones_lane).astype(BF16)


def _inproj(x, layer, mod_row, mods, g1, win, gq, wuq, gkv, wuk, wuv, tabs, tm):
    b, s, d = x.shape
    nt = s // tm

    def xmap(j, bb):
        return (bb, j, 0)

    def modmap(j, bb):
        return (layer, (bb if mod_row is None else mod_row), 0, 0)

    def wmap(j, bb):
        return (layer, 0, 0)

    def tabmap(j, bb):
        return (j, 0)

    widths = (512, 128, 256, 256, 256, 256, 512, 512, 512)
    in_specs = [
        pl.BlockSpec((1, tm, d), xmap),
        pl.BlockSpec((None, 1, N_MOD, d), modmap),
        pl.BlockSpec((None, 1, d), wmap),
        pl.BlockSpec((None, d, IN_COLS), wmap),
        pl.BlockSpec((None, 1, C_Q_RANK), wmap),
        pl.BlockSpec((None, C_Q_RANK, 512), wmap),
        pl.BlockSpec((None, 1, C_KV_RANK), wmap),
        pl.BlockSpec((None, C_KV_RANK, 512), wmap),
        pl.BlockSpec((None, C_KV_RANK, 512), wmap),
    ] + [pl.BlockSpec((tm, LANES), tabmap) for _ in range(6)]
    return pl.pallas_call(
        functools.partial(_inproj_kernel, sub=min(tm, 128)),
        grid=(nt, b),
        in_specs=in_specs,
        out_specs=[pl.BlockSpec((1, tm, w), xmap) for w in widths],
        out_shape=[jax.ShapeDtypeStruct((b, s, w), BF16) for w in widths],
        compiler_params=_params("arbitrary", "arbitrary"),
        name="norm_inproj_rope",
    )(x, mods, g1, win, gq, wuq, gkv, wuk, wuv, *tabs)


def _lane_lo(shape):
    return lax.broadcasted_iota(jnp.int32, shape, 1) < HEAD_DIM


def _split_pair(q2):
    lo = _lane_lo(q2.shape)
    zero = jnp.zeros_like(q2)
    return jnp.concatenate([jnp.where(lo, q2, zero), jnp.where(lo, zero, q2)], axis=0)


def _merge_pair(o):
    n = o.shape[0] // 2
    return jnp.where(_lane_lo((n, o.shape[1])), o[:n], o[n:])


def _softmax_pv(parts, sink=None):
    m = parts[0][0].max(axis=-1, keepdims=True)
    for s, _ in parts[1:]:
        m = jnp.maximum(m, s.max(axis=-1, keepdims=True))
    if sink is not None:
        m = jnp.maximum(m, sink)
    denom = None
    acc = None
    for s, v in parts:
        e = jnp.exp2(s - m)
        r = e.sum(axis=-1, keepdims=True)
        denom = r if denom is None else denom + r
        o = _dot(e.astype(BF16), v)
        acc = o if acc is None else acc + o
    if sink is not None:
        denom = denom + jnp.exp2(sink - m)
    return acc / denom


def _attn_a_kernel(q_ref, k_ref, v_ref, kx_ref, vx_ref, sink_ref, o_ref, *, tq, seq):
    t = pl.program_id(1)
    nloc = tq + 2 * A_WINDOW
    qstart = t * tq
    start = pl.multiple_of(jnp.clip(qstart - A_WINDOW, 0, seq - nloc), A_WINDOW)
    ks = k_ref[0, pl.ds(start, nloc), :]
    kx = kx_ref[0]
    rel = (lax.broadcasted_iota(jnp.int32, (tq, nloc), 1) - lax.broadcasted_iota(jnp.int32, (tq, nloc), 0)
           + (start - qstart))
    band = jnp.where(jnp.abs(rel) <= A_WINDOW, 0.0, NEG_INF)
    lo = _lane_lo((tq, LANES))
    units = [(j, g) for j in range(A_GROUP) for g in range(A_KV_HEADS)]

    def scores(u):
        j, g = units[u]
        q2 = q_ref[0, :, LANES * j:LANES * (j + 1)]
        zero = jnp.zeros_like(q2)
        q = jnp.where(lo, q2, zero) if g == 0 else jnp.where(lo, zero, q2)
        return _dot_t(q, ks) + band, _dot_t(q, kx)

    outs = {}
    nxt = scores(0)
    for u, (j, g) in enumerate(units):
        s_loc, s_ctx = nxt
        if u + 1 < len(units):
            nxt = scores(u + 1)
        vsg = v_ref[0, pl.ds(start, nloc), LANES * g:LANES * (g + 1)]
        vxg = vx_ref[0, :, LANES * g:LANES * (g + 1)]
        hd = g * A_GROUP + j
        sink = jnp.broadcast_to(sink_ref[hd:hd + 1, 0:1], (tq, 1))
        m = jnp.maximum(jnp.maximum(s_loc.max(axis=-1, keepdims=True), s_ctx.max(axis=-1, keepdims=True)), sink)
        o = _dot(jnp.exp2(s_loc - m).astype(BF16), vsg) + _dot(jnp.exp2(s_ctx - m).astype(BF16), vxg)
        denom = (o[:, HEAD_DIM:HEAD_DIM + 1] if g == 0 else o[:, 0:1]) + jnp.exp2(sink - m)
        outs[(j, g)] = o / denom
        if g == A_KV_HEADS - 1:
            o_ref[0, :, LANES * j:LANES * (j + 1)] = jnp.where(lo, outs[(j, 0)], outs[(j, 1)]).astype(BF16)


def _attn_a(qa, ka, va, kax, vax, sinks, layer, tq):
    b, s, _ = qa.shape
    c = kax.shape[1]

    def qmap(bb, t):
        return (bb, t, 0)

    def kmap(bb, t):
        return (bb, 0, 0)

    return pl.pallas_call(
        functools.partial(_attn_a_kernel, tq=tq, seq=s),
        grid=(b, s // tq),
        in_specs=[
            pl.BlockSpec((1, tq, 512), qmap),
            pl.BlockSpec((1, s, LANES), kmap),
            pl.BlockSpec((1, s, 2 * LANES), kmap),
            pl.BlockSpec((1, c, LANES), kmap),
            pl.BlockSpec((1, c, 2 * LANES), kmap),
            pl.BlockSpec((None, A_HEADS, LANES), lambda bb, t: (layer, 0, 0)),
        ],
        out_specs=pl.BlockSpec((1, tq, 512), qmap),
        out_shape=jax.ShapeDtypeStruct((b, s, 512), BF16),
        compiler_params=_params("arbitrary", "arbitrary"),
        name="attn_window_gqa",
    )(qa, ka, va, kax, vax, sinks)


def _attn_b_kernel(q_ref, k_ref, v_ref, kx_ref, vx_ref, bias_ref, o_ref, *, rows):
    pair, rb = pl.program_id(1), pl.program_id(2)
    kx, vx = kx_ref[0], vx_ref[0]

    def scores(r):
        qr = rb * NA_ROWS + r
        wr = jnp.clip(qr - NA_ROWS // 2, 0, rows - NA_ROWS)
        oi = wr - qr + (NA_ROWS - 1)
        start = pl.multiple_of(wr * GRID_W, GRID_W)
        q = _split_pair(q_ref[0, r * GRID_W:(r + 1) * GRID_W, :])
        ks = k_ref[0, pl.ds(start, NA_ROWS * GRID_W), :]
        bias = jnp.concatenate([bias_ref[pair, oi], bias_ref[pair, NA_ROWS + oi]], axis=0)
        return _dot_t(q, ks) + bias, _dot_t(q, kx), start

    nxt = scores(0)
    for r in range(NA_ROWS):
        s_loc, s_ctx, start = nxt
        if r + 1 < NA_ROWS:
            nxt = scores(r + 1)
        vs = v_ref[0, pl.ds(start, NA_ROWS * GRID_W), :]
        o = _softmax_pv([(s_loc, vs), (s_ctx, vx)])
        o_ref[0, r * GRID_W:(r + 1) * GRID_W, :] = _merge_pair(o).astype(BF16)


def _attn_b(qb, kb, vb, kbx, vbx, bias, layer):
    b, s, _ = qb.shape
    c = kbx.shape[1]
    rows = s // GRID_W
    tq = NA_ROWS * GRID_W

    def qmap(bb, p, rb):
        return (bb, rb, p)

    def kmap(bb, p, rb):
        return (bb, 0, p)

    return pl.pallas_call(
        functools.partial(_attn_b_kernel, rows=rows),
        grid=(b, B_HEADS // 2, s // tq),
        in_specs=[
            pl.BlockSpec((1, tq, LANES), qmap),
            pl.BlockSpec((1, s, LANES), kmap),
            pl.BlockSpec((1, s, LANES), kmap),
            pl.BlockSpec((1, c, LANES), kmap),
            pl.BlockSpec((1, c, LANES), kmap),
            pl.BlockSpec((None, B_HEADS // 2, 2 * NA_ROWS, GRID_W, NA_ROWS * GRID_W),
                         lambda bb, p, rb: (layer, 0, 0, 0, 0)),
        ],
        out_specs=pl.BlockSpec((1, tq, LANES), qmap),
        out_shape=jax.ShapeDtypeStruct((b, s, 256), BF16),
        compiler_params=_params("arbitrary", "arbitrary", "arbitrary"),
        name="attn_neighbourhood",
    )(qb, kb, vb, kbx, vbx, bias)


def _na_bias_table(rpb):
    depth, h = rpb.shape[:2]
    qc = jnp.arange(GRID_W)[:, None]
    kc = jnp.arange(GRID_W)[None, :]
    wc = jnp.clip(qc - NA_COLS // 2, 0, GRID_W - NA_COLS)
    valid = (kc >= wc) & (kc < wc + NA_COLS)
    pad = GRID_W - NA_COLS
    rp = jnp.pad(rpb, ((0, 0), (0, 0), (0, 0), (pad, pad)))
    toe = jnp.stack([rp[..., GRID_W - 1 - q:2 * GRID_W - 1 - q] for q in range(GRID_W)], axis=3)
    toe = jnp.where(valid, toe * LOG2E, NEG_INF)
    t = jnp.stack([toe[:, :, o:o + NA_ROWS] for o in range(NA_ROWS)], axis=2)
    t = t.transpose(0, 1, 2, 4, 3, 5).reshape(depth, h // 2, 2 * NA_ROWS, GRID_W, NA_ROWS * GRID_W)
    return t.astype(F32)


def _attn_c_kernel(q_ref, k_ref, kx_ref, v_ref, vx_ref, o_ref, *, kc):
    s = k_ref.shape[1]
    chunks = [(k_ref, v_ref, c0, kc) for c0 in range(0, s, kc)] + [(kx_ref, vx_ref, 0, kx_ref.shape[1])]
    qs = [q_ref[0, :, LANES * hh:LANES * (hh + 1)] for hh in range(2)]
    m = [None, None]
    acc = [None, None]

    def scores(ci):
        kr, _, c0, n = chunks[ci]
        return [_dot_t(qs[hh], kr[0, c0:c0 + n, LANES * hh:LANES * (hh + 1)]) for hh in range(2)]

    nxt = scores(0)
    for ci, (_, vr, c0, n) in enumerate(chunks):
        cur = nxt
        if ci + 1 < len(chunks):
            nxt = scores(ci + 1)
        for hh in range(2):
            sl = slice(LANES * hh, LANES * (hh + 1))
            sc = cur[hh]
            cm = sc.max(axis=-1, keepdims=True)
            if m[hh] is None:
                m[hh] = cm
                acc[hh] = _dot(jnp.exp2(sc - cm).astype(BF16), vr[0, c0:c0 + n, sl])
            else:
                m_new = jnp.maximum(m[hh], cm)
                acc[hh] = (jnp.exp2(m[hh] - m_new) * acc[hh]
                           + _dot(jnp.exp2(sc - m_new).astype(BF16), vr[0, c0:c0 + n, sl]))
                m[hh] = m_new
    o0 = acc[0] / acc[0][:, C_V:C_V + 1]
    o1 = acc[1] / acc[1][:, C_V:C_V + 1]
    o_ref[0] = jnp.where(_lane_lo(o0.shape), o0, pltpu.roll(o1, C_V, 1)).astype(BF16)


def _attn_c(qc, kc, vc, kcx, vcx, tq):
    b, s, _ = qc.shape
    c = kcx.shape[1]

    def qmap(bb, p, i):
        return (bb, i, p)

    def kmap(bb, p, i):
        return (bb, 0, p)

    return pl.pallas_call(
        functools.partial(_attn_c_kernel, kc=512),
        grid=(b, C_HEADS // 2, s // tq),
        in_specs=[
            pl.BlockSpec((1, tq, 2 * LANES), qmap),
            pl.BlockSpec((1, s, 2 * LANES), kmap),
            pl.BlockSpec((1, c, 2 * LANES), kmap),
            pl.BlockSpec((1, s, 2 * LANES), kmap),
            pl.BlockSpec((1, c, 2 * LANES), kmap),
        ],
        out_specs=pl.BlockSpec((1, tq, LANES), qmap),
        out_shape=jax.ShapeDtypeStruct((b, s, 256), BF16),
        compiler_params=_params("arbitrary", "arbitrary", "arbitrary"),
        name="attn_latent",
    )(qc, kc, kcx, vc, vcx)


def _attn_ctx_kernel(qa_ref, ka_ref, va_ref, qb_ref, kb_ref, vb_ref, qc_ref, kc_ref, vc_ref,
                     sink_ref, oa_ref, ob_ref, oc_ref):
    c = qa_ref.shape[1]
    ka = ka_ref[0]
    half = A_HEADS // 2
    for j in range(half):
        sl = slice(LANES * j, LANES * (j + 1))
        q = _split_pair(qa_ref[0, :, sl])
        outs = []
        for g in range(A_KV_HEADS):
            sink = jnp.broadcast_to(sink_ref[j + half * g:j + half * g + 1, 0:1], (c, 1))
            outs.append(_softmax_pv([(_dot_t(q[c * g:c * (g + 1)], ka), va_ref[0, :, LANES * g:LANES * (g + 1)])], sink))
        oa_ref[0, :, sl] = jnp.where(_lane_lo(outs[0].shape), outs[0], outs[1]).astype(BF16)
    for p in range(B_HEADS // 2):
        sl = slice(LANES * p, LANES * (p + 1))
        q = _split_pair(qb_ref[0, :, sl])
        o = _softmax_pv([(_dot_t(q, kb_ref[0, :, sl]), vb_ref[0, :, sl])])
        ob_ref[0, :, sl] = _merge_pair(o).astype(BF16)
    for p in range(C_HEADS // 2):
        outs = []
        for hh in range(2):
            sl = slice(LANES * (2 * p + hh), LANES * (2 * p + hh + 1))
            outs.append(_softmax_pv([(_dot_t(qc_ref[0, :, sl], kc_ref[0, :, sl]), vc_ref[0, :, sl])]))
        oc_ref[0, :, LANES * p:LANES * (p + 1)] = jnp.where(
            _lane_lo(outs[0].shape), outs[0], pltpu.roll(outs[1], C_V, 1)).astype(BF16)


def _attn_ctx(parts, sinks, layer):
    b, c, _ = parts[0].shape
    widths = [a.shape[2] for a in parts]

    def bmap(bb):
        return (bb, 0, 0)

    return pl.pallas_call(
        _attn_ctx_kernel,
        grid=(b,),
        in_specs=[pl.BlockSpec((1, c, w), bmap) for w in widths]
        + [pl.BlockSpec((None, A_HEADS, LANES), lambda bb: (layer, 0, 0))],
        out_specs=[pl.BlockSpec((1, c, w), bmap) for w in (512, 256, 256)],
        out_shape=[jax.ShapeDtypeStruct((b, c, w), BF16) for w in (512, 256, 256)],
        compiler_params=_params("arbitrary"),
        name="attn_context",
    )(*parts, sinks)


def _outproj_mlp_kernel(x_ref, ma_ref, mb_ref, mc_ref, mod_ref, woa_ref, wob_ref, woc_ref,
                        g2_ref, w1_ref, w2_ref, gf_ref, o_ref, *, final, ff_chunk):
    mod = mod_ref[0]
    attn = _dot(ma_ref[0], woa_ref[...]) + _dot(mb_ref[0], wob_ref[...]) + _dot(mc_ref[0], woc_ref[...])
    x1 = x_ref[0] + mod[2:3] * attn
    h = (_rms(x1, g2_ref[...]) * (1.0 + mod[4:5]) + mod[3:4]).astype(BF16)
    d_ff = w1_ref.shape[1]
    y = None
    for c0 in range(0, d_ff, ff_chunk):
        u = jnp.maximum(_dot(h, w1_ref[:, c0:c0 + ff_chunk]), 0.0)
        part = _dot((u * u).astype(BF16), w2_ref[c0:c0 + ff_chunk, :])
        y = part if y is None else y + part
    x2 = x1 + mod[5:6] * y
    if final:
        x2 = _rms(x2, gf_ref[...])
    o_ref[0] = x2


def _outproj_mlp(x, ma, mb, mc, layer, mod_row, mods, woa, wob, woc, g2, w1, w2, gf, tm, final):
    b, s, d = x.shape
    d_ff = w1.shape[2]

    def xmap(bb, j):
        return (bb, j, 0)

    def modmap(bb, j):
        return (layer, (bb if mod_row is None else mod_row), 0, 0)

    def resident(shape):
        return pl.BlockSpec((None,) + shape, lambda bb, j: (layer, 0, 0), pipeline_mode=pl.Buffered(1))

    return pl.pallas_call(
        functools.partial(_outproj_mlp_kernel, final=final, ff_chunk=1024),
        grid=(b, s // tm),
        in_specs=[
            pl.BlockSpec((1, tm, d), xmap),
            pl.BlockSpec((1, tm, 512), xmap),
            pl.BlockSpec((1, tm, 256), xmap),
            pl.BlockSpec((1, tm, 256), xmap),
            pl.BlockSpec((None, 1, N_MOD, d), modmap),
            resident((512, d)), resident((256, d)), resident((256, d)),
            resident((1, d)),
            resident((d, d_ff)), resident((d_ff, d)),
            pl.BlockSpec((1, d), lambda bb, j: (0, 0)),
        ],
        out_specs=pl.BlockSpec((1, tm, d), xmap),
        out_shape=jax.ShapeDtypeStruct((b, s, d), F32),
        compiler_params=_params("arbitrary", "arbitrary"),
        name="outproj_mlp",
    )(x, ma, mb, mc, mods, woa, wob, woc, g2, w1, w2, gf)


def _rope_tables(s):
    tok = jnp.arange(s)
    row, col = (tok // GRID_W).astype(F32), (tok % GRID_W).astype(F32)

    def cs(pos, half):
        freqs = ROPE_BASE ** (-jnp.arange(half, dtype=F32) / half)
        ang = pos[:, None] * freqs
        return jnp.cos(ang), jnp.sin(ang)

    cr, sr = cs(row, 16)
    cc, sc = cs(col, 16)
    z = jnp.zeros_like(sr)
    a_c = jnp.tile(jnp.concatenate([cr, cr, cc, cc], axis=1), (1, 2))
    a_s1 = jnp.tile(jnp.concatenate([-sr, z, -sc, z], axis=1), (1, 2))
    a_s2 = jnp.tile(jnp.concatenate([z, sr, z, sc], axis=1), (1, 2))
    cr, sr = cs(row, 8)
    cc, sc = cs(col, 8)
    z = jnp.zeros_like(sr)
    one64, zero64 = jnp.ones((s, C_NOPE), F32), jnp.zeros((s, C_NOPE), F32)
    one32, zero32 = jnp.ones((s, 32), F32), jnp.zeros((s, 32), F32)
    c_c = jnp.concatenate([one64, cr, cr, cc, cc, one32], axis=1)
    c_s1 = jnp.concatenate([zero64, -sr, z, -sc, z, zero32], axis=1)
    c_s2 = jnp.concatenate([zero64, z, sr, z, sc, zero32], axis=1)
    return (a_c, a_s1, a_s2, c_c, c_s1, c_s2)


def _identity_tables(s):
    one, zero = jnp.ones((s, LANES), F32), jnp.zeros((s, LANES), F32)
    return (one, zero, zero, one, zero, zero)


def _pad_heads(w, offs, width):
    pad = jnp.zeros(w.shape[:2] + (LANES - width,), w.dtype)
    return jnp.concatenate([jnp.concatenate([w[:, :, o:o + width], pad], axis=2) for o in offs], axis=2)


def _weight_layouts(w_in, w_uq, w_ukv, w_out):
    depth, d, _ = w_in.shape
    qa_w = w_in[:, :, :512].reshape(depth, d, A_KV_HEADS, A_GROUP, HEAD_DIM).transpose(0, 1, 3, 2, 4)
    qa_w = qa_w.reshape(depth, d, 512)
    z = lambda n: jnp.zeros((depth, d, n), w_in.dtype)
    kr_w = jnp.concatenate([z(64), w_in[:, :, 1920:1952], z(32)], axis=2)
    win = jnp.concatenate([qa_w, w_in[:, :, 512:1920], kr_w], axis=2).astype(BF16)
    hq, hk = C_NOPE + C_ROPE, C_NOPE + C_V
    wuq = _pad_heads(w_uq, [hq * h for h in range(C_HEADS)], hq).astype(BF16)
    wuk = _pad_heads(w_ukv, [hk * h for h in range(C_HEADS)], C_NOPE).astype(BF16)
    wuv = _pad_heads(w_ukv, [hk * h + C_NOPE for h in range(C_HEADS)], C_V).astype(BF16)
    woa = w_out[:, :512].reshape(depth, A_KV_HEADS, A_GROUP, HEAD_DIM, d).transpose(0, 2, 1, 3, 4)
    woa = woa.reshape(depth, 512, d).astype(BF16)
    wob = w_out[:, 512:768].astype(BF16)
    woc = w_out[:, 768:].astype(BF16)
    return win, wuq, wuk, wuv, woa, wob, woc


def kernel(x, c, ctx, c_ctx, w_ada, b_ada, norm1_g, norm2_g, w_in, attn_sink, na_rpb, mla_q_norm_g,
           mla_w_uq, mla_kv_norm_g, mla_w_ukv, w_out, w_mlp_in, w_mlp_out, final_norm_g):
    b, s, d = x.shape
    n_ctx = ctx.shape[1]
    depth = w_ada.shape[0]
    assert b + 1 <= MOD_ROWS and s % 512 == 0 and n_ctx % 128 == 0
    tm_x = 512
    tm_c = min(n_ctx, 256)
    tq_a = 256
    tq_c = 256

    cvec = jnp.concatenate([c, c_ctx[None], jnp.zeros((MOD_ROWS - b - 1, d), c.dtype)], axis=0)
    mods = _modulation(cvec, w_ada, b_ada).reshape(depth, MOD_ROWS, N_MOD, d)
    tabs_x = _rope_tables(s)
    tabs_c = _identity_tables(n_ctx)
    gf = final_norm_g.reshape(1, d)
    win, wuq, wuk, wuv, woa, wob, woc = _weight_layouts(w_in, mla_w_uq, mla_w_ukv, w_out)
    w1, w2 = w_mlp_in.astype(BF16), w_mlp_out.astype(BF16)
    sinks = jnp.broadcast_to((attn_sink * LOG2E)[:, :, None], (depth, A_HEADS, LANES)).astype(F32)
    bias = _na_bias_table(na_rpb)
    g1, g2 = norm1_g.reshape(depth, 1, d), norm2_g.reshape(depth, 1, d)
    gq, gkv = mla_q_norm_g.reshape(depth, 1, -1), mla_kv_norm_g.reshape(depth, 1, -1)

    for l in range(depth):
        last = l == depth - 1
        xs = _inproj(x, l, None, mods, g1, win, gq, wuq, gkv, wuk, wuv, tabs_x, tm_x)
        cs = _inproj(ctx, l, b, mods, g1, win, gq, wuq, gkv, wuk, wuv, tabs_c, tm_c)
        qa, ka, va, qb, kb, vb, qc, kc, vc = xs
        cqa, cka, cva, cqb, ckb, cvb, cqc, ckc, cvc = cs

        oa = _attn_a(qa, ka, va, cka, cva, sinks, l, tq_a)
        ob = _attn_b(qb, kb, vb, ckb, cvb, bias, l)
        oc = _attn_c(qc, kc, vc, ckc, cvc, tq_c)
        x = _outproj_mlp(x, oa, ob, oc, l, None, mods, woa, wob, woc, g2, w1, w2, gf, tm_x, last)
        if not last:
            coa, cob, coc = _attn_ctx(cs, sinks, l)
            ctx = _outproj_mlp(ctx, coa, cob, coc, l, b, mods, woa, wob, woc, g2, w1, w2, gf, tm_c, False)
    return x
```

```python
import functools
import math

import jax
import jax.numpy as jnp
from jax import lax
from jax.experimental import pallas as pl
from jax.experimental.pallas import tpu as pltpu

F32 = jnp.float32
BF16 = jnp.bfloat16

GRID_W = 64
HEAD_DIM = 64
A_HEADS = 8
A_KV_HEADS = 2
A_GROUP = A_HEADS // A_KV_HEADS
A_WINDOW = 128
B_HEADS = 4
NA_ROWS = 8
NA_COLS = 16
C_HEADS = 4
C_Q_RANK = 256
C_KV_RANK = 128
C_NOPE = 64
C_ROPE = 32
C_V = 64
N_MOD = 6
ROPE_BASE = 10000.0
EPS = 1e-6
NEG_INF = -1e30
LOG2E = math.log2(math.e)

LANES = 128
MOD_ROWS = 16
IN_COLS = 2048
VMEM_LIMIT = 56 * 1024 * 1024

_OFF_QA, _OFF_KA, _OFF_VA = 0, 512, 640
_OFF_QB, _OFF_KB, _OFF_VB = 768, 1024, 1280
_OFF_CQ, _OFF_CKV, _OFF_KR = 1536, 1792, 1920


def _dot(a, b):
    return jnp.dot(a, b, preferred_element_type=F32)


def _dot_t(a, b):
    return lax.dot_general(a, b, (((1,), (1,)), ((), ())), preferred_element_type=F32)


def _params(*sem):
    return pltpu.CompilerParams(dimension_semantics=sem, vmem_limit_bytes=VMEM_LIMIT)


def _mod_kernel(c_ref, w_ref, b_ref, o_ref):
    c = c_ref[...]
    s = c * (1.0 / (1.0 + jnp.exp(-c)))
    o_ref[0] = jnp.dot(s, w_ref[0], preferred_element_type=F32,
                       precision=lax.Precision.HIGHEST) + b_ref[0]


def _modulation(cvec, w_ada, b_ada):
    depth, d, n = w_ada.shape
    tn = 1024
    return pl.pallas_call(
        _mod_kernel,
        grid=(depth, n // tn),
        in_specs=[
            pl.BlockSpec((MOD_ROWS, d), lambda l, j: (0, 0)),
            pl.BlockSpec((1, d, tn), lambda l, j: (l, 0, j)),
            pl.BlockSpec((1, 1, tn), lambda l, j: (l, 0, j)),
        ],
        out_specs=pl.BlockSpec((1, MOD_ROWS, tn), lambda l, j: (l, 0, j)),
        out_shape=jax.ShapeDtypeStruct((depth, MOD_ROWS, n), F32),
        compiler_params=_params("arbitrary", "arbitrary"),
        name="adaln_modulation",
    )(cvec, w_ada, b_ada.reshape(depth, 1, n))


def _rope(x, c, s1, s2, shift):
    return x * c + pltpu.roll(x, LANES - shift, 1) * s1 + pltpu.roll(x, shift, 1) * s2


def _rms(x, g):
    return x * lax.rsqrt(jnp.mean(x * x, axis=-1, keepdims=True) + EPS) * g


def _inproj_kernel(x_ref, mod_ref, g1_ref, win_ref, gq_ref, wuq_ref, gkv_ref, wuk_ref, wuv_ref,
                   ac_ref, as1_ref, as2_ref, cc_ref, cs1_ref, cs2_ref,
                   qa_ref, ka_ref, va_ref, qb_ref, kb_ref, vb_ref, qc_ref, kc_ref, vc_ref, *, sub):
    tm = x_ref.shape[1]
    mod = mod_ref[0]
    scale_ab = HEAD_DIM ** -0.5 * LOG2E
    scale_c = (C_NOPE + C_ROPE) ** -0.5 * LOG2E
    ones_lane = (lax.broadcasted_iota(jnp.int32, (1, LANES), 1) == C_V).astype(F32)

    subs = [slice(r0, r0 + sub) for r0 in range(0, tm, sub)]
    def up_project(p):
        cq = _rms(p[:, _OFF_CQ:_OFF_CQ + C_Q_RANK], gq_ref[...]).astype(BF16)
        ckv = _rms(p[:, _OFF_CKV:_OFF_CKV + C_KV_RANK], gkv_ref[...]).astype(BF16)
        return _dot(cq, wuq_ref[...]), _dot(ckv, wuk_ref[...]), _dot(ckv, wuv_ref[...])

    ps, ups = [], []
    for i, rs in enumerate(subs):
        h = _rms(x_ref[0, rs, :], g1_ref[...]) * (1.0 + mod[1:2]) + mod[0:1]
        ps.append(_dot(h.astype(BF16), win_ref[...]))
        if i >= 1:
            ups.append(up_project(ps[i - 1]))
    ups.append(up_project(ps[-1]))

    for rs, p, (q_up, k_up, v_up) in zip(subs, ps, ups):
        ac, as1, as2 = ac_ref[rs, :], as1_ref[rs, :], as2_ref[rs, :]
        cc, cs1, cs2 = cc_ref[rs, :], cs1_ref[rs, :], cs2_ref[rs, :]
        for j in range(4):
            blk = p[:, _OFF_QA + LANES * j:_OFF_QA + LANES * (j + 1)]
            qa_ref[0, rs, LANES * j:LANES * (j + 1)] = (_rope(blk, ac, as1, as2, 16) * scale_ab).astype(BF16)
        ka_ref[0, rs, :] = _rope(p[:, _OFF_KA:_OFF_KA + LANES], ac, as1, as2, 16).astype(BF16)
        va = p[:, _OFF_VA:_OFF_VA + LANES]
        lane = lax.broadcasted_iota(jnp.int32, va.shape, 1)
        va_ref[0, rs, 0:LANES] = jnp.where(lane < HEAD_DIM, va, jnp.where(lane == HEAD_DIM, 1.0, 0.0)).astype(BF16)
        va_ref[0, rs, LANES:2 * LANES] = jnp.where(lane >= HEAD_DIM, va, jnp.where(lane == 0, 1.0, 0.0)).astype(BF16)
        qb_ref[0, rs, :] = (p[:, _OFF_QB:_OFF_QB + 256] * scale_ab).astype(BF16)
        kb_ref[0, rs, :] = p[:, _OFF_KB:_OFF_KB + 256].astype(BF16)
        vb_ref[0, rs, :] = p[:, _OFF_VB:_OFF_VB + 256].astype(BF16)
        kr = _rope(p[:, _OFF_KR:_OFF_KR + LANES], cc, cs1, cs2, 8)
        for hd in range(C_HEADS):
            sl = slice(LANES * hd, LANES * (hd + 1))
            qc_ref[0, rs, sl] = (_rope(q_up[:, sl], cc, cs1, cs2, 8) * scale_c).astype(BF16)
            kc_ref[0, rs, sl] = (k_up[:, sl] + kr).astype(BF16)
            vc_ref[0, rs, sl] = (v_up[:, sl] + ones_lane).astype(BF16)


def _inproj(x, layer, mod_row, mods, g1, win, gq, wuq, gkv, wuk, wuv, tabs, tm):
    b, s, d = x.shape
    nt = s // tm

    def xmap(j, bb):
        return (bb, j, 0)

    def modmap(j, bb):
        return (layer, (bb if mod_row is None else mod_row), 0, 0)

    def wmap(j, bb):
        return (layer, 0, 0)

    def tabmap(j, bb):
        return (j, 0)

    widths = (512, 128, 256, 256, 256, 256, 512, 512, 512)
    in_specs = [
        pl.BlockSpec((1, tm, d), xmap),
        pl.BlockSpec((None, 1, N_MOD, d), modmap),
        pl.BlockSpec((None, 1, d), wmap),
        pl.BlockSpec((None, d, IN_COLS), wmap),
        pl.BlockSpec((None, 1, C_Q_RANK), wmap),
        pl.BlockSpec((None, C_Q_RANK, 512), wmap),
        pl.BlockSpec((None, 1, C_KV_RANK), wmap),
        pl.BlockSpec((None, C_KV_RANK, 512), wmap),
        pl.BlockSpec((None, C_KV_RANK, 512), wmap),
    ] + [pl.BlockSpec((tm, LANES), tabmap) for _ in range(6)]
    return pl.pallas_call(
        functools.partial(_inproj_kernel, sub=min(tm, 128)),
        grid=(nt, b),
        in_specs=in_specs,
        out_specs=[pl.BlockSpec((1, tm, w), xmap) for w in widths],
        out_shape=[jax.ShapeDtypeStruct((b, s, w), BF16) for w in widths],
        compiler_params=_params("arbitrary", "arbitrary"),
        name="norm_inproj_rope",
    )(x, mods, g1, win, gq, wuq, gkv, wuk, wuv, *tabs)


def _lane_lo(shape):
    return lax.broadcasted_iota(jnp.int32, shape, 1) < HEAD_DIM


def _split_pair(q2):
    lo = _lane_lo(q2.shape)
    zero = jnp.zeros_like(q2)
    return jnp.concatenate([jnp.where(lo, q2, zero), jnp.where(lo, zero, q2)], axis=0)


def _merge_pair(o):
    n = o.shape[0] // 2
    return jnp.where(_lane_lo((n, o.shape[1])), o[:n], o[n:])


def _softmax_pv(parts, sink=None):
    m = parts[0][0].max(axis=-1, keepdims=True)
    for s, _ in parts[1:]:
        m = jnp.maximum(m, s.max(axis=-1, keepdims=True))
    if sink is not None:
        m = jnp.maximum(m, sink)
    denom = None
    acc = None
    for s, v in parts:
        e = jnp.exp2(s - m)
        r = e.sum(axis=-1, keepdims=True)
        denom = r if denom is None else denom + r
        o = _dot(e.astype(BF16), v)
        acc = o if acc is None else acc + o
    if sink is not None:
        denom = denom + jnp.exp2(sink - m)
    return acc / denom


def _attn_a_kernel(q_ref, k_ref, v_ref, kx_ref, vx_ref, sink_ref, o_ref, *, tq, seq):
    t = pl.program_id(1)
    nsub = q_ref.shape[1] // tq
    nloc = tq + 2 * A_WINDOW
    kx = kx_ref[0]
    lo = _lane_lo((tq, LANES))
    iota_rel = (lax.broadcasted_iota(jnp.int32, (tq, nloc), 1) - lax.broadcasted_iota(jnp.int32, (tq, nloc), 0))
    starts, bands = [], []
    for sb in range(nsub):
        qstart = (t * nsub + sb) * tq
        start = pl.multiple_of(jnp.clip(qstart - A_WINDOW, 0, seq - nloc), A_WINDOW)
        rel = iota_rel + (start - qstart)
        starts.append(start)
        bands.append(jnp.where(jnp.abs(rel) <= A_WINDOW, 0.0, NEG_INF))
    units = [(sb, j, g) for sb in range(nsub) for j in range(A_GROUP) for g in range(A_KV_HEADS)]

    def scores(u):
        sb, j, g = units[u]
        q2 = q_ref[0, sb * tq:(sb + 1) * tq, LANES * j:LANES * (j + 1)]
        zero = jnp.zeros_like(q2)
        q = jnp.where(lo, q2, zero) if g == 0 else jnp.where(lo, zero, q2)
        return _dot_t(q, k_ref[0, pl.ds(starts[sb], nloc), :]) + bands[sb], _dot_t(q, kx)

    outs = {}
    nxt = scores(0)
    for u, (sb, j, g) in enumerate(units):
        s_loc, s_ctx = nxt
        if u + 1 < len(units):
            nxt = scores(u + 1)
        vsg = v_ref[0, pl.ds(starts[sb], nloc), LANES * g:LANES * (g + 1)]
        vxg = vx_ref[0, :, LANES * g:LANES * (g + 1)]
        hd = g * A_GROUP + j
        sink = jnp.broadcast_to(sink_ref[hd:hd + 1, 0:1], (tq, 1))
        m = jnp.maximum(jnp.maximum(s_loc.max(axis=-1, keepdims=True), s_ctx.max(axis=-1, keepdims=True)), sink)
        o = _dot(jnp.exp2(s_loc - m).astype(BF16), vsg) + _dot(jnp.exp2(s_ctx - m).astype(BF16), vxg)
        denom = (o[:, HEAD_DIM:HEAD_DIM + 1] if g == 0 else o[:, 0:1]) + jnp.exp2(sink - m)
        outs[g] = o / denom
        if g == A_KV_HEADS - 1:
            o_ref[0, sb * tq:(sb + 1) * tq, LANES * j:LANES * (j + 1)] = jnp.where(lo, outs[0], outs[1]).astype(BF16)


def _attn_a(qa, ka, va, kax, vax, sinks, layer, tq, tstep):
    b, s, _ = qa.shape
    c = kax.shape[1]

    def qmap(bb, t):
        return (bb, t, 0)

    def kmap(bb, t):
        return (bb, 0, 0)

    return pl.pallas_call(
        functools.partial(_attn_a_kernel, tq=tq, seq=s),
        grid=(b, s // tstep),
        in_specs=[
            pl.BlockSpec((1, tstep, 512), qmap),
            pl.BlockSpec((1, s, LANES), kmap),
            pl.BlockSpec((1, s, 2 * LANES), kmap),
            pl.BlockSpec((1, c, LANES), kmap),
            pl.BlockSpec((1, c, 2 * LANES), kmap),
            pl.BlockSpec((None, A_HEADS, LANES), lambda bb, t: (layer, 0, 0)),
        ],
        out_specs=pl.BlockSpec((1, tstep, 512), qmap),
        out_shape=jax.ShapeDtypeStruct((b, s, 512), BF16),
        compiler_params=_params("arbitrary", "arbitrary"),
        name="attn_window_gqa",
    )(qa, ka, va, kax, vax, sinks)


def _attn_b_kernel(q_ref, k_ref, v_ref, kx_ref, vx_ref, bias_ref, o_ref, *, rows):
    pair, rb = pl.program_id(1), pl.program_id(2)
    kx, vx = kx_ref[0], vx_ref[0]
    nrow = q_ref.shape[1] // GRID_W

    def scores(r):
        qr = rb * nrow + r
        wr = jnp.clip(qr - NA_ROWS // 2, 0, rows - NA_ROWS)
        oi = wr - qr + (NA_ROWS - 1)
        start = pl.multiple_of(wr * GRID_W, GRID_W)
        q = _split_pair(q_ref[0, r * GRID_W:(r + 1) * GRID_W, :])
        ks = k_ref[0, pl.ds(start, NA_ROWS * GRID_W), :]
        bias = jnp.concatenate([bias_ref[pair, oi], bias_ref[pair, NA_ROWS + oi]], axis=0)
        return _dot_t(q, ks) + bias, _dot_t(q, kx), start

    nxt = scores(0)
    for r in range(nrow):
        s_loc, s_ctx, start = nxt
        if r + 1 < nrow:
            nxt = scores(r + 1)
        vs = v_ref[0, pl.ds(start, NA_ROWS * GRID_W), :]
        o = _softmax_pv([(s_loc, vs), (s_ctx, vx)])
        o_ref[0, r * GRID_W:(r + 1) * GRID_W, :] = _merge_pair(o).astype(BF16)


def _attn_b(qb, kb, vb, kbx, vbx, bias, layer):
    b, s, _ = qb.shape
    c = kbx.shape[1]
    rows = s // GRID_W
    tq = 2 * NA_ROWS * GRID_W

    def qmap(bb, p, rb):
        return (bb, rb, p)

    def kmap(bb, p, rb):
        return (bb, 0, p)

    return pl.pallas_call(
        functools.partial(_attn_b_kernel, rows=rows),
        grid=(b, B_HEADS // 2, s // tq),
        in_specs=[
            pl.BlockSpec((1, tq, LANES), qmap),
            pl.BlockSpec((1, s, LANES), kmap),
            pl.BlockSpec((1, s, LANES), kmap),
            pl.BlockSpec((1, c, LANES), kmap),
            pl.BlockSpec((1, c, LANES), kmap),
            pl.BlockSpec((None, B_HEADS // 2, 2 * NA_ROWS, GRID_W, NA_ROWS * GRID_W),
                         lambda bb, p, rb: (layer, 0, 0, 0, 0)),
        ],
        out_specs=pl.BlockSpec((1, tq, LANES), qmap),
        out_shape=jax.ShapeDtypeStruct((b, s, 256), BF16),
        compiler_params=_params("arbitrary", "arbitrary", "arbitrary"),
        name="attn_neighbourhood",
    )(qb, kb, vb, kbx, vbx, bias)


def _na_bias_table(rpb):
    depth, h = rpb.shape[:2]
    qc = jnp.arange(GRID_W)[:, None]
    kc = jnp.arange(GRID_W)[None, :]
    wc = jnp.clip(qc - NA_COLS // 2, 0, GRID_W - NA_COLS)
    valid = (kc >= wc) & (kc < wc + NA_COLS)
    pad = GRID_W - NA_COLS
    rp = jnp.pad(rpb, ((0, 0), (0, 0), (0, 0), (pad, pad)))
    toe = jnp.stack([rp[..., GRID_W - 1 - q:2 * GRID_W - 1 - q] for q in range(GRID_W)], axis=3)
    toe = jnp.where(valid, toe * LOG2E, NEG_INF)
    t = jnp.stack([toe[:, :, o:o + NA_ROWS] for o in range(NA_ROWS)], axis=2)
    t = t.transpose(0, 1, 2, 4, 3, 5).reshape(depth, h // 2, 2 * NA_ROWS, GRID_W, NA_ROWS * GRID_W)
    return t.astype(F32)


def _attn_c_kernel(q_ref, k_ref, kx_ref, v_ref, vx_ref, o_ref, *, kc):
    s = k_ref.shape[1]
    chunks = [(k_ref, v_ref, c0, kc) for c0 in range(0, s, kc)] + [(kx_ref, vx_ref, 0, kx_ref.shape[1])]
    qs = [q_ref[0, :, LANES * hh:LANES * (hh + 1)] for hh in range(2)]
    m = [None, None]
    acc = [None, None]

    def scores(ci):
        kr, _, c0, n = chunks[ci]
        return [_dot_t(qs[hh], kr[0, c0:c0 + n, LANES * hh:LANES * (hh + 1)]) for hh in range(2)]

    nxt = scores(0)
    for ci, (_, vr, c0, n) in enumerate(chunks):
        cur = nxt
        if ci + 1 < len(chunks):
            nxt = scores(ci + 1)
        for hh in range(2):
            sl = slice(LANES * hh, LANES * (hh + 1))
            sc = cur[hh]
            cm = sc.max(axis=-1, keepdims=True)
            if m[hh] is None:
                m[hh] = cm
                acc[hh] = _dot(jnp.exp2(sc - cm).astype(BF16), vr[0, c0:c0 + n, sl])
            else:
                m_new = jnp.maximum(m[hh], cm)
                acc[hh] = (jnp.exp2(m[hh] - m_new) * acc[hh]
                           + _dot(jnp.exp2(sc - m_new).astype(BF16), vr[0, c0:c0 + n, sl]))
                m[hh] = m_new
    o0 = acc[0] / acc[0][:, C_V:C_V + 1]
    o1 = acc[1] / acc[1][:, C_V:C_V + 1]
    o_ref[0] = jnp.where(_lane_lo(o0.shape), o0, pltpu.roll(o1, C_V, 1)).astype(BF16)


def _attn_c(qc, kc, vc, kcx, vcx, tq):
    b, s, _ = qc.shape
    c = kcx.shape[1]

    def qmap(bb, p, i):
        return (bb, i, p)

    def kmap(bb, p, i):
        return (bb, 0, p)

    return pl.pallas_call(
        functools.partial(_attn_c_kernel, kc=512),
        grid=(b, C_HEADS // 2, s // tq),
        in_specs=[
            pl.BlockSpec((1, tq, 2 * LANES), qmap),
            pl.BlockSpec((1, s, 2 * LANES), kmap),
            pl.BlockSpec((1, c, 2 * LANES), kmap),
            pl.BlockSpec((1, s, 2 * LANES), kmap),
            pl.BlockSpec((1, c, 2 * LANES), kmap),
        ],
        out_specs=pl.BlockSpec((1, tq, LANES), qmap),
        out_shape=jax.ShapeDtypeStruct((b, s, 256), BF16),
        compiler_params=_params("arbitrary", "arbitrary", "arbitrary"),
        name="attn_latent",
    )(qc, kc, kcx, vc, vcx)


def _attn_ctx_kernel(qa_ref, ka_ref, va_ref, qb_ref, kb_ref, vb_ref, qc_ref, kc_ref, vc_ref,
                     sink_ref, oa_ref, ob_ref, oc_ref):
    c = qa_ref.shape[1]
    ka = ka_ref[0]
    half = A_HEADS // 2
    for j in range(half):
        sl = slice(LANES * j, LANES * (j + 1))
        q = _split_pair(qa_ref[0, :, sl])
        outs = []
        for g in range(A_KV_HEADS):
            sink = jnp.broadcast_to(sink_ref[j + half * g:j + half * g + 1, 0:1], (c, 1))
            outs.append(_softmax_pv([(_dot_t(q[c * g:c * (g + 1)], ka), va_ref[0, :, LANES * g:LANES * (g + 1)])], sink))
        oa_ref[0, :, sl] = jnp.where(_lane_lo(outs[0].shape), outs[0], outs[1]).astype(BF16)
    for p in range(B_HEADS // 2):
        sl = slice(LANES * p, LANES * (p + 1))
        q = _split_pair(qb_ref[0, :, sl])
        o = _softmax_pv([(_dot_t(q, kb_ref[0, :, sl]), vb_ref[0, :, sl])])
        ob_ref[0, :, sl] = _merge_pair(o).astype(BF16)
    for p in range(C_HEADS // 2):
        outs = []
        for hh in range(2):
            sl = slice(LANES * (2 * p + hh), LANES * (2 * p + hh + 1))
            outs.append(_softmax_pv([(_dot_t(qc_ref[0, :, sl], kc_ref[0, :, sl]), vc_ref[0, :, sl])]))
        oc_ref[0, :, LANES * p:LANES * (p + 1)] = jnp.where(
            _lane_lo(outs[0].shape), outs[0], pltpu.roll(outs[1], C_V, 1)).astype(BF16)


def _attn_ctx(parts, sinks, layer):
    b, c, _ = parts[0].shape
    widths = [a.shape[2] for a in parts]

    def bmap(bb):
        return (bb, 0, 0)

    return pl.pallas_call(
        _attn_ctx_kernel,
        grid=(b,),
        in_specs=[pl.BlockSpec((1, c, w), bmap) for w in widths]
        + [pl.BlockSpec((None, A_HEADS, LANES), lambda bb: (layer, 0, 0))],
        out_specs=[pl.BlockSpec((1, c, w), bmap) for w in (512, 256, 256)],
        out_shape=[jax.ShapeDtypeStruct((b, c, w), BF16) for w in (512, 256, 256)],
        compiler_params=_params("arbitrary"),
        name="attn_context",
    )(*parts, sinks)


def _outproj_mlp_kernel(x_ref, ma_ref, mb_ref, mc_ref, mod_ref, woa_ref, wob_ref, woc_ref,
                        g2_ref, w1_ref, w2_ref, gf_ref, o_ref, *, final, ff_chunk):
    mod = mod_ref[0]
    attn = _dot(ma_ref[0], woa_ref[...]) + _dot(mb_ref[0], wob_ref[...]) + _dot(mc_ref[0], woc_ref[...])
    x1 = x_ref[0] + mod[2:3] * attn
    h = (_rms(x1, g2_ref[...]) * (1.0 + mod[4:5]) + mod[3:4]).astype(BF16)
    d_ff = w1_ref.shape[1]
    def up(c0):
        return _dot(h, w1_ref[:, c0:c0 + ff_chunk])

    y = None
    chunk_starts = list(range(0, d_ff, ff_chunk))
    nxt = up(chunk_starts[0])
    for i, c0 in enumerate(chunk_starts):
        u = jnp.maximum(nxt, 0.0)
        if i + 1 < len(chunk_starts):
            nxt = up(chunk_starts[i + 1])
        part = _dot((u * u).astype(BF16), w2_ref[c0:c0 + ff_chunk, :])
        y = part if y is None else y + part
    x2 = x1 + mod[5:6] * y
    if final:
        x2 = _rms(x2, gf_ref[...])
    o_ref[0] = x2


def _outproj_mlp(x, ma, mb, mc, layer, mod_row, mods, woa, wob, woc, g2, w1, w2, gf, tm, final):
    b, s, d = x.shape
    d_ff = w1.shape[2]

    def xmap(bb, j):
        return (bb, j, 0)

    def modmap(bb, j):
        return (layer, (bb if mod_row is None else mod_row), 0, 0)

    def resident(shape):
        return pl.BlockSpec((None,) + shape, lambda bb, j: (layer, 0, 0), pipeline_mode=pl.Buffered(1))

    return pl.pallas_call(
        functools.partial(_outproj_mlp_kernel, final=final, ff_chunk=1024),
        grid=(b, s // tm),
        in_specs=[
            pl.BlockSpec((1, tm, d), xmap),
            pl.BlockSpec((1, tm, 512), xmap),
            pl.BlockSpec((1, tm, 256), xmap),
            pl.BlockSpec((1, tm, 256), xmap),
            pl.BlockSpec((None, 1, N_MOD, d), modmap),
            resident((512, d)), resident((256, d)), resident((256, d)),
            resident((1, d)),
            resident((d, d_ff)), resident((d_ff, d)),
            pl.BlockSpec((1, d), lambda bb, j: (0, 0)),
        ],
        out_specs=pl.BlockSpec((1, tm, d), xmap),
        out_shape=jax.ShapeDtypeStruct((b, s, d), F32),
        compiler_params=_params("arbitrary", "arbitrary"),
        name="outproj_mlp",
    )(x, ma, mb, mc, mods, woa, wob, woc, g2, w1, w2, gf)


def _rope_tables(s):
    tok = jnp.arange(s)
    row, col = (tok // GRID_W).astype(F32), (tok % GRID_W).astype(F32)

    def cs(pos, half):
        freqs = ROPE_BASE ** (-jnp.arange(half, dtype=F32) / half)
        ang = pos[:, None] * freqs
        return jnp.cos(ang), jnp.sin(ang)

    cr, sr = cs(row, 16)
    cc, sc = cs(col, 16)
    z = jnp.zeros_like(sr)
    a_c = jnp.tile(jnp.concatenate([cr, cr, cc, cc], axis=1), (1, 2))
    a_s1 = jnp.tile(jnp.concatenate([-sr, z, -sc, z], axis=1), (1, 2))
    a_s2 = jnp.tile(jnp.concatenate([z, sr, z, sc], axis=1), (1, 2))
    cr, sr = cs(row, 8)
    cc, sc = cs(col, 8)
    z = jnp.zeros_like(sr)
    one64, zero64 = jnp.ones((s, C_NOPE), F32), jnp.zeros((s, C_NOPE), F32)
    one32, zero32 = jnp.ones((s, 32), F32), jnp.zeros((s, 32), F32)
    c_c = jnp.concatenate([one64, cr, cr, cc, cc, one32], axis=1)
    c_s1 = jnp.concatenate([zero64, -sr, z, -sc, z, zero32], axis=1)
    c_s2 = jnp.concatenate([zero64, z, sr, z, sc, zero32], axis=1)
    return (a_c, a_s1, a_s2, c_c, c_s1, c_s2)


def _identity_tables(s):
    one, zero = jnp.ones((s, LANES), F32), jnp.zeros((s, LANES), F32)
    return (one, zero, zero, one, zero, zero)


def _pad_heads(w, offs, width):
    pad = jnp.zeros(w.shape[:2] + (LANES - width,), w.dtype)
    return jnp.concatenate([jnp.concatenate([w[:, :, o:o + width], pad], axis=2) for o in offs], axis=2)


def _weight_layouts(w_in, w_uq, w_ukv, w_out):
    depth, d, _ = w_in.shape
    qa_w = w_in[:, :, :512].reshape(depth, d, A_KV_HEADS, A_GROUP, HEAD_DIM).transpose(0, 1, 3, 2, 4)
    qa_w = qa_w.reshape(depth, d, 512)
    z = lambda n: jnp.zeros((depth, d, n), w_in.dtype)
    kr_w = jnp.concatenate([z(64), w_in[:, :, 1920:1952], z(32)], axis=2)
    win = jnp.concatenate([qa_w, w_in[:, :, 512:1920], kr_w], axis=2).astype(BF16)
    hq, hk = C_NOPE + C_ROPE, C_NOPE + C_V
    wuq = _pad_heads(w_uq, [hq * h for h in range(C_HEADS)], hq).astype(BF16)
    wuk = _pad_heads(w_ukv, [hk * h for h in range(C_HEADS)], C_NOPE).astype(BF16)
    wuv = _pad_heads(w_ukv, [hk * h + C_NOPE for h in range(C_HEADS)], C_V).astype(BF16)
    woa = w_out[:, :512].reshape(depth, A_KV_HEADS, A_GROUP, HEAD_DIM, d).transpose(0, 2, 1, 3, 4)
    woa = woa.reshape(depth, 512, d).astype(BF16)
    wob = w_out[:, 512:768].astype(BF16)
    woc = w_out[:, 768:].astype(BF16)
    return win, wuq, wuk, wuv, woa, wob, woc


def kernel(x, c, ctx, c_ctx, w_ada, b_ada, norm1_g, norm2_g, w_in, attn_sink, na_rpb, mla_q_norm_g,
           mla_w_uq, mla_kv_norm_g, mla_w_ukv, w_out, w_mlp_in, w_mlp_out, final_norm_g):
    b, s, d = x.shape
    n_ctx = ctx.shape[1]
    depth = w_ada.shape[0]
    assert b + 1 <= MOD_ROWS and s % 512 == 0 and n_ctx % 128 == 0
    tm_x = 512
    tm_c = min(n_ctx, 256)
    tq_a = 256
    tq_c = 512

    cvec = jnp.concatenate([c, c_ctx[None], jnp.zeros((MOD_ROWS - b - 1, d), c.dtype)], axis=0)
    mods = _modulation(cvec, w_ada, b_ada).reshape(depth, MOD_ROWS, N_MOD, d)
    tabs_x = _rope_tables(s)
    tabs_c = _identity_tables(n_ctx)
    gf = final_norm_g.reshape(1, d)
    win, wuq, wuk, wuv, woa, wob, woc = _weight_layouts(w_in, mla_w_uq, mla_w_ukv, w_out)
    w1, w2 = w_mlp_in.astype(BF16), w_mlp_out.astype(BF16)
    sinks = jnp.broadcast_to((attn_sink * LOG2E)[:, :, None], (depth, A_HEADS, LANES)).astype(F32)
    bias = _na_bias_table(na_rpb)
    g1, g2 = norm1_g.reshape(depth, 1, d), norm2_g.reshape(depth, 1, d)
    gq, gkv = mla_q_norm_g.reshape(depth, 1, -1), mla_kv_norm_g.reshape(depth, 1, -1)

    for l in range(depth):
        last = l == depth - 1
        xs = _inproj(x, l, None, mods, g1, win, gq, wuq, gkv, wuk, wuv, tabs_x, tm_x)
        cs = _inproj(ctx, l, b, mods, g1, win, gq, wuq, gkv, wuk, wuv, tabs_c, tm_c)
        qa, ka, va, qb, kb, vb, qc, kc, vc = xs
        cqa, cka, cva, cqb, ckb, cvb, cqc, ckc, cvc = cs

        oa = _attn_a(qa, ka, va, cka, cva, sinks, l, tq_a, 2 * tq_a)
        ob = _attn_b(qb, kb, vb, ckb, cvb, bias, l)
        oc = _attn_c(qc, kc, vc, ckc, cvc, tq_c)
        x = _outproj_mlp(x, oa, ob, oc, l, None, mods, woa, wob, woc, g2, w1, w2, gf, tm_x, last)
        if not last:
            coa, cob, coc = _attn_ctx(cs, sinks, l)
            ctx = _outproj_mlp(ctx, coa, cob, coc, l, b, mods, woa, wob, woc, g2, w1, w2, gf, tm_c, False)
    return x
```

```python
import functools
import math

import jax
import jax.numpy as jnp
from jax import lax
from jax.experimental import pallas as pl
from jax.experimental.pallas import tpu as pltpu

F32 = jnp.float32
BF16 = jnp.bfloat16

GRID_W = 64
HEAD_DIM = 64
A_HEADS = 8
A_KV_HEADS = 2
A_GROUP = A_HEADS // A_KV_HEADS
A_WINDOW = 128
B_HEADS = 4
NA_ROWS = 8
NA_COLS = 16
C_HEADS = 4
C_Q_RANK = 256
C_KV_RANK = 128
C_NOPE = 64
C_ROPE = 32
C_V = 64
N_MOD = 6
ROPE_BASE = 10000.0
EPS = 1e-6
NEG_INF = -1e30
LOG2E = math.log2(math.e)

LANES = 128
MOD_ROWS = 16
IN_COLS = 2048
VMEM_LIMIT = 56 * 1024 * 1024

_OFF_QA, _OFF_KA, _OFF_VA = 0, 512, 640
_OFF_QB, _OFF_KB, _OFF_VB = 768, 1024, 1280
_OFF_CQ, _OFF_CKV, _OFF_KR = 1536, 1792, 1920


def _dot(a, b):
    return jnp.dot(a, b, preferred_element_type=F32)


def _dot_t(a, b):
    return lax.dot_general(a, b, (((1,), (1,)), ((), ())), preferred_element_type=F32)


def _params(*sem):
    return pltpu.CompilerParams(dimension_semantics=sem, vmem_limit_bytes=VMEM_LIMIT)


def _mod_kernel(c_ref, w_ref, b_ref, o_ref):
    c = c_ref[...]
    s = c * (1.0 / (1.0 + jnp.exp(-c)))
    o_ref[0] = jnp.dot(s, w_ref[0], preferred_element_type=F32,
                       precision=lax.Precision.HIGHEST) + b_ref[0]


def _modulation(cvec, w_ada, b_ada):
    depth, d, n = w_ada.shape
    tn = 1024
    return pl.pallas_call(
        _mod_kernel,
        grid=(depth, n // tn),
        in_specs=[
            pl.BlockSpec((MOD_ROWS, d), lambda l, j: (0, 0)),
            pl.BlockSpec((1, d, tn), lambda l, j: (l, 0, j)),
            pl.BlockSpec((1, 1, tn), lambda l, j: (l, 0, j)),
        ],
        out_specs=pl.BlockSpec((1, MOD_ROWS, tn), lambda l, j: (l, 0, j)),
        out_shape=jax.ShapeDtypeStruct((depth, MOD_ROWS, n), F32),
        compiler_params=_params("arbitrary", "arbitrary"),
        name="adaln_modulation",
    )(cvec, w_ada, b_ada.reshape(depth, 1, n))


def _rope(x, c, s1, s2, shift):
    return x * c + pltpu.roll(x, LANES - shift, 1) * s1 + pltpu.roll(x, shift, 1) * s2


def _rms(x, g):
    return x * lax.rsqrt(jnp.mean(x * x, axis=-1, keepdims=True) + EPS) * g


def _inproj_kernel(x_ref, mod_ref, g1_ref, win_ref, gq_ref, wuq_ref, gkv_ref, wuk_ref, wuv_ref,
                   ac_ref, as1_ref, as2_ref, cc_ref, cs1_ref, cs2_ref,
                   qa_ref, ka_ref, va_ref, qb_ref, kb_ref, vb_ref, qc_ref, kc_ref, vc_ref, *, sub):
    tm = x_ref.shape[1]
    mod = mod_ref[0]
    scale_ab = HEAD_DIM ** -0.5 * LOG2E
    scale_c = (C_NOPE + C_ROPE) ** -0.5 * LOG2E
    ones_lane = (lax.broadcasted_iota(jnp.int32, (1, LANES), 1) == C_V).astype(F32)

    subs = [slice(r0, r0 + sub) for r0 in range(0, tm, sub)]
    def up_project(p):
        cq = _rms(p[:, _OFF_CQ:_OFF_CQ + C_Q_RANK], gq_ref[...]).astype(BF16)
        ckv = _rms(p[:, _OFF_CKV:_OFF_CKV + C_KV_RANK], gkv_ref[...]).astype(BF16)
        return _dot(cq, wuq_ref[...]), _dot(ckv, wuk_ref[...]), _dot(ckv, wuv_ref[...])

    ps, ups = [], []
    for i, rs in enumerate(subs):
        h = _rms(x_ref[0, rs, :], g1_ref[...]) * (1.0 + mod[1:2]) + mod[0:1]
        ps.append(_dot(h.astype(BF16), win_ref[...]))
        if i >= 1:
            ups.append(up_project(ps[i - 1]))
    ups.append(up_project(ps[-1]))

    for rs, p, (q_up, k_up, v_up) in zip(subs, ps, ups):
        ac, as1, as2 = ac_ref[rs, :], as1_ref[rs, :], as2_ref[rs, :]
        cc, cs1, cs2 = cc_ref[rs, :], cs1_ref[rs, :], cs2_ref[rs, :]
        for j in range(4):
            blk = p[:, _OFF_QA + LANES * j:_OFF_QA + LANES * (j + 1)]
            qa_ref[0, rs, LANES * j:LANES * (j + 1)] = (_rope(blk, ac, as1, as2, 16) * scale_ab).astype(BF16)
        ka_ref[0, rs, :] = _rope(p[:, _OFF_KA:_OFF_KA + LANES], ac, as1, as2, 16).astype(BF16)
        va = p[:, _OFF_VA:_OFF_VA + LANES]
        lane = lax.broadcasted_iota(jnp.int32, va.shape, 1)
        va_ref[0, rs, 0:LANES] = jnp.where(lane < HEAD_DIM, va, jnp.where(lane == HEAD_DIM, 1.0, 0.0)).astype(BF16)
        va_ref[0, rs, LANES:2 * LANES] = jnp.where(lane >= HEAD_DIM, va, jnp.where(lane == 0, 1.0, 0.0)).astype(BF16)
        qb_ref[0, rs, :] = (p[:, _OFF_QB:_OFF_QB + 256] * scale_ab).astype(BF16)
        kb_ref[0, rs, :] = p[:, _OFF_KB:_OFF_KB + 256].astype(BF16)
        vb_ref[0, rs, :] = p[:, _OFF_VB:_OFF_VB + 256].astype(BF16)
        kr = _rope(p[:, _OFF_KR:_OFF_KR + LANES], cc, cs1, cs2, 8)
        for hd in range(C_HEADS):
            sl = slice(LANES * hd, LANES * (hd + 1))
            qc_ref[0, rs, sl] = (_rope(q_up[:, sl], cc, cs1, cs2, 8) * scale_c).astype(BF16)
            kc_ref[0, rs, sl] = (k_up[:, sl] + kr).astype(BF16)
            vc_ref[0, rs, sl] = (v_up[:, sl] + ones_lane).astype(BF16)


def _inproj(x, layer, mod_row, mods, g1, win, gq, wuq, gkv, wuk, wuv, tabs, tm):
    b, s, d = x.shape
    nt = s // tm

    def xmap(j, bb):
        return (bb, j, 0)

    def modmap(j, bb):
        return (layer, (bb if mod_row is None else mod_row), 0, 0)

    def wmap(j, bb):
        return (layer, 0, 0)

    def tabmap(j, bb):
        return (j, 0)

    widths = (512, 128, 256, 256, 256, 256, 512, 512, 512)
    in_specs = [
        pl.BlockSpec((1, tm, d), xmap),
        pl.BlockSpec((None, 1, N_MOD, d), modmap),
        pl.BlockSpec((None, 1, d), wmap),
        pl.BlockSpec((None, d, IN_COLS), wmap),
        pl.BlockSpec((None, 1, C_Q_RANK), wmap),
        pl.BlockSpec((None, C_Q_RANK, 512), wmap),
        pl.BlockSpec((None, 1, C_KV_RANK), wmap),
        pl.BlockSpec((None, C_KV_RANK, 512), wmap),
        pl.BlockSpec((None, C_KV_RANK, 512), wmap),
    ] + [pl.BlockSpec((tm, LANES), tabmap) for _ in range(6)]
    return pl.pallas_call(
        functools.partial(_inproj_kernel, sub=min(tm, 128)),
        grid=(nt, b),
        in_specs=in_specs,
        out_specs=[pl.BlockSpec((1, tm, w), xmap) for w in widths],
        out_shape=[jax.ShapeDtypeStruct((b, s, w), BF16) for w in widths],
        compiler_params=_params("arbitrary", "arbitrary"),
        name="norm_inproj_rope",
    )(x, mods, g1, win, gq, wuq, gkv, wuk, wuv, *tabs)


def _lane_lo(shape):
    return lax.broadcasted_iota(jnp.int32, shape, 1) < HEAD_DIM


def _split_pair(q2):
    lo = _lane_lo(q2.shape)
    zero = jnp.zeros_like(q2)
    return jnp.concatenate([jnp.where(lo, q2, zero), jnp.where(lo, zero, q2)], axis=0)


def _merge_pair(o):
    n = o.shape[0] // 2
    return jnp.where(_lane_lo((n, o.shape[1])), o[:n], o[n:])


def _softmax_pv(parts, sink=None):
    m = parts[0][0].max(axis=-1, keepdims=True)
    for s, _ in parts[1:]:
        m = jnp.maximum(m, s.max(axis=-1, keepdims=True))
    if sink is not None:
        m = jnp.maximum(m, sink)
    denom = None
    acc = None
    for s, v in parts:
        e = jnp.exp2(s - m)
        r = e.sum(axis=-1, keepdims=True)
        denom = r if denom is None else denom + r
        o = _dot(e.astype(BF16), v)
        acc = o if acc is None else acc + o
    if sink is not None:
        denom = denom + jnp.exp2(sink - m)
    return acc / denom


def _attn_a_kernel(q_ref, k_ref, v_ref, kx_ref, vx_ref, sink_ref, o_ref, *, tq, seq):
    t = pl.program_id(1)
    nsub = q_ref.shape[1] // tq
    nloc = tq + 2 * A_WINDOW
    kx = kx_ref[0]
    lo = _lane_lo((tq, LANES))
    iota_rel = (lax.broadcasted_iota(jnp.int32, (tq, nloc), 1) - lax.broadcasted_iota(jnp.int32, (tq, nloc), 0))
    starts, bands = [], []
    for sb in range(nsub):
        qstart = (t * nsub + sb) * tq
        start = pl.multiple_of(jnp.clip(qstart - A_WINDOW, 0, seq - nloc), A_WINDOW)
        rel = iota_rel + (start - qstart)
        starts.append(start)
        bands.append(jnp.where(jnp.abs(rel) <= A_WINDOW, 0.0, NEG_INF))
    units = [(sb, j, g) for sb in range(nsub) for j in range(A_GROUP) for g in range(A_KV_HEADS)]

    def scores(u):
        sb, j, g = units[u]
        q2 = q_ref[0, sb * tq:(sb + 1) * tq, LANES * j:LANES * (j + 1)]
        zero = jnp.zeros_like(q2)
        q = jnp.where(lo, q2, zero) if g == 0 else jnp.where(lo, zero, q2)
        return _dot_t(q, k_ref[0, pl.ds(starts[sb], nloc), :]) + bands[sb], _dot_t(q, kx)

    outs = {}
    nxt = scores(0)
    for u, (sb, j, g) in enumerate(units):
        s_loc, s_ctx = nxt
        if u + 1 < len(units):
            nxt = scores(u + 1)
        vsg = v_ref[0, pl.ds(starts[sb], nloc), LANES * g:LANES * (g + 1)]
        vxg = vx_ref[0, :, LANES * g:LANES * (g + 1)]
        hd = g * A_GROUP + j
        sink = jnp.broadcast_to(sink_ref[hd:hd + 1, 0:1], (tq, 1))
        m = jnp.maximum(jnp.maximum(s_loc.max(axis=-1, keepdims=True), s_ctx.max(axis=-1, keepdims=True)), sink)
        o = _dot(jnp.exp2(s_loc - m).astype(BF16), vsg) + _dot(jnp.exp2(s_ctx - m).astype(BF16), vxg)
        denom = (o[:, HEAD_DIM:HEAD_DIM + 1] if g == 0 else o[:, 0:1]) + jnp.exp2(sink - m)
        outs[g] = o / denom
        if g == A_KV_HEADS - 1:
            o_ref[0, sb * tq:(sb + 1) * tq, LANES * j:LANES * (j + 1)] = jnp.where(lo, outs[0], outs[1]).astype(BF16)


def _attn_a(qa, ka, va, kax, vax, sinks, layer, tq, tstep):
    b, s, _ = qa.shape
    c = kax.shape[1]

    def qmap(bb, t):
        return (bb, t, 0)

    def kmap(bb, t):
        return (bb, 0, 0)

    return pl.pallas_call(
        functools.partial(_attn_a_kernel, tq=tq, seq=s),
        grid=(b, s // tstep),
        in_specs=[
            pl.BlockSpec((1, tstep, 512), qmap),
            pl.BlockSpec((1, s, LANES), kmap),
            pl.BlockSpec((1, s, 2 * LANES), kmap),
            pl.BlockSpec((1, c, LANES), kmap),
            pl.BlockSpec((1, c, 2 * LANES), kmap),
            pl.BlockSpec((None, A_HEADS, LANES), lambda bb, t: (layer, 0, 0)),
        ],
        out_specs=pl.BlockSpec((1, tstep, 512), qmap),
        out_shape=jax.ShapeDtypeStruct((b, s, 512), BF16),
        compiler_params=_params("arbitrary", "arbitrary"),
        name="attn_window_gqa",
    )(qa, ka, va, kax, vax, sinks)


def _attn_b_kernel(q_ref, k_ref, v_ref, kx_ref, vx_ref, bias_ref, o_ref, *, rows):
    pair, rb = pl.program_id(1), pl.program_id(2)
    kx, vx = kx_ref[0], vx_ref[0]
    nrow = q_ref.shape[1] // GRID_W

    def scores(r):
        qr = rb * nrow + r
        wr = jnp.clip(qr - NA_ROWS // 2, 0, rows - NA_ROWS)
        oi = wr - qr + (NA_ROWS - 1)
        start = pl.multiple_of(wr * GRID_W, GRID_W)
        q = _split_pair(q_ref[0, r * GRID_W:(r + 1) * GRID_W, :])
        ks = k_ref[0, pl.ds(start, NA_ROWS * GRID_W), :]
        bias = jnp.concatenate([bias_ref[pair, oi], bias_ref[pair, NA_ROWS + oi]], axis=0)
        return _dot_t(q, ks) + bias, _dot_t(q, kx), start

    nxt = scores(0)
    for r in range(nrow):
        s_loc, s_ctx, start = nxt
        if r + 1 < nrow:
            nxt = scores(r + 1)
        vs = v_ref[0, pl.ds(start, NA_ROWS * GRID_W), :]
        o = _softmax_pv([(s_loc, vs), (s_ctx, vx)])
        o_ref[0, r * GRID_W:(r + 1) * GRID_W, :] = _merge_pair(o).astype(BF16)


def _attn_b(qb, kb, vb, kbx, vbx, bias, layer):
    b, s, _ = qb.shape
    c = kbx.shape[1]
    rows = s // GRID_W
    tq = 2 * NA_ROWS * GRID_W

    def qmap(bb, p, rb):
        return (bb, rb, p)

    def kmap(bb, p, rb):
        return (bb, 0, p)

    return pl.pallas_call(
        functools.partial(_attn_b_kernel, rows=rows),
        grid=(b, B_HEADS // 2, s // tq),
        in_specs=[
            pl.BlockSpec((1, tq, LANES), qmap),
            pl.BlockSpec((1, s, LANES), kmap),
            pl.BlockSpec((1, s, LANES), kmap),
            pl.BlockSpec((1, c, LANES), kmap),
            pl.BlockSpec((1, c, LANES), kmap),
            pl.BlockSpec((None, B_HEADS // 2, 2 * NA_ROWS, GRID_W, NA_ROWS * GRID_W),
                         lambda bb, p, rb: (layer, 0, 0, 0, 0)),
        ],
        out_specs=pl.BlockSpec((1, tq, LANES), qmap),
        out_shape=jax.ShapeDtypeStruct((b, s, 256), BF16),
        compiler_params=_params("arbitrary", "arbitrary", "arbitrary"),
        name="attn_neighbourhood",
    )(qb, kb, vb, kbx, vbx, bias)


def _na_bias_table(rpb):
    depth, h = rpb.shape[:2]
    qc = jnp.arange(GRID_W)[:, None]
    kc = jnp.arange(GRID_W)[None, :]
    wc = jnp.clip(qc - NA_COLS // 2, 0, GRID_W - NA_COLS)
    valid = (kc >= wc) & (kc < wc + NA_COLS)
    pad = GRID_W - NA_COLS
    rp = jnp.pad(rpb, ((0, 0), (0, 0), (0, 0), (pad, pad)))
    toe = jnp.stack([rp[..., GRID_W - 1 - q:2 * GRID_W - 1 - q] for q in range(GRID_W)], axis=3)
    toe = jnp.where(valid, toe * LOG2E, NEG_INF)
    t = jnp.stack([toe[:, :, o:o + NA_ROWS] for o in range(NA_ROWS)], axis=2)
    t = t.transpose(0, 1, 2, 4, 3, 5).reshape(depth, h // 2, 2 * NA_ROWS, GRID_W, NA_ROWS * GRID_W)
    return t.astype(F32)


def _attn_c_kernel(q_ref, k_ref, kx_ref, v_ref, vx_ref, o_ref, *, kc):
    s = k_ref.shape[1]
    chunks = [(k_ref, v_ref, c0, kc) for c0 in range(0, s, kc)] + [(kx_ref, vx_ref, 0, kx_ref.shape[1])]
    qs = [q_ref[0, :, LANES * hh:LANES * (hh + 1)] for hh in range(2)]
    m = [None, None]
    acc = [None, None]

    def scores(ci):
        kr, _, c0, n = chunks[ci]
        return [_dot_t(qs[hh], kr[0, c0:c0 + n, LANES * hh:LANES * (hh + 1)]) for hh in range(2)]

    nxt = scores(0)
    for ci, (_, vr, c0, n) in enumerate(chunks):
        cur = nxt
        if ci + 1 < len(chunks):
            nxt = scores(ci + 1)
        for hh in range(2):
            sl = slice(LANES * hh, LANES * (hh + 1))
            sc = cur[hh]
            cm = sc.max(axis=-1, keepdims=True)
            if m[hh] is None:
                m[hh] = cm
                acc[hh] = _dot(jnp.exp2(sc - cm).astype(BF16), vr[0, c0:c0 + n, sl])
            else:
                m_new = jnp.maximum(m[hh], cm)
                acc[hh] = (jnp.exp2(m[hh] - m_new) * acc[hh]
                           + _dot(jnp.exp2(sc - m_new).astype(BF16), vr[0, c0:c0 + n, sl]))
                m[hh] = m_new
    o0 = acc[0] / acc[0][:, C_V:C_V + 1]
    o1 = acc[1] / acc[1][:, C_V:C_V + 1]
    o_ref[0] = jnp.where(_lane_lo(o0.shape), o0, pltpu.roll(o1, C_V, 1)).astype(BF16)


def _attn_c(qc, kc, vc, kcx, vcx, tq):
    b, s, _ = qc.shape
    c = kcx.shape[1]

    def qmap(bb, p, i):
        return (bb, i, p)

    def kmap(bb, p, i):
        return (bb, 0, p)

    return pl.pallas_call(
        functools.partial(_attn_c_kernel, kc=512),
        grid=(b, C_HEADS // 2, s // tq),
        in_specs=[
            pl.BlockSpec((1, tq, 2 * LANES), qmap),
            pl.BlockSpec((1, s, 2 * LANES), kmap),
            pl.BlockSpec((1, c, 2 * LANES), kmap),
            pl.BlockSpec((1, s, 2 * LANES), kmap),
            pl.BlockSpec((1, c, 2 * LANES), kmap),
        ],
        out_specs=pl.BlockSpec((1, tq, LANES), qmap),
        out_shape=jax.ShapeDtypeStruct((b, s, 256), BF16),
        compiler_params=_params("arbitrary", "arbitrary", "arbitrary"),
        name="attn_latent",
    )(qc, kc, kcx, vc, vcx)


def _attn_ctx_kernel(qa_ref, ka_ref, va_ref, qb_ref, kb_ref, vb_ref, qc_ref, kc_ref, vc_ref,
                     sink_ref, oa_ref, ob_ref, oc_ref):
    c = qa_ref.shape[1]
    half = A_HEADS // 2
    lo = _lane_lo((c, LANES))

    def blk(i):
        return slice(LANES * i, LANES * (i + 1))

    units = []
    for j in range(half):
        for g in range(A_KV_HEADS):
            def score(j=j, g=g):
                q2 = qa_ref[0, :, blk(j)]
                zero = jnp.zeros_like(q2)
                return _dot_t(jnp.where(lo, q2, zero) if g == 0 else jnp.where(lo, zero, q2), ka_ref[0])

            def fin_a(o, j=j):
                oa_ref[0, :, blk(j)] = jnp.where(lo, o[0], o[1]).astype(BF16)

            units.append((score, lambda g=g: va_ref[0, :, blk(g)], j + half * g, fin_a if g == 1 else None))
    for p in range(B_HEADS // 2):
        def score(p=p):
            return _dot_t(_split_pair(qb_ref[0, :, blk(p)]), kb_ref[0, :, blk(p)])

        def fin_b(o, p=p):
            ob_ref[0, :, blk(p)] = _merge_pair(o[0]).astype(BF16)

        units.append((score, lambda p=p: vb_ref[0, :, blk(p)], None, fin_b))
    for hd in range(C_HEADS):
        def score(hd=hd):
            return _dot_t(qc_ref[0, :, blk(hd)], kc_ref[0, :, blk(hd)])

        def fin_c(o, hd=hd):
            oc_ref[0, :, blk(hd // 2)] = jnp.where(lo, o[0], pltpu.roll(o[1], C_V, 1)).astype(BF16)

        units.append((score, lambda hd=hd: vc_ref[0, :, blk(hd)], None, fin_c if hd % 2 == 1 else None))

    pending = []
    nxt = units[0][0]()
    for u, (_, value, sink_head, finish) in enumerate(units):
        s = nxt
        if u + 1 < len(units):
            nxt = units[u + 1][0]()
        sink = None
        if sink_head is not None:
            sink = jnp.broadcast_to(sink_ref[sink_head:sink_head + 1, 0:1], (s.shape[0], 1))
        pending.append(_softmax_pv([(s, value())], sink))
        if finish is not None:
            finish(pending)
            pending = []


def _attn_ctx(parts, sinks, layer):
    b, c, _ = parts[0].shape
    widths = [a.shape[2] for a in parts]

    def bmap(bb):
        return (bb, 0, 0)

    return pl.pallas_call(
        _attn_ctx_kernel,
        grid=(b,),
        in_specs=[pl.BlockSpec((1, c, w), bmap) for w in widths]
        + [pl.BlockSpec((None, A_HEADS, LANES), lambda bb: (layer, 0, 0))],
        out_specs=[pl.BlockSpec((1, c, w), bmap) for w in (512, 256, 256)],
        out_shape=[jax.ShapeDtypeStruct((b, c, w), BF16) for w in (512, 256, 256)],
        compiler_params=_params("arbitrary"),
        name="attn_context",
    )(*parts, sinks)


def _outproj_mlp_kernel(x_ref, ma_ref, mb_ref, mc_ref, mod_ref, woa_ref, wob_ref, woc_ref,
                        g2_ref, w1_ref, w2_ref, gf_ref, o_ref, *, final, ff_chunk):
    mod = mod_ref[0]
    attn = _dot(ma_ref[0], woa_ref[...]) + _dot(mb_ref[0], wob_ref[...]) + _dot(mc_ref[0], woc_ref[...])
    x1 = x_ref[0] + mod[2:3] * attn
    h = (_rms(x1, g2_ref[...]) * (1.0 + mod[4:5]) + mod[3:4]).astype(BF16)
    d_ff = w1_ref.shape[1]
    def up(c0):
        return _dot(h, w1_ref[:, c0:c0 + ff_chunk])

    y = None
    chunk_starts = list(range(0, d_ff, ff_chunk))
    nxt = up(chunk_starts[0])
    for i, c0 in enumerate(chunk_starts):
        u = jnp.maximum(nxt, 0.0)
        if i + 1 < len(chunk_starts):
            nxt = up(chunk_starts[i + 1])
        part = _dot((u * u).astype(BF16), w2_ref[c0:c0 + ff_chunk, :])
        y = part if y is None else y + part
    x2 = x1 + mod[5:6] * y
    if final:
        x2 = _rms(x2, gf_ref[...])
    o_ref[0] = x2


def _outproj_mlp(x, ma, mb, mc, layer, mod_row, mods, woa, wob, woc, g2, w1, w2, gf, tm, final):
    b, s, d = x.shape
    d_ff = w1.shape[2]

    def xmap(bb, j):
        return (bb, j, 0)

    def modmap(bb, j):
        return (layer, (bb if mod_row is None else mod_row), 0, 0)

    def resident(shape):
        return pl.BlockSpec((None,) + shape, lambda bb, j: (layer, 0, 0), pipeline_mode=pl.Buffered(1))

    return pl.pallas_call(
        functools.partial(_outproj_mlp_kernel, final=final, ff_chunk=1024),
        grid=(b, s // tm),
        in_specs=[
            pl.BlockSpec((1, tm, d), xmap),
            pl.BlockSpec((1, tm, 512), xmap),
            pl.BlockSpec((1, tm, 256), xmap),
            pl.BlockSpec((1, tm, 256), xmap),
            pl.BlockSpec((None, 1, N_MOD, d), modmap),
            resident((512, d)), resident((256, d)), resident((256, d)),
            resident((1, d)),
            resident((d, d_ff)), resident((d_ff, d)),
            pl.BlockSpec((1, d), lambda bb, j: (0, 0)),
        ],
        out_specs=pl.BlockSpec((1, tm, d), xmap),
        out_shape=jax.ShapeDtypeStruct((b, s, d), F32),
        compiler_params=_params("arbitrary", "arbitrary"),
        name="outproj_mlp",
    )(x, ma, mb, mc, mods, woa, wob, woc, g2, w1, w2, gf)


def _rope_tables(s):
    tok = jnp.arange(s)
    row, col = (tok // GRID_W).astype(F32), (tok % GRID_W).astype(F32)

    def cs(pos, half):
        freqs = ROPE_BASE ** (-jnp.arange(half, dtype=F32) / half)
        ang = pos[:, None] * freqs
        return jnp.cos(ang), jnp.sin(ang)

    cr, sr = cs(row, 16)
    cc, sc = cs(col, 16)
    z = jnp.zeros_like(sr)
    a_c = jnp.tile(jnp.concatenate([cr, cr, cc, cc], axis=1), (1, 2))
    a_s1 = jnp.tile(jnp.concatenate([-sr, z, -sc, z], axis=1), (1, 2))
    a_s2 = jnp.tile(jnp.concatenate([z, sr, z, sc], axis=1), (1, 2))
    cr, sr = cs(row, 8)
    cc, sc = cs(col, 8)
    z = jnp.zeros_like(sr)
    one64, zero64 = jnp.ones((s, C_NOPE), F32), jnp.zeros((s, C_NOPE), F32)
    one32, zero32 = jnp.ones((s, 32), F32), jnp.zeros((s, 32), F32)
    c_c = jnp.concatenate([one64, cr, cr, cc, cc, one32], axis=1)
    c_s1 = jnp.concatenate([zero64, -sr, z, -sc, z, zero32], axis=1)
    c_s2 = jnp.concatenate([zero64, z, sr, z, sc, zero32], axis=1)
    return (a_c, a_s1, a_s2, c_c, c_s1, c_s2)


def _identity_tables(s):
    one, zero = jnp.ones((s, LANES), F32), jnp.zeros((s, LANES), F32)
    return (one, zero, zero, one, zero, zero)


def _pad_heads(w, offs, width):
    pad = jnp.zeros(w.shape[:2] + (LANES - width,), w.dtype)
    return jnp.concatenate([jnp.concatenate([w[:, :, o:o + width], pad], axis=2) for o in offs], axis=2)


def _weight_layouts(w_in, w_uq, w_ukv, w_out):
    depth, d, _ = w_in.shape
    qa_w = w_in[:, :, :512].reshape(depth, d, A_KV_HEADS, A_GROUP, HEAD_DIM).transpose(0, 1, 3, 2, 4)
    qa_w = qa_w.reshape(depth, d, 512)
    z = lambda n: jnp.zeros((depth, d, n), w_in.dtype)
    kr_w = jnp.concatenate([z(64), w_in[:, :, 1920:1952], z(32)], axis=2)
    win = jnp.concatenate([qa_w, w_in[:, :, 512:1920], kr_w], axis=2).astype(BF16)
    hq, hk = C_NOPE + C_ROPE, C_NOPE + C_V
    wuq = _pad_heads(w_uq, [hq * h for h in range(C_HEADS)], hq).astype(BF16)
    wuk = _pad_heads(w_ukv, [hk * h for h in range(C_HEADS)], C_NOPE).astype(BF16)
    wuv = _pad_heads(w_ukv, [hk * h + C_NOPE for h in range(C_HEADS)], C_V).astype(BF16)
    woa = w_out[:, :512].reshape(depth, A_KV_HEADS, A_GROUP, HEAD_DIM, d).transpose(0, 2, 1, 3, 4)
    woa = woa.reshape(depth, 512, d).astype(BF16)
    wob = w_out[:, 512:768].astype(BF16)
    woc = w_out[:, 768:].astype(BF16)
    return win, wuq, wuk, wuv, woa, wob, woc


def kernel(x, c, ctx, c_ctx, w_ada, b_ada, norm1_g, norm2_g, w_in, attn_sink, na_rpb, mla_q_norm_g,
           mla_w_uq, mla_kv_norm_g, mla_w_ukv, w_out, w_mlp_in, w_mlp_out, final_norm_g):
    b, s, d = x.shape
    n_ctx = ctx.shape[1]
    depth = w_ada.shape[0]
    assert b + 1 <= MOD_ROWS and s % 512 == 0 and n_ctx % 128 == 0
    tm_x = 512
    tm_c = min(n_ctx, 256)
    tq_a = 256
    tq_c = 512

    cvec = jnp.concatenate([c, c_ctx[None], jnp.zeros((MOD_ROWS - b - 1, d), c.dtype)], axis=0)
    mods = _modulation(cvec, w_ada, b_ada).reshape(depth, MOD_ROWS, N_MOD, d)
    tabs_x = _rope_tables(s)
    tabs_c = _identity_tables(n_ctx)
    gf = final_norm_g.reshape(1, d)
    win, wuq, wuk, wuv, woa, wob, woc = _weight_layouts(w_in, mla_w_uq, mla_w_ukv, w_out)
    w1, w2 = w_mlp_in.astype(BF16), w_mlp_out.astype(BF16)
    sinks = jnp.broadcast_to((attn_sink * LOG2E)[:, :, None], (depth, A_HEADS, LANES)).astype(F32)
    bias = _na_bias_table(na_rpb)
    g1, g2 = norm1_g.reshape(depth, 1, d), norm2_g.reshape(depth, 1, d)
    gq, gkv = mla_q_norm_g.reshape(depth, 1, -1), mla_kv_norm_g.reshape(depth, 1, -1)

    for l in range(depth):
        last = l == depth - 1
        xs = _inproj(x, l, None, mods, g1, win, gq, wuq, gkv, wuk, wuv, tabs_x, 2 * tm_x)
        cs = _inproj(ctx, l, b, mods, g1, win, gq, wuq, gkv, wuk, wuv, tabs_c, tm_c)
        qa, ka, va, qb, kb, vb, qc, kc, vc = xs
        cqa, cka, cva, cqb, ckb, cvb, cqc, ckc, cvc = cs

        oa = _attn_a(qa, ka, va, cka, cva, sinks, l, tq_a, 2 * tq_a)
        ob = _attn_b(qb, kb, vb, ckb, cvb, bias, l)
        oc = _attn_c(qc, kc, vc, ckc, cvc, tq_c)
        x = _outproj_mlp(x, oa, ob, oc, l, None, mods, woa, wob, woc, g2, w1, w2, gf, tm_x, last)
        if not last:
            coa, cob, coc = _attn_ctx(cs, sinks, l)
            ctx = _outproj_mlp(ctx, coa, cob, coc, l, b, mods, woa, wob, woc, g2, w1, w2, gf, tm_c, False)
    return x
```

```python
import functools
import math

import jax
import jax.numpy as jnp
from jax import lax
from jax.experimental import pallas as pl
from jax.experimental.pallas import tpu as pltpu

F32 = jnp.float32
BF16 = jnp.bfloat16

GRID_W = 64
HEAD_DIM = 64
A_HEADS = 8
A_KV_HEADS = 2
A_GROUP = A_HEADS // A_KV_HEADS
A_WINDOW = 128
B_HEADS = 4
NA_ROWS = 8
NA_COLS = 16
C_HEADS = 4
C_Q_RANK = 256
C_KV_RANK = 128
C_NOPE = 64
C_ROPE = 32
C_V = 64
N_MOD = 6
ROPE_BASE = 10000.0
EPS = 1e-6
NEG_INF = -1e30
LOG2E = math.log2(math.e)

LANES = 128
MOD_ROWS = 16
IN_COLS = 2048
VMEM_LIMIT = 56 * 1024 * 1024

_OFF_QA, _OFF_KA, _OFF_VA = 0, 512, 640
_OFF_QB, _OFF_KB, _OFF_VB = 768, 1024, 1280
_OFF_CQ, _OFF_CKV, _OFF_KR = 1536, 1792, 1920


def _dot(a, b):
    return jnp.dot(a, b, preferred_element_type=F32)


def _dot_t(a, b):
    return lax.dot_general(a, b, (((1,), (1,)), ((), ())), preferred_element_type=F32)


def _params(*sem):
    return pltpu.CompilerParams(dimension_semantics=sem, vmem_limit_bytes=VMEM_LIMIT)


def _mod_kernel(c_ref, w_ref, b_ref, o_ref):
    c = c_ref[...]
    s = c * (1.0 / (1.0 + jnp.exp(-c)))
    o_ref[0] = _dot(s.astype(BF16), w_ref[0].astype(BF16)) + b_ref[0]


def _modulation(cvec, w_ada, b_ada):
    depth, d, n = w_ada.shape
    tn = 1024
    return pl.pallas_call(
        _mod_kernel,
        grid=(depth, n // tn),
        in_specs=[
            pl.BlockSpec((MOD_ROWS, d), lambda l, j: (0, 0)),
            pl.BlockSpec((1, d, tn), lambda l, j: (l, 0, j)),
            pl.BlockSpec((1, 1, tn), lambda l, j: (l, 0, j)),
        ],
        out_specs=pl.BlockSpec((1, MOD_ROWS, tn), lambda l, j: (l, 0, j)),
        out_shape=jax.ShapeDtypeStruct((depth, MOD_ROWS, n), F32),
        compiler_params=_params("arbitrary", "arbitrary"),
        name="adaln_modulation",
    )(cvec, w_ada, b_ada.reshape(depth, 1, n))


def _rope(x, c, s1, s2, shift):
    return x * c + pltpu.roll(x, LANES - shift, 1) * s1 + pltpu.roll(x, shift, 1) * s2


def _rms(x, g):
    return x * lax.rsqrt(jnp.mean(x * x, axis=-1, keepdims=True) + EPS) * g


def _inproj_kernel(x_ref, mod_ref, g1_ref, win_ref, gq_ref, wuq_ref, gkv_ref, wuk_ref, wuv_ref,
                   ac_ref, as1_ref, as2_ref, cc_ref, cs1_ref, cs2_ref,
                   qa_ref, ka_ref, va_ref, qb_ref, kb_ref, vb_ref, qc_ref, kc_ref, vc_ref, *, sub):
    tm = x_ref.shape[1]
    mod = mod_ref[0]
    scale_ab = HEAD_DIM ** -0.5 * LOG2E
    scale_c = (C_NOPE + C_ROPE) ** -0.5 * LOG2E
    ones_lane = (lax.broadcasted_iota(jnp.int32, (1, LANES), 1) == C_V).astype(F32)

    subs = [slice(r0, r0 + sub) for r0 in range(0, tm, sub)]
    def up_project(pc):
        cq = _rms(pc[:, 0:C_Q_RANK], gq_ref[...]).astype(BF16)
        ckv = _rms(pc[:, C_Q_RANK:C_Q_RANK + C_KV_RANK], gkv_ref[...]).astype(BF16)
        return _dot(cq, wuq_ref[...]), _dot(ckv, wuk_ref[...]), _dot(ckv, wuv_ref[...])

    pabs, pcs, ups = [], [], []
    for rs in subs:
        h = (_rms(x_ref[0, rs, :], g1_ref[...]) * (1.0 + mod[1:2]) + mod[0:1]).astype(BF16)
        pcs.append(_dot(h, win_ref[:, _OFF_CQ:IN_COLS]))
        pabs.append(_dot(h, win_ref[:, 0:_OFF_CQ]))
        ups.append(up_project(pcs[-1]))

    for rs, p, pc, (q_up, k_up, v_up) in zip(subs, pabs, pcs, ups):
        ac, as1, as2 = ac_ref[rs, :], as1_ref[rs, :], as2_ref[rs, :]
        cc, cs1, cs2 = cc_ref[rs, :], cs1_ref[rs, :], cs2_ref[rs, :]
        for j in range(4):
            blk = p[:, _OFF_QA + LANES * j:_OFF_QA + LANES * (j + 1)]
            qa_ref[0, rs, LANES * j:LANES * (j + 1)] = (_rope(blk, ac, as1, as2, 16) * scale_ab).astype(BF16)
        ka_ref[0, rs, :] = _rope(p[:, _OFF_KA:_OFF_KA + LANES], ac, as1, as2, 16).astype(BF16)
        va = p[:, _OFF_VA:_OFF_VA + LANES]
        lane = lax.broadcasted_iota(jnp.int32, va.shape, 1)
        va_ref[0, rs, 0:LANES] = jnp.where(lane < HEAD_DIM, va, jnp.where(lane == HEAD_DIM, 1.0, 0.0)).astype(BF16)
        va_ref[0, rs, LANES:2 * LANES] = jnp.where(lane >= HEAD_DIM, va, jnp.where(lane == 0, 1.0, 0.0)).astype(BF16)
        qb_ref[0, rs, :] = (p[:, _OFF_QB:_OFF_QB + 256] * scale_ab).astype(BF16)
        kb_ref[0, rs, :] = p[:, _OFF_KB:_OFF_KB + 256].astype(BF16)
        vb_ref[0, rs, :] = p[:, _OFF_VB:_OFF_VB + 256].astype(BF16)
        kr = _rope(pc[:, _OFF_KR - _OFF_CQ:], cc, cs1, cs2, 8)
        for hd in range(C_HEADS):
            sl = slice(LANES * hd, LANES * (hd + 1))
            qc_ref[0, rs, sl] = (_rope(q_up[:, sl], cc, cs1, cs2, 8) * scale_c).astype(BF16)
            kc_ref[0, rs, sl] = (k_up[:, sl] + kr).astype(BF16)
            vc_ref[0, rs, sl] = (v_up[:, sl] + ones_lane).astype(BF16)


def _inproj(x, layer, mod_row, mods, g1, win, gq, wuq, gkv, wuk, wuv, tabs, tm):
    b, s, d = x.shape
    nt = s // tm

    def xmap(j, bb):
        return (bb, j, 0)

    def modmap(j, bb):
        return (layer, (bb if mod_row is None else mod_row), 0, 0)

    def wmap(j, bb):
        return (layer, 0, 0)

    def tabmap(j, bb):
        return (j, 0)

    widths = (512, 128, 256, 256, 256, 256, 512, 512, 512)
    in_specs = [
        pl.BlockSpec((1, tm, d), xmap),
        pl.BlockSpec((None, 1, N_MOD, d), modmap),
        pl.BlockSpec((None, 1, d), wmap),
        pl.BlockSpec((None, d, IN_COLS), wmap),
        pl.BlockSpec((None, 1, C_Q_RANK), wmap),
        pl.BlockSpec((None, C_Q_RANK, 512), wmap),
        pl.BlockSpec((None, 1, C_KV_RANK), wmap),
        pl.BlockSpec((None, C_KV_RANK, 512), wmap),
        pl.BlockSpec((None, C_KV_RANK, 512), wmap),
    ] + [pl.BlockSpec((tm, LANES), tabmap) for _ in range(6)]
    return pl.pallas_call(
        functools.partial(_inproj_kernel, sub=min(tm, 128)),
        grid=(nt, b),
        in_specs=in_specs,
        out_specs=[pl.BlockSpec((1, tm, w), xmap) for w in widths],
        out_shape=[jax.ShapeDtypeStruct((b, s, w), BF16) for w in widths],
        compiler_params=_params("arbitrary", "arbitrary"),
        name="norm_inproj_rope",
    )(x, mods, g1, win, gq, wuq, gkv, wuk, wuv, *tabs)


def _lane_lo(shape):
    return lax.broadcasted_iota(jnp.int32, shape, 1) < HEAD_DIM


def _split_pair(q2):
    lo = _lane_lo(q2.shape)
    zero = jnp.zeros_like(q2)
    return jnp.concatenate([jnp.where(lo, q2, zero), jnp.where(lo, zero, q2)], axis=0)


def _merge_pair(o):
    n = o.shape[0] // 2
    return jnp.where(_lane_lo((n, o.shape[1])), o[:n], o[n:])


def _softmax_pv(parts, sink=None):
    m = parts[0][0].max(axis=-1, keepdims=True)
    for s, _ in parts[1:]:
        m = jnp.maximum(m, s.max(axis=-1, keepdims=True))
    if sink is not None:
        m = jnp.maximum(m, sink)
    denom = None
    acc = None
    for s, v in parts:
        e = jnp.exp2(s - m)
        r = e.sum(axis=-1, keepdims=True)
        denom = r if denom is None else denom + r
        o = _dot(e.astype(BF16), v)
        acc = o if acc is None else acc + o
    if sink is not None:
        denom = denom + jnp.exp2(sink - m)
    return acc / denom


def _attn_a_kernel(q_ref, k_ref, v_ref, kx_ref, vx_ref, sink_ref, o_ref, *, tq, seq):
    t = pl.program_id(1)
    nsub = q_ref.shape[1] // tq
    nloc = tq + 2 * A_WINDOW
    kx = kx_ref[0]
    lo = _lane_lo((tq, LANES))
    iota_rel = (lax.broadcasted_iota(jnp.int32, (tq, nloc), 1) - lax.broadcasted_iota(jnp.int32, (tq, nloc), 0))
    starts, bands = [], []
    for sb in range(nsub):
        qstart = (t * nsub + sb) * tq
        start = pl.multiple_of(jnp.clip(qstart - A_WINDOW, 0, seq - nloc), A_WINDOW)
        rel = iota_rel + (start - qstart)
        starts.append(start)
        bands.append(jnp.where(jnp.abs(rel) <= A_WINDOW, 0.0, NEG_INF))
    units = [(sb, j, g) for sb in range(nsub) for j in range(A_GROUP) for g in range(A_KV_HEADS)]

    def scores(u):
        sb, j, g = units[u]
        q2 = q_ref[0, sb * tq:(sb + 1) * tq, LANES * j:LANES * (j + 1)]
        zero = jnp.zeros_like(q2)
        q = jnp.where(lo, q2, zero) if g == 0 else jnp.where(lo, zero, q2)
        return _dot_t(q, k_ref[0, pl.ds(starts[sb], nloc), :]) + bands[sb], _dot_t(q, kx)

    outs = {}
    nxt = scores(0)
    for u, (sb, j, g) in enumerate(units):
        s_loc, s_ctx = nxt
        if u + 1 < len(units):
            nxt = scores(u + 1)
        vsg = v_ref[0, pl.ds(starts[sb], nloc), LANES * g:LANES * (g + 1)]
        vxg = vx_ref[0, :, LANES * g:LANES * (g + 1)]
        hd = g * A_GROUP + j
        sink = jnp.broadcast_to(sink_ref[hd:hd + 1, 0:1], (tq, 1))
        m = jnp.maximum(jnp.maximum(s_loc.max(axis=-1, keepdims=True), s_ctx.max(axis=-1, keepdims=True)), sink)
        o = _dot(jnp.exp2(s_loc - m).astype(BF16), vsg) + _dot(jnp.exp2(s_ctx - m).astype(BF16), vxg)
        denom = (o[:, HEAD_DIM:HEAD_DIM + 1] if g == 0 else o[:, 0:1]) + jnp.exp2(sink - m)
        outs[g] = o / denom
        if g == A_KV_HEADS - 1:
            o_ref[0, sb * tq:(sb + 1) * tq, LANES * j:LANES * (j + 1)] = jnp.where(lo, outs[0], outs[1]).astype(BF16)


def _attn_a(qa, ka, va, kax, vax, sinks, layer, tq, tstep):
    b, s, _ = qa.shape
    c = kax.shape[1]

    def qmap(bb, t):
        return (bb, t, 0)

    def kmap(bb, t):
        return (bb, 0, 0)

    return pl.pallas_call(
        functools.partial(_attn_a_kernel, tq=tq, seq=s),
        grid=(b, s // tstep),
        in_specs=[
            pl.BlockSpec((1, tstep, 512), qmap),
            pl.BlockSpec((1, s, LANES), kmap),
            pl.BlockSpec((1, s, 2 * LANES), kmap),
            pl.BlockSpec((1, c, LANES), kmap),
            pl.BlockSpec((1, c, 2 * LANES), kmap),
            pl.BlockSpec((None, A_HEADS, LANES), lambda bb, t: (layer, 0, 0)),
        ],
        out_specs=pl.BlockSpec((1, tstep, 512), qmap),
        out_shape=jax.ShapeDtypeStruct((b, s, 512), BF16),
        compiler_params=_params("arbitrary", "arbitrary"),
        name="attn_window_gqa",
    )(qa, ka, va, kax, vax, sinks)


def _attn_b_kernel(q_ref, k_ref, v_ref, kx_ref, vx_ref, bias_ref, o_ref, *, rows):
    pair, rb = pl.program_id(1), pl.program_id(2)
    kx, vx = kx_ref[0], vx_ref[0]
    nrow = q_ref.shape[1] // GRID_W

    def scores(r):
        qr = rb * nrow + r
        wr = jnp.clip(qr - NA_ROWS // 2, 0, rows - NA_ROWS)
        oi = wr - qr + (NA_ROWS - 1)
        start = pl.multiple_of(wr * GRID_W, GRID_W)
        q = _split_pair(q_ref[0, r * GRID_W:(r + 1) * GRID_W, :])
        ks = k_ref[0, pl.ds(start, NA_ROWS * GRID_W), :]
        bias = jnp.concatenate([bias_ref[pair, oi], bias_ref[pair, NA_ROWS + oi]], axis=0)
        return _dot_t(q, ks) + bias, _dot_t(q, kx), start

    nxt = scores(0)
    for r in range(nrow):
        s_loc, s_ctx, start = nxt
        if r + 1 < nrow:
            nxt = scores(r + 1)
        vs = v_ref[0, pl.ds(start, NA_ROWS * GRID_W), :]
        o = _softmax_pv([(s_loc, vs), (s_ctx, vx)])
        o_ref[0, r * GRID_W:(r + 1) * GRID_W, :] = _merge_pair(o).astype(BF16)


def _attn_b(qb, kb, vb, kbx, vbx, bias, layer):
    b, s, _ = qb.shape
    c = kbx.shape[1]
    rows = s // GRID_W
    tq = min(4 * NA_ROWS * GRID_W, s)

    def qmap(bb, p, rb):
        return (bb, rb, p)

    def kmap(bb, p, rb):
        return (bb, 0, p)

    return pl.pallas_call(
        functools.partial(_attn_b_kernel, rows=rows),
        grid=(b, B_HEADS // 2, s // tq),
        in_specs=[
            pl.BlockSpec((1, tq, LANES), qmap),
            pl.BlockSpec((1, s, LANES), kmap),
            pl.BlockSpec((1, s, LANES), kmap),
            pl.BlockSpec((1, c, LANES), kmap),
            pl.BlockSpec((1, c, LANES), kmap),
            pl.BlockSpec((None, B_HEADS // 2, 2 * NA_ROWS, GRID_W, NA_ROWS * GRID_W),
                         lambda bb, p, rb: (layer, 0, 0, 0, 0)),
        ],
        out_specs=pl.BlockSpec((1, tq, LANES), qmap),
        out_shape=jax.ShapeDtypeStruct((b, s, 256), BF16),
        compiler_params=_params("arbitrary", "arbitrary", "arbitrary"),
        name="attn_neighbourhood",
    )(qb, kb, vb, kbx, vbx, bias)


def _na_bias_table(rpb):
    depth, h = rpb.shape[:2]
    qc = jnp.arange(GRID_W)[:, None]
    kc = jnp.arange(GRID_W)[None, :]
    wc = jnp.clip(qc - NA_COLS // 2, 0, GRID_W - NA_COLS)
    valid = (kc >= wc) & (kc < wc + NA_COLS)
    pad = GRID_W - NA_COLS
    rp = jnp.pad(rpb, ((0, 0), (0, 0), (0, 0), (pad, pad)))
    toe = jnp.stack([rp[..., GRID_W - 1 - q:2 * GRID_W - 1 - q] for q in range(GRID_W)], axis=3)
    toe = jnp.where(valid, toe * LOG2E, NEG_INF)
    t = jnp.stack([toe[:, :, o:o + NA_ROWS] for o in range(NA_ROWS)], axis=2)
    t = t.transpose(0, 1, 2, 4, 3, 5).reshape(depth, h // 2, 2 * NA_ROWS, GRID_W, NA_ROWS * GRID_W)
    return t.astype(F32)


def _attn_c_kernel(q_ref, k_ref, kx_ref, v_ref, vx_ref, o_ref, *, kc):
    s = k_ref.shape[1]
    chunks = [(k_ref, v_ref, c0, kc) for c0 in range(0, s, kc)] + [(kx_ref, vx_ref, 0, kx_ref.shape[1])]
    qs = [q_ref[0, :, LANES * hh:LANES * (hh + 1)] for hh in range(2)]
    m = [None, None]
    acc = [None, None]

    def scores(ci):
        kr, _, c0, n = chunks[ci]
        return [_dot_t(qs[hh], kr[0, c0:c0 + n, LANES * hh:LANES * (hh + 1)]) for hh in range(2)]

    nxt = scores(0)
    for ci, (_, vr, c0, n) in enumerate(chunks):
        cur = nxt
        if ci + 1 < len(chunks):
            nxt = scores(ci + 1)
        for hh in range(2):
            sl = slice(LANES * hh, LANES * (hh + 1))
            sc = cur[hh]
            cm = sc.max(axis=-1, keepdims=True)
            if m[hh] is None:
                m[hh] = cm
                acc[hh] = _dot(jnp.exp2(sc - cm).astype(BF16), vr[0, c0:c0 + n, sl])
            else:
                m_new = jnp.maximum(m[hh], cm)
                acc[hh] = (jnp.exp2(m[hh] - m_new) * acc[hh]
                           + _dot(jnp.exp2(sc - m_new).astype(BF16), vr[0, c0:c0 + n, sl]))
                m[hh] = m_new
    o0 = acc[0] / acc[0][:, C_V:C_V + 1]
    o1 = acc[1] / acc[1][:, C_V:C_V + 1]
    o_ref[0] = jnp.where(_lane_lo(o0.shape), o0, pltpu.roll(o1, C_V, 1)).astype(BF16)


def _attn_c(qc, kc, vc, kcx, vcx, tq):
    b, s, _ = qc.shape
    c = kcx.shape[1]

    def qmap(bb, p, i):
        return (bb, i, p)

    def kmap(bb, p, i):
        return (bb, 0, p)

    return pl.pallas_call(
        functools.partial(_attn_c_kernel, kc=512),
        grid=(b, C_HEADS // 2, s // tq),
        in_specs=[
            pl.BlockSpec((1, tq, 2 * LANES), qmap),
            pl.BlockSpec((1, s, 2 * LANES), kmap),
            pl.BlockSpec((1, c, 2 * LANES), kmap),
            pl.BlockSpec((1, s, 2 * LANES), kmap),
            pl.BlockSpec((1, c, 2 * LANES), kmap),
        ],
        out_specs=pl.BlockSpec((1, tq, LANES), qmap),
        out_shape=jax.ShapeDtypeStruct((b, s, 256), BF16),
        compiler_params=_params("arbitrary", "arbitrary", "arbitrary"),
        name="attn_latent",
    )(qc, kc, kcx, vc, vcx)


def _attn_ctx_kernel(qa_ref, ka_ref, va_ref, qb_ref, kb_ref, vb_ref, qc_ref, kc_ref, vc_ref,
                     sink_ref, oa_ref, ob_ref, oc_ref):
    c = qa_ref.shape[1]
    half = A_HEADS // 2
    lo = _lane_lo((c, LANES))

    def blk(i):
        return slice(LANES * i, LANES * (i + 1))

    units = []
    for j in range(half):
        for g in range(A_KV_HEADS):
            def score(j=j, g=g):
                q2 = qa_ref[0, :, blk(j)]
                zero = jnp.zeros_like(q2)
                return _dot_t(jnp.where(lo, q2, zero) if g == 0 else jnp.where(lo, zero, q2), ka_ref[0])

            def fin_a(o, j=j):
                oa_ref[0, :, blk(j)] = jnp.where(lo, o[0], o[1]).astype(BF16)

            units.append((score, lambda g=g: va_ref[0, :, blk(g)], j + half * g, fin_a if g == 1 else None))
    for p in range(B_HEADS // 2):
        def score(p=p):
            return _dot_t(_split_pair(qb_ref[0, :, blk(p)]), kb_ref[0, :, blk(p)])

        def fin_b(o, p=p):
            ob_ref[0, :, blk(p)] = _merge_pair(o[0]).astype(BF16)

        units.append((score, lambda p=p: vb_ref[0, :, blk(p)], None, fin_b))
    for hd in range(C_HEADS):
        def score(hd=hd):
            return _dot_t(qc_ref[0, :, blk(hd)], kc_ref[0, :, blk(hd)])

        def fin_c(o, hd=hd):
            oc_ref[0, :, blk(hd // 2)] = jnp.where(lo, o[0], pltpu.roll(o[1], C_V, 1)).astype(BF16)

        units.append((score, lambda hd=hd: vc_ref[0, :, blk(hd)], None, fin_c if hd % 2 == 1 else None))

    pending = []
    nxt = units[0][0]()
    for u, (_, value, sink_head, finish) in enumerate(units):
        s = nxt
        if u + 1 < len(units):
            nxt = units[u + 1][0]()
        sink = None
        if sink_head is not None:
            sink = jnp.broadcast_to(sink_ref[sink_head:sink_head + 1, 0:1], (s.shape[0], 1))
        pending.append(_softmax_pv([(s, value())], sink))
        if finish is not None:
            finish(pending)
            pending = []


def _attn_ctx(parts, sinks, layer):
    b, c, _ = parts[0].shape
    widths = [a.shape[2] for a in parts]

    def bmap(bb):
        return (bb, 0, 0)

    return pl.pallas_call(
        _attn_ctx_kernel,
        grid=(b,),
        in_specs=[pl.BlockSpec((1, c, w), bmap) for w in widths]
        + [pl.BlockSpec((None, A_HEADS, LANES), lambda bb: (layer, 0, 0))],
        out_specs=[pl.BlockSpec((1, c, w), bmap) for w in (512, 256, 256)],
        out_shape=[jax.ShapeDtypeStruct((b, c, w), BF16) for w in (512, 256, 256)],
        compiler_params=_params("arbitrary"),
        name="attn_context",
    )(*parts, sinks)


def _outproj_mlp_kernel(x_ref, ma_ref, mb_ref, mc_ref, mod_ref, woa_ref, wob_ref, woc_ref,
                        g2_ref, w1_ref, w2_ref, gf_ref, o_ref, *, final, ff_chunk):
    mod = mod_ref[0]
    attn = _dot(ma_ref[0], woa_ref[...]) + _dot(mb_ref[0], wob_ref[...]) + _dot(mc_ref[0], woc_ref[...])
    x1 = x_ref[0] + mod[2:3] * attn
    h = (_rms(x1, g2_ref[...]) * (1.0 + mod[4:5]) + mod[3:4]).astype(BF16)
    d_ff = w1_ref.shape[1]
    def up(c0):
        return _dot(h, w1_ref[:, c0:c0 + ff_chunk])

    y = None
    chunk_starts = list(range(0, d_ff, ff_chunk))
    nxt = up(chunk_starts[0])
    for i, c0 in enumerate(chunk_starts):
        u = jnp.maximum(nxt, 0.0)
        if i + 1 < len(chunk_starts):
            nxt = up(chunk_starts[i + 1])
        part = _dot((u * u).astype(BF16), w2_ref[c0:c0 + ff_chunk, :])
        y = part if y is None else y + part
    x2 = x1 + mod[5:6] * y
    if final:
        x2 = _rms(x2, gf_ref[...])
    o_ref[0] = x2


def _outproj_mlp(x, ma, mb, mc, layer, mod_row, mods, woa, wob, woc, g2, w1, w2, gf, tm, final):
    b, s, d = x.shape
    d_ff = w1.shape[2]

    def xmap(bb, j):
        return (bb, j, 0)

    def modmap(bb, j):
        return (layer, (bb if mod_row is None else mod_row), 0, 0)

    def resident(shape):
        return pl.BlockSpec((None,) + shape, lambda bb, j: (layer, 0, 0), pipeline_mode=pl.Buffered(1))

    return pl.pallas_call(
        functools.partial(_outproj_mlp_kernel, final=final, ff_chunk=1024),
        grid=(b, s // tm),
        in_specs=[
            pl.BlockSpec((1, tm, d), xmap),
            pl.BlockSpec((1, tm, 512), xmap),
            pl.BlockSpec((1, tm, 256), xmap),
            pl.BlockSpec((1, tm, 256), xmap),
            pl.BlockSpec((None, 1, N_MOD, d), modmap),
            resident((512, d)), resident((256, d)), resident((256, d)),
            resident((1, d)),
            resident((d, d_ff)), resident((d_ff, d)),
            pl.BlockSpec((1, d), lambda bb, j: (0, 0)),
        ],
        out_specs=pl.BlockSpec((1, tm, d), xmap),
        out_shape=jax.ShapeDtypeStruct((b, s, d), F32),
        compiler_params=_params("arbitrary", "arbitrary"),
        name="outproj_mlp",
    )(x, ma, mb, mc, mods, woa, wob, woc, g2, w1, w2, gf)


def _rope_tables(s):
    tok = jnp.arange(s)
    row, col = (tok // GRID_W).astype(F32), (tok % GRID_W).astype(F32)

    def cs(pos, half):
        freqs = ROPE_BASE ** (-jnp.arange(half, dtype=F32) / half)
        ang = pos[:, None] * freqs
        return jnp.cos(ang), jnp.sin(ang)

    cr, sr = cs(row, 16)
    cc, sc = cs(col, 16)
    z = jnp.zeros_like(sr)
    a_c = jnp.tile(jnp.concatenate([cr, cr, cc, cc], axis=1), (1, 2))
    a_s1 = jnp.tile(jnp.concatenate([-sr, z, -sc, z], axis=1), (1, 2))
    a_s2 = jnp.tile(jnp.concatenate([z, sr, z, sc], axis=1), (1, 2))
    cr, sr = cs(row, 8)
    cc, sc = cs(col, 8)
    z = jnp.zeros_like(sr)
    one64, zero64 = jnp.ones((s, C_NOPE), F32), jnp.zeros((s, C_NOPE), F32)
    one32, zero32 = jnp.ones((s, 32), F32), jnp.zeros((s, 32), F32)
    c_c = jnp.concatenate([one64, cr, cr, cc, cc, one32], axis=1)
    c_s1 = jnp.concatenate([zero64, -sr, z, -sc, z, zero32], axis=1)
    c_s2 = jnp.concatenate([zero64, z, sr, z, sc, zero32], axis=1)
    return (a_c, a_s1, a_s2, c_c, c_s1, c_s2)


def _identity_tables(s):
    one, zero = jnp.ones((s, LANES), F32), jnp.zeros((s, LANES), F32)
    return (one, zero, zero, one, zero, zero)


def _pad_heads(w, offs, width):
    pad = jnp.zeros(w.shape[:2] + (LANES - width,), w.dtype)
    return jnp.concatenate([jnp.concatenate([w[:, :, o:o + width], pad], axis=2) for o in offs], axis=2)


def _weight_layouts(w_in, w_uq, w_ukv, w_out):
    depth, d, _ = w_in.shape
    qa_w = w_in[:, :, :512].reshape(depth, d, A_KV_HEADS, A_GROUP, HEAD_DIM).transpose(0, 1, 3, 2, 4)
    qa_w = qa_w.reshape(depth, d, 512)
    z = lambda n: jnp.zeros((depth, d, n), w_in.dtype)
    kr_w = jnp.concatenate([z(64), w_in[:, :, 1920:1952], z(32)], axis=2)
    win = jnp.concatenate([qa_w, w_in[:, :, 512:1920], kr_w], axis=2).astype(BF16)
    hq, hk = C_NOPE + C_ROPE, C_NOPE + C_V
    wuq = _pad_heads(w_uq, [hq * h for h in range(C_HEADS)], hq).astype(BF16)
    wuk = _pad_heads(w_ukv, [hk * h for h in range(C_HEADS)], C_NOPE).astype(BF16)
    wuv = _pad_heads(w_ukv, [hk * h + C_NOPE for h in range(C_HEADS)], C_V).astype(BF16)
    woa = w_out[:, :512].reshape(depth, A_KV_HEADS, A_GROUP, HEAD_DIM, d).transpose(0, 2, 1, 3, 4)
    woa = woa.reshape(depth, 512, d).astype(BF16)
    wob = w_out[:, 512:768].astype(BF16)
    woc = w_out[:, 768:].astype(BF16)
    return win, wuq, wuk, wuv, woa, wob, woc


def kernel(x, c, ctx, c_ctx, w_ada, b_ada, norm1_g, norm2_g, w_in, attn_sink, na_rpb, mla_q_norm_g,
           mla_w_uq, mla_kv_norm_g, mla_w_ukv, w_out, w_mlp_in, w_mlp_out, final_norm_g):
    b, s, d = x.shape
    n_ctx = ctx.shape[1]
    depth = w_ada.shape[0]
    assert b + 1 <= MOD_ROWS and s % 1024 == 0 and n_ctx % 128 == 0
    tm_x = 512
    tm_c = min(n_ctx, 256)
    tq_a = 256
    tq_c = 512

    cvec = jnp.concatenate([c, c_ctx[None], jnp.zeros((MOD_ROWS - b - 1, d), c.dtype)], axis=0)
    mods = _modulation(cvec, w_ada, b_ada).reshape(depth, MOD_ROWS, N_MOD, d)
    tabs_x = _rope_tables(s)
    tabs_c = _identity_tables(n_ctx)
    gf = final_norm_g.reshape(1, d)
    win, wuq, wuk, wuv, woa, wob, woc = _weight_layouts(w_in, mla_w_uq, mla_w_ukv, w_out)
    w1, w2 = w_mlp_in.astype(BF16), w_mlp_out.astype(BF16)
    sinks = jnp.broadcast_to((attn_sink * LOG2E)[:, :, None], (depth, A_HEADS, LANES)).astype(F32)
    bias = _na_bias_table(na_rpb)
    g1, g2 = norm1_g.reshape(depth, 1, d), norm2_g.reshape(depth, 1, d)
    gq, gkv = mla_q_norm_g.reshape(depth, 1, -1), mla_kv_norm_g.reshape(depth, 1, -1)

    for l in range(depth):
        last = l == depth - 1
        xs = _inproj(x, l, None, mods, g1, win, gq, wuq, gkv, wuk, wuv, tabs_x, 2 * tm_x)
        cs = _inproj(ctx, l, b, mods, g1, win, gq, wuq, gkv, wuk, wuv, tabs_c, tm_c)
        qa, ka, va, qb, kb, vb, qc, kc, vc = xs
        cqa, cka, cva, cqb, ckb, cvb, cqc, ckc, cvc = cs

        oa = _attn_a(qa, ka, va, cka, cva, sinks, l, tq_a, 4 * tq_a)
        ob = _attn_b(qb, kb, vb, ckb, cvb, bias, l)
        oc = _attn_c(qc, kc, vc, ckc, cvc, tq_c)
        x = _outproj_mlp(x, oa, ob, oc, l, None, mods, woa, wob, woc, g2, w1, w2, gf, tm_x, last)
        if not last:
            coa, cob, coc = _attn_ctx(cs, sinks, l)
            ctx = _outproj_mlp(ctx, coa, cob, coc, l, b, mods, woa, wob, woc, g2, w1, w2, gf, tm_c, False)
    return x
```

```python
import functools
import math

import jax
import jax.numpy as jnp
from jax import lax
from jax.experimental import pallas as pl
from jax.experimental.pallas import tpu as pltpu

F32 = jnp.float32
BF16 = jnp.bfloat16

GRID_W = 64
HEAD_DIM = 64
A_HEADS = 8
A_KV_HEADS = 2
A_GROUP = A_HEADS // A_KV_HEADS
A_WINDOW = 128
B_HEADS = 4
NA_ROWS = 8
NA_COLS = 16
C_HEADS = 4
C_Q_RANK = 256
C_KV_RANK = 128
C_NOPE = 64
C_ROPE = 32
C_V = 64
N_MOD = 6
ROPE_BASE = 10000.0
EPS = 1e-6
NEG_INF = -1e30
LOG2E = math.log2(math.e)

LANES = 128
MOD_ROWS = 16
IN_COLS = 2048
VMEM_LIMIT = 56 * 1024 * 1024

_OFF_QA, _OFF_KA, _OFF_VA = 0, 512, 640
_OFF_QB, _OFF_KB, _OFF_VB = 768, 1024, 1280
_OFF_CQ, _OFF_CKV, _OFF_KR = 1536, 1792, 1920


def _dot(a, b):
    return jnp.dot(a, b, preferred_element_type=F32)


def _dot_t(a, b):
    return lax.dot_general(a, b, (((1,), (1,)), ((), ())), preferred_element_type=F32)


def _params(*sem):
    return pltpu.CompilerParams(dimension_semantics=sem, vmem_limit_bytes=VMEM_LIMIT)


def _mod_kernel(c_ref, w_ref, b_ref, o_ref):
    c = c_ref[...]
    s = c * (1.0 / (1.0 + jnp.exp(-c)))
    o_ref[0] = _dot(s.astype(BF16), w_ref[0].astype(BF16)) + b_ref[0]


def _modulation(cvec, w_ada, b_ada):
    depth, d, n = w_ada.shape
    tn = 1024
    return pl.pallas_call(
        _mod_kernel,
        grid=(depth, n // tn),
        in_specs=[
            pl.BlockSpec((MOD_ROWS, d), lambda l, j: (0, 0)),
            pl.BlockSpec((1, d, tn), lambda l, j: (l, 0, j)),
            pl.BlockSpec((1, 1, tn), lambda l, j: (l, 0, j)),
        ],
        out_specs=pl.BlockSpec((1, MOD_ROWS, tn), lambda l, j: (l, 0, j)),
        out_shape=jax.ShapeDtypeStruct((depth, MOD_ROWS, n), F32),
        compiler_params=_params("arbitrary", "arbitrary"),
        name="adaln_modulation",
    )(cvec, w_ada, b_ada.reshape(depth, 1, n))


def _rope(x, c, s1, s2, shift):
    return x * c + pltpu.roll(x, LANES - shift, 1) * s1 + pltpu.roll(x, shift, 1) * s2


def _rms(x, g):
    return x * lax.rsqrt(jnp.mean(x * x, axis=-1, keepdims=True) + EPS) * g


def _inproj_kernel(x_ref, mod_ref, g1_ref, win_ref, gq_ref, wuq_ref, gkv_ref, wuk_ref, wuv_ref,
                   ac_ref, as1_ref, as2_ref, cc_ref, cs1_ref, cs2_ref,
                   qa_ref, ka_ref, va_ref, qb_ref, kb_ref, vb_ref, qc_ref, kc_ref, vc_ref, *, sub):
    tm = x_ref.shape[1]
    mod = mod_ref[0]
    scale_ab = HEAD_DIM ** -0.5 * LOG2E
    scale_c = (C_NOPE + C_ROPE) ** -0.5 * LOG2E
    ones_lane = (lax.broadcasted_iota(jnp.int32, (1, LANES), 1) == C_V).astype(F32)

    subs = [slice(r0, r0 + sub) for r0 in range(0, tm, sub)]
    def up_project(pc):
        cq = _rms(pc[:, 0:C_Q_RANK], gq_ref[...]).astype(BF16)
        ckv = _rms(pc[:, C_Q_RANK:C_Q_RANK + C_KV_RANK], gkv_ref[...]).astype(BF16)
        return _dot(cq, wuq_ref[...]), _dot(ckv, wuk_ref[...]), _dot(ckv, wuv_ref[...])

    pabs, pcs, ups = [], [], []
    for rs in subs:
        h = (_rms(x_ref[0, rs, :], g1_ref[...]) * (1.0 + mod[1:2]) + mod[0:1]).astype(BF16)
        pcs.append(_dot(h, win_ref[:, _OFF_CQ:IN_COLS]))
        pabs.append(_dot(h, win_ref[:, 0:_OFF_CQ]))
        ups.append(up_project(pcs[-1]))

    for rs, p, pc, (q_up, k_up, v_up) in zip(subs, pabs, pcs, ups):
        ac, as1, as2 = ac_ref[rs, :], as1_ref[rs, :], as2_ref[rs, :]
        cc, cs1, cs2 = cc_ref[rs, :], cs1_ref[rs, :], cs2_ref[rs, :]
        for j in range(4):
            blk = p[:, _OFF_QA + LANES * j:_OFF_QA + LANES * (j + 1)]
            qa_ref[0, rs, LANES * j:LANES * (j + 1)] = (_rope(blk, ac, as1, as2, 16) * scale_ab).astype(BF16)
        ka_ref[0, rs, :] = _rope(p[:, _OFF_KA:_OFF_KA + LANES], ac, as1, as2, 16).astype(BF16)
        va = p[:, _OFF_VA:_OFF_VA + LANES]
        lane = lax.broadcasted_iota(jnp.int32, va.shape, 1)
        va_ref[0, rs, 0:LANES] = jnp.where(lane < HEAD_DIM, va, jnp.where(lane == HEAD_DIM, 1.0, 0.0)).astype(BF16)
        va_ref[0, rs, LANES:2 * LANES] = jnp.where(lane >= HEAD_DIM, va, jnp.where(lane == 0, 1.0, 0.0)).astype(BF16)
        qb_ref[0, rs, :] = (p[:, _OFF_QB:_OFF_QB + 256] * scale_ab).astype(BF16)
        kb_ref[0, rs, :] = p[:, _OFF_KB:_OFF_KB + 256].astype(BF16)
        vb_ref[0, rs, :] = p[:, _OFF_VB:_OFF_VB + 256].astype(BF16)
        kr = _rope(pc[:, _OFF_KR - _OFF_CQ:], cc, cs1, cs2, 8)
        for hd in range(C_HEADS):
            sl = slice(LANES * hd, LANES * (hd + 1))
            qc_ref[0, rs, sl] = (_rope(q_up[:, sl], cc, cs1, cs2, 8) * scale_c).astype(BF16)
            kc_ref[0, rs, sl] = (k_up[:, sl] + kr).astype(BF16)
            vc_ref[0, rs, sl] = (v_up[:, sl] + ones_lane).astype(BF16)


def _inproj(x, layer, mod_row, mods, g1, win, gq, wuq, gkv, wuk, wuv, tabs, tm):
    b, s, d = x.shape
    nt = s // tm

    def xmap(j, bb):
        return (bb, j, 0)

    def modmap(j, bb):
        return (layer, (bb if mod_row is None else mod_row), 0, 0)

    def wmap(j, bb):
        return (layer, 0, 0)

    def tabmap(j, bb):
        return (j, 0)

    widths = (512, 128, 256, 256, 256, 256, 512, 512, 512)
    in_specs = [
        pl.BlockSpec((1, tm, d), xmap),
        pl.BlockSpec((None, 1, N_MOD, d), modmap),
        pl.BlockSpec((None, 1, d), wmap),
        pl.BlockSpec((None, d, IN_COLS), wmap),
        pl.BlockSpec((None, 1, C_Q_RANK), wmap),
        pl.BlockSpec((None, C_Q_RANK, 512), wmap),
        pl.BlockSpec((None, 1, C_KV_RANK), wmap),
        pl.BlockSpec((None, C_KV_RANK, 512), wmap),
        pl.BlockSpec((None, C_KV_RANK, 512), wmap),
    ] + [pl.BlockSpec((tm, LANES), tabmap) for _ in range(6)]
    return pl.pallas_call(
        functools.partial(_inproj_kernel, sub=min(tm, 256)),
        grid=(nt, b),
        in_specs=in_specs,
        out_specs=[pl.BlockSpec((1, tm, w), xmap) for w in widths],
        out_shape=[jax.ShapeDtypeStruct((b, s, w), BF16) for w in widths],
        compiler_params=_params("arbitrary", "arbitrary"),
        name="norm_inproj_rope",
    )(x, mods, g1, win, gq, wuq, gkv, wuk, wuv, *tabs)


def _lane_lo(shape):
    return lax.broadcasted_iota(jnp.int32, shape, 1) < HEAD_DIM


def _split_pair(q2):
    lo = _lane_lo(q2.shape)
    zero = jnp.zeros_like(q2)
    return jnp.concatenate([jnp.where(lo, q2, zero), jnp.where(lo, zero, q2)], axis=0)


def _merge_pair(o):
    n = o.shape[0] // 2
    return jnp.where(_lane_lo((n, o.shape[1])), o[:n], o[n:])


def _softmax_pv(parts, sink=None):
    m = parts[0][0].max(axis=-1, keepdims=True)
    for s, _ in parts[1:]:
        m = jnp.maximum(m, s.max(axis=-1, keepdims=True))
    if sink is not None:
        m = jnp.maximum(m, sink)
    denom = None
    acc = None
    for s, v in parts:
        e = jnp.exp2(s - m)
        r = e.sum(axis=-1, keepdims=True)
        denom = r if denom is None else denom + r
        o = _dot(e.astype(BF16), v)
        acc = o if acc is None else acc + o
    if sink is not None:
        denom = denom + jnp.exp2(sink - m)
    return acc / denom


def _attn_a_kernel(q_ref, k_ref, v_ref, kx_ref, vx_ref, sink_ref, o_ref, *, tq, seq, ahead):
    t = pl.program_id(1)
    nsub = q_ref.shape[1] // tq
    nloc = tq + 2 * A_WINDOW
    kx = kx_ref[0]
    lo = _lane_lo((tq, LANES))
    iota_rel = (lax.broadcasted_iota(jnp.int32, (tq, nloc), 1) - lax.broadcasted_iota(jnp.int32, (tq, nloc), 0))
    starts, bands = [], []
    for sb in range(nsub):
        qstart = (t * nsub + sb) * tq
        start = pl.multiple_of(jnp.clip(qstart - A_WINDOW, 0, seq - nloc), A_WINDOW)
        rel = iota_rel + (start - qstart)
        starts.append(start)
        bands.append(jnp.where(jnp.abs(rel) <= A_WINDOW, 0.0, NEG_INF))
    units = [(sb, j, g) for sb in range(nsub) for j in range(A_GROUP) for g in range(A_KV_HEADS)]

    def scores(u):
        sb, j, g = units[u]
        q2 = q_ref[0, sb * tq:(sb + 1) * tq, LANES * j:LANES * (j + 1)]
        zero = jnp.zeros_like(q2)
        q = jnp.where(lo, q2, zero) if g == 0 else jnp.where(lo, zero, q2)
        return _dot_t(q, k_ref[0, pl.ds(starts[sb], nloc), :]) + bands[sb], _dot_t(q, kx)

    outs = {}
    queue = [scores(u) for u in range(min(ahead, len(units)))]
    for u, (sb, j, g) in enumerate(units):
        s_loc, s_ctx = queue.pop(0)
        if u + ahead < len(units):
            queue.append(scores(u + ahead))
        vsg = v_ref[0, pl.ds(starts[sb], nloc), LANES * g:LANES * (g + 1)]
        vxg = vx_ref[0, :, LANES * g:LANES * (g + 1)]
        hd = g * A_GROUP + j
        sink = jnp.broadcast_to(sink_ref[hd:hd + 1, 0:1], (tq, 1))
        m = jnp.maximum(jnp.maximum(s_loc.max(axis=-1, keepdims=True), s_ctx.max(axis=-1, keepdims=True)), sink)
        o = _dot(jnp.exp2(s_loc - m).astype(BF16), vsg) + _dot(jnp.exp2(s_ctx - m).astype(BF16), vxg)
        denom = (o[:, HEAD_DIM:HEAD_DIM + 1] if g == 0 else o[:, 0:1]) + jnp.exp2(sink - m)
        outs[g] = o / denom
        if g == A_KV_HEADS - 1:
            o_ref[0, sb * tq:(sb + 1) * tq, LANES * j:LANES * (j + 1)] = jnp.where(lo, outs[0], outs[1]).astype(BF16)


def _attn_a(qa, ka, va, kax, vax, sinks, layer, tq, tstep):
    b, s, _ = qa.shape
    c = kax.shape[1]

    def qmap(bb, t):
        return (bb, t, 0)

    def kmap(bb, t):
        return (bb, 0, 0)

    return pl.pallas_call(
        functools.partial(_attn_a_kernel, tq=tq, seq=s, ahead=2),
        grid=(b, s // tstep),
        in_specs=[
            pl.BlockSpec((1, tstep, 512), qmap),
            pl.BlockSpec((1, s, LANES), kmap),
            pl.BlockSpec((1, s, 2 * LANES), kmap),
            pl.BlockSpec((1, c, LANES), kmap),
            pl.BlockSpec((1, c, 2 * LANES), kmap),
            pl.BlockSpec((None, A_HEADS, LANES), lambda bb, t: (layer, 0, 0)),
        ],
        out_specs=pl.BlockSpec((1, tstep, 512), qmap),
        out_shape=jax.ShapeDtypeStruct((b, s, 512), BF16),
        compiler_params=_params("arbitrary", "arbitrary"),
        name="attn_window_gqa",
    )(qa, ka, va, kax, vax, sinks)


def _attn_b_kernel(q_ref, k_ref, v_ref, kx_ref, vx_ref, bias_ref, o_ref, *, rows, ahead):
    pair, rb = pl.program_id(1), pl.program_id(2)
    kx, vx = kx_ref[0], vx_ref[0]
    nrow = q_ref.shape[1] // GRID_W

    def scores(r):
        qr = rb * nrow + r
        wr = jnp.clip(qr - NA_ROWS // 2, 0, rows - NA_ROWS)
        oi = wr - qr + (NA_ROWS - 1)
        start = pl.multiple_of(wr * GRID_W, GRID_W)
        q = _split_pair(q_ref[0, r * GRID_W:(r + 1) * GRID_W, :])
        ks = k_ref[0, pl.ds(start, NA_ROWS * GRID_W), :]
        bias = jnp.concatenate([bias_ref[pair, oi], bias_ref[pair, NA_ROWS + oi]], axis=0)
        return _dot_t(q, ks) + bias, _dot_t(q, kx), start

    queue = [scores(r) for r in range(min(ahead, nrow))]
    for r in range(nrow):
        s_loc, s_ctx, start = queue.pop(0)
        if r + ahead < nrow:
            queue.append(scores(r + ahead))
        vs = v_ref[0, pl.ds(start, NA_ROWS * GRID_W), :]
        o = _softmax_pv([(s_loc, vs), (s_ctx, vx)])
        o_ref[0, r * GRID_W:(r + 1) * GRID_W, :] = _merge_pair(o).astype(BF16)


def _attn_b(qb, kb, vb, kbx, vbx, bias, layer):
    b, s, _ = qb.shape
    c = kbx.shape[1]
    rows = s // GRID_W
    tq = min(4 * NA_ROWS * GRID_W, s)

    def qmap(bb, p, rb):
        return (bb, rb, p)

    def kmap(bb, p, rb):
        return (bb, 0, p)

    return pl.pallas_call(
        functools.partial(_attn_b_kernel, rows=rows, ahead=3),
        grid=(b, B_HEADS // 2, s // tq),
        in_specs=[
            pl.BlockSpec((1, tq, LANES), qmap),
            pl.BlockSpec((1, s, LANES), kmap),
            pl.BlockSpec((1, s, LANES), kmap),
            pl.BlockSpec((1, c, LANES), kmap),
            pl.BlockSpec((1, c, LANES), kmap),
            pl.BlockSpec((None, B_HEADS // 2, 2 * NA_ROWS, GRID_W, NA_ROWS * GRID_W),
                         lambda bb, p, rb: (layer, 0, 0, 0, 0)),
        ],
        out_specs=pl.BlockSpec((1, tq, LANES), qmap),
        out_shape=jax.ShapeDtypeStruct((b, s, 256), BF16),
        compiler_params=_params("arbitrary", "arbitrary", "arbitrary"),
        name="attn_neighbourhood",
    )(qb, kb, vb, kbx, vbx, bias)


def _na_bias_table(rpb):
    depth, h = rpb.shape[:2]
    qc = jnp.arange(GRID_W)[:, None]
    kc = jnp.arange(GRID_W)[None, :]
    wc = jnp.clip(qc - NA_COLS // 2, 0, GRID_W - NA_COLS)
    valid = (kc >= wc) & (kc < wc + NA_COLS)
    pad = GRID_W - NA_COLS
    rp = jnp.pad(rpb, ((0, 0), (0, 0), (0, 0), (pad, pad)))
    toe = jnp.stack([rp[..., GRID_W - 1 - q:2 * GRID_W - 1 - q] for q in range(GRID_W)], axis=3)
    toe = jnp.where(valid, toe * LOG2E, NEG_INF)
    t = jnp.stack([toe[:, :, o:o + NA_ROWS] for o in range(NA_ROWS)], axis=2)
    t = t.transpose(0, 1, 2, 4, 3, 5).reshape(depth, h // 2, 2 * NA_ROWS, GRID_W, NA_ROWS * GRID_W)
    return t.astype(F32)


def _attn_c_kernel(q_ref, k_ref, kx_ref, v_ref, vx_ref, o_ref, *, tq, kc, ahead):
    s = k_ref.shape[1]
    nsub = q_ref.shape[1] // tq
    chunks = [(k_ref, v_ref, c0, kc) for c0 in range(0, s, kc)] + [(kx_ref, vx_ref, 0, kx_ref.shape[1])]
    units = [(sb, ci) for sb in range(nsub) for ci in range(len(chunks))]

    def scores(u):
        sb, ci = units[u]
        kr, _, c0, n = chunks[ci]
        return [_dot_t(q_ref[0, sb * tq:(sb + 1) * tq, LANES * hh:LANES * (hh + 1)],
                       kr[0, c0:c0 + n, LANES * hh:LANES * (hh + 1)]) for hh in range(2)]

    m = [None, None]
    acc = [None, None]
    queue = [scores(u) for u in range(min(ahead, len(units)))]
    for u, (sb, ci) in enumerate(units):
        cur = queue.pop(0)
        if u + ahead < len(units):
            queue.append(scores(u + ahead))
        _, vr, c0, n = chunks[ci]
        for hh in range(2):
            sl = slice(LANES * hh, LANES * (hh + 1))
            sc = cur[hh]
            cm = sc.max(axis=-1, keepdims=True)
            if ci == 0:
                m[hh] = cm
                acc[hh] = _dot(jnp.exp2(sc - cm).astype(BF16), vr[0, c0:c0 + n, sl])
            else:
                m_new = jnp.maximum(m[hh], cm)
                acc[hh] = (jnp.exp2(m[hh] - m_new) * acc[hh]
                           + _dot(jnp.exp2(sc - m_new).astype(BF16), vr[0, c0:c0 + n, sl]))
                m[hh] = m_new
        if ci == len(chunks) - 1:
            o0 = acc[0] / acc[0][:, C_V:C_V + 1]
            o1 = acc[1] / acc[1][:, C_V:C_V + 1]
            o_ref[0, sb * tq:(sb + 1) * tq, :] = jnp.where(
                _lane_lo(o0.shape), o0, pltpu.roll(o1, C_V, 1)).astype(BF16)


def _attn_c(qc, kc, vc, kcx, vcx, tq, tstep):
    b, s, _ = qc.shape
    c = kcx.shape[1]

    def qmap(bb, p, i):
        return (bb, i, p)

    def kmap(bb, p, i):
        return (bb, 0, p)

    return pl.pallas_call(
        functools.partial(_attn_c_kernel, tq=tq, kc=512, ahead=1),
        grid=(b, C_HEADS // 2, s // tstep),
        in_specs=[
            pl.BlockSpec((1, tstep, 2 * LANES), qmap),
            pl.BlockSpec((1, s, 2 * LANES), kmap),
            pl.BlockSpec((1, c, 2 * LANES), kmap),
            pl.BlockSpec((1, s, 2 * LANES), kmap),
            pl.BlockSpec((1, c, 2 * LANES), kmap),
        ],
        out_specs=pl.BlockSpec((1, tstep, LANES), qmap),
        out_shape=jax.ShapeDtypeStruct((b, s, 256), BF16),
        compiler_params=_params("arbitrary", "arbitrary", "arbitrary"),
        name="attn_latent",
    )(qc, kc, kcx, vc, vcx)


def _attn_ctx_kernel(qa_ref, ka_ref, va_ref, qb_ref, kb_ref, vb_ref, qc_ref, kc_ref, vc_ref,
                     sink_ref, oa_ref, ob_ref, oc_ref):
    c = qa_ref.shape[1]
    half = A_HEADS // 2
    lo = _lane_lo((c, LANES))

    def blk(i):
        return slice(LANES * i, LANES * (i + 1))

    units = []
    for j in range(half):
        for g in range(A_KV_HEADS):
            def score(j=j, g=g):
                q2 = qa_ref[0, :, blk(j)]
                zero = jnp.zeros_like(q2)
                return _dot_t(jnp.where(lo, q2, zero) if g == 0 else jnp.where(lo, zero, q2), ka_ref[0])

            def fin_a(o, j=j):
                oa_ref[0, :, blk(j)] = jnp.where(lo, o[0], o[1]).astype(BF16)

            units.append((score, lambda g=g: va_ref[0, :, blk(g)], j + half * g, fin_a if g == 1 else None))
    for p in range(B_HEADS // 2):
        def score(p=p):
            return _dot_t(_split_pair(qb_ref[0, :, blk(p)]), kb_ref[0, :, blk(p)])

        def fin_b(o, p=p):
            ob_ref[0, :, blk(p)] = _merge_pair(o[0]).astype(BF16)

        units.append((score, lambda p=p: vb_ref[0, :, blk(p)], None, fin_b))
    for hd in range(C_HEADS):
        def score(hd=hd):
            return _dot_t(qc_ref[0, :, blk(hd)], kc_ref[0, :, blk(hd)])

        def fin_c(o, hd=hd):
            oc_ref[0, :, blk(hd // 2)] = jnp.where(lo, o[0], pltpu.roll(o[1], C_V, 1)).astype(BF16)

        units.append((score, lambda hd=hd: vc_ref[0, :, blk(hd)], None, fin_c if hd % 2 == 1 else None))

    pending = []
    nxt = units[0][0]()
    for u, (_, value, sink_head, finish) in enumerate(units):
        s = nxt
        if u + 1 < len(units):
            nxt = units[u + 1][0]()
        sink = None
        if sink_head is not None:
            sink = jnp.broadcast_to(sink_ref[sink_head:sink_head + 1, 0:1], (s.shape[0], 1))
        pending.append(_softmax_pv([(s, value())], sink))
        if finish is not None:
            finish(pending)
            pending = []


def _attn_ctx(parts, sinks, layer):
    b, c, _ = parts[0].shape
    widths = [a.shape[2] for a in parts]

    def bmap(bb):
        return (bb, 0, 0)

    return pl.pallas_call(
        _attn_ctx_kernel,
        grid=(b,),
        in_specs=[pl.BlockSpec((1, c, w), bmap) for w in widths]
        + [pl.BlockSpec((None, A_HEADS, LANES), lambda bb: (layer, 0, 0))],
        out_specs=[pl.BlockSpec((1, c, w), bmap) for w in (512, 256, 256)],
        out_shape=[jax.ShapeDtypeStruct((b, c, w), BF16) for w in (512, 256, 256)],
        compiler_params=_params("arbitrary"),
        name="attn_context",
    )(*parts, sinks)


def _outproj_mlp_kernel(x_ref, ma_ref, mb_ref, mc_ref, mod_ref, woa_ref, wob_ref, woc_ref,
                        g2_ref, w1_ref, w2_ref, gf_ref, o_ref, *, final, ff_chunk):
    mod = mod_ref[0]
    attn = _dot(ma_ref[0], woa_ref[...]) + _dot(mb_ref[0], wob_ref[...]) + _dot(mc_ref[0], woc_ref[...])
    x1 = x_ref[0] + mod[2:3] * attn
    h = (_rms(x1, g2_ref[...]) * (1.0 + mod[4:5]) + mod[3:4]).astype(BF16)
    d_ff = w1_ref.shape[1]
    def up(c0):
        return _dot(h, w1_ref[:, c0:c0 + ff_chunk])

    y = None
    chunk_starts = list(range(0, d_ff, ff_chunk))
    nxt = up(chunk_starts[0])
    for i, c0 in enumerate(chunk_starts):
        u = jnp.maximum(nxt, 0.0)
        if i + 1 < len(chunk_starts):
            nxt = up(chunk_starts[i + 1])
        part = _dot((u * u).astype(BF16), w2_ref[c0:c0 + ff_chunk, :])
        y = part if y is None else y + part
    x2 = x1 + mod[5:6] * y
    if final:
        x2 = _rms(x2, gf_ref[...])
    o_ref[0] = x2


def _outproj_mlp(x, ma, mb, mc, layer, mod_row, mods, woa, wob, woc, g2, w1, w2, gf, tm, final):
    b, s, d = x.shape
    d_ff = w1.shape[2]

    def xmap(bb, j):
        return (bb, j, 0)

    def modmap(bb, j):
        return (layer, (bb if mod_row is None else mod_row), 0, 0)

    def resident(shape):
        return pl.BlockSpec((None,) + shape, lambda bb, j: (layer, 0, 0), pipeline_mode=pl.Buffered(1))

    return pl.pallas_call(
        functools.partial(_outproj_mlp_kernel, final=final, ff_chunk=1024),
        grid=(b, s // tm),
        in_specs=[
            pl.BlockSpec((1, tm, d), xmap),
            pl.BlockSpec((1, tm, 512), xmap),
            pl.BlockSpec((1, tm, 256), xmap),
            pl.BlockSpec((1, tm, 256), xmap),
            pl.BlockSpec((None, 1, N_MOD, d), modmap),
            resident((512, d)), resident((256, d)), resident((256, d)),
            resident((1, d)),
            resident((d, d_ff)), resident((d_ff, d)),
            pl.BlockSpec((1, d), lambda bb, j: (0, 0)),
        ],
        out_specs=pl.BlockSpec((1, tm, d), xmap),
        out_shape=jax.ShapeDtypeStruct((b, s, d), F32),
        compiler_params=_params("arbitrary", "arbitrary"),
        name="outproj_mlp",
    )(x, ma, mb, mc, mods, woa, wob, woc, g2, w1, w2, gf)


def _rope_tables(s):
    tok = jnp.arange(s)
    row, col = (tok // GRID_W).astype(F32), (tok % GRID_W).astype(F32)

    def cs(pos, half):
        freqs = ROPE_BASE ** (-jnp.arange(half, dtype=F32) / half)
        ang = pos[:, None] * freqs
        return jnp.cos(ang), jnp.sin(ang)

    cr, sr = cs(row, 16)
    cc, sc = cs(col, 16)
    z = jnp.zeros_like(sr)
    a_c = jnp.tile(jnp.concatenate([cr, cr, cc, cc], axis=1), (1, 2))
    a_s1 = jnp.tile(jnp.concatenate([-sr, z, -sc, z], axis=1), (1, 2))
    a_s2 = jnp.tile(jnp.concatenate([z, sr, z, sc], axis=1), (1, 2))
    cr, sr = cs(row, 8)
    cc, sc = cs(col, 8)
    z = jnp.zeros_like(sr)
    one64, zero64 = jnp.ones((s, C_NOPE), F32), jnp.zeros((s, C_NOPE), F32)
    one32, zero32 = jnp.ones((s, 32), F32), jnp.zeros((s, 32), F32)
    c_c = jnp.concatenate([one64, cr, cr, cc, cc, one32], axis=1)
    c_s1 = jnp.concatenate([zero64, -sr, z, -sc, z, zero32], axis=1)
    c_s2 = jnp.concatenate([zero64, z, sr, z, sc, zero32], axis=1)
    return (a_c, a_s1, a_s2, c_c, c_s1, c_s2)


def _identity_tables(s):
    one, zero = jnp.ones((s, LANES), F32), jnp.zeros((s, LANES), F32)
    return (one, zero, zero, one, zero, zero)


def _pad_heads(w, offs, width):
    pad = jnp.zeros(w.shape[:2] + (LANES - width,), w.dtype)
    return jnp.concatenate([jnp.concatenate([w[:, :, o:o + width], pad], axis=2) for o in offs], axis=2)


def _weight_layouts(w_in, w_uq, w_ukv, w_out):
    depth, d, _ = w_in.shape
    qa_w = w_in[:, :, :512].reshape(depth, d, A_KV_HEADS, A_GROUP, HEAD_DIM).transpose(0, 1, 3, 2, 4)
    qa_w = qa_w.reshape(depth, d, 512)
    z = lambda n: jnp.zeros((depth, d, n), w_in.dtype)
    kr_w = jnp.concatenate([z(64), w_in[:, :, 1920:1952], z(32)], axis=2)
    win = jnp.concatenate([qa_w, w_in[:, :, 512:1920], kr_w], axis=2).astype(BF16)
    hq, hk = C_NOPE + C_ROPE, C_NOPE + C_V
    wuq = _pad_heads(w_uq, [hq * h for h in range(C_HEADS)], hq).astype(BF16)
    wuk = _pad_heads(w_ukv, [hk * h for h in range(C_HEADS)], C_NOPE).astype(BF16)
    wuv = _pad_heads(w_ukv, [hk * h + C_NOPE for h in range(C_HEADS)], C_V).astype(BF16)
    woa = w_out[:, :512].reshape(depth, A_KV_HEADS, A_GROUP, HEAD_DIM, d).transpose(0, 2, 1, 3, 4)
    woa = woa.reshape(depth, 512, d).astype(BF16)
    wob = w_out[:, 512:768].astype(BF16)
    woc = w_out[:, 768:].astype(BF16)
    return win, wuq, wuk, wuv, woa, wob, woc


def kernel(x, c, ctx, c_ctx, w_ada, b_ada, norm1_g, norm2_g, w_in, attn_sink, na_rpb, mla_q_norm_g,
           mla_w_uq, mla_kv_norm_g, mla_w_ukv, w_out, w_mlp_in, w_mlp_out, final_norm_g):
    b, s, d = x.shape
    n_ctx = ctx.shape[1]
    depth = w_ada.shape[0]
    assert b + 1 <= MOD_ROWS and s % 1024 == 0 and n_ctx % 128 == 0
    tm_x = 512
    tm_c = min(n_ctx, 256)
    tq_a = 256
    tq_c = 512

    cvec = jnp.concatenate([c, c_ctx[None], jnp.zeros((MOD_ROWS - b - 1, d), c.dtype)], axis=0)
    mods = _modulation(cvec, w_ada, b_ada).reshape(depth, MOD_ROWS, N_MOD, d)
    tabs_x = _rope_tables(s)
    tabs_c = _identity_tables(n_ctx)
    gf = final_norm_g.reshape(1, d)
    win, wuq, wuk, wuv, woa, wob, woc = _weight_layouts(w_in, mla_w_uq, mla_w_ukv, w_out)
    w1, w2 = w_mlp_in.astype(BF16), w_mlp_out.astype(BF16)
    sinks = jnp.broadcast_to((attn_sink * LOG2E)[:, :, None], (depth, A_HEADS, LANES)).astype(F32)
    bias = _na_bias_table(na_rpb)
    g1, g2 = norm1_g.reshape(depth, 1, d), norm2_g.reshape(depth, 1, d)
    gq, gkv = mla_q_norm_g.reshape(depth, 1, -1), mla_kv_norm_g.reshape(depth, 1, -1)

    for l in range(depth):
        last = l == depth - 1
        xs = _inproj(x, l, None, mods, g1, win, gq, wuq, gkv, wuk, wuv, tabs_x, 2 * tm_x)
        cs = _inproj(ctx, l, b, mods, g1, win, gq, wuq, gkv, wuk, wuv, tabs_c, tm_c)
        qa, ka, va, qb, kb, vb, qc, kc, vc = xs
        cqa, cka, cva, cqb, ckb, cvb, cqc, ckc, cvc = cs

        oa = _attn_a(qa, ka, va, cka, cva, sinks, l, tq_a, 4 * tq_a)
        ob = _attn_b(qb, kb, vb, ckb, cvb, bias, l)
        oc = _attn_c(qc, kc, vc, ckc, cvc, tq_c, 2 * tq_c)
        x = _outproj_mlp(x, oa, ob, oc, l, None, mods, woa, wob, woc, g2, w1, w2, gf, tm_x, last)
        if not last:
            coa, cob, coc = _attn_ctx(cs, sinks, l)
            ctx = _outproj_mlp(ctx, coa, cob, coc, l, b, mods, woa, wob, woc, g2, w1, w2, gf, tm_c, False)
    return x
```

```python
import functools
import math

import jax
import jax.numpy as jnp
import numpy as np
from jax import lax
from jax.experimental import pallas as pl
from jax.experimental.pallas import tpu as pltpu

F32 = jnp.float32
BF16 = jnp.bfloat16

GRID_W = 64
HEAD_DIM = 64
A_HEADS = 8
A_KV_HEADS = 2
A_GROUP = A_HEADS // A_KV_HEADS
A_WINDOW = 128
B_HEADS = 4
NA_ROWS = 8
NA_COLS = 16
C_HEADS = 4
C_Q_RANK = 256
C_KV_RANK = 128
C_NOPE = 64
C_ROPE = 32
C_V = 64
N_MOD = 6
ROPE_BASE = 10000.0
EPS = 1e-6
NEG_INF = -1e30
LOG2E = math.log2(math.e)

LANES = 128
MOD_ROWS = 16
IN_COLS = 2048
VMEM_LIMIT = 56 * 1024 * 1024

_OFF_QA, _OFF_KA, _OFF_VA = 0, 512, 640
_OFF_QB, _OFF_KB, _OFF_VB = 768, 1024, 1280
_OFF_CQ, _OFF_CKV, _OFF_KR = 1536, 1792, 1920


def _dot(a, b):
    return jnp.dot(a, b, preferred_element_type=F32)


def _dot_t(a, b):
    return lax.dot_general(a, b, (((1,), (1,)), ((), ())), preferred_element_type=F32)


def _params(*sem):
    return pltpu.CompilerParams(dimension_semantics=sem, vmem_limit_bytes=VMEM_LIMIT)


def _mod_kernel(c_ref, w_ref, b_ref, o_ref):
    c = c_ref[...]
    s = c * (1.0 / (1.0 + jnp.exp(-c)))
    o_ref[0] = _dot(s.astype(BF16), w_ref[0].astype(BF16)) + b_ref[0]


def _modulation(cvec, w_ada, b_ada):
    depth, d, n = w_ada.shape
    tn = 1024
    return pl.pallas_call(
        _mod_kernel,
        grid=(depth, n // tn),
        in_specs=[
            pl.BlockSpec((MOD_ROWS, d), lambda l, j: (0, 0)),
            pl.BlockSpec((1, d, tn), lambda l, j: (l, 0, j)),
            pl.BlockSpec((1, 1, tn), lambda l, j: (l, 0, j)),
        ],
        out_specs=pl.BlockSpec((1, MOD_ROWS, tn), lambda l, j: (l, 0, j)),
        out_shape=jax.ShapeDtypeStruct((depth, MOD_ROWS, n), F32),
        compiler_params=_params("arbitrary", "arbitrary"),
        name="adaln_modulation",
    )(cvec, w_ada, b_ada.reshape(depth, 1, n))


def _rope(x, c, s1, s2, shift):
    return x * c + pltpu.roll(x, LANES - shift, 1) * s1 + pltpu.roll(x, shift, 1) * s2


def _rms(x, g):
    return x * lax.rsqrt(jnp.mean(x * x, axis=-1, keepdims=True) + EPS) * g


def _inproj_kernel(x_ref, mod_ref, g1_ref, win_ref, gq_ref, wuq_ref, gkv_ref, wukv_ref,
                   ac_ref, as1_ref, as2_ref, cc_ref, cs1_ref, cs2_ref,
                   qa_ref, ka_ref, va_ref, qb_ref, kb_ref, vb_ref, qc_ref, kc_ref, vc_ref, *, sub):
    tm = x_ref.shape[1]
    mod = mod_ref[0]
    scale_ab = HEAD_DIM ** -0.5 * LOG2E
    scale_c = (C_NOPE + C_ROPE) ** -0.5 * LOG2E
    ones_lane = (lax.broadcasted_iota(jnp.int32, (1, LANES), 1) == C_V).astype(F32)

    subs = [slice(r0, r0 + sub) for r0 in range(0, tm, sub)]

    def up_project(pc):
        cq = _rms(pc[:, 0:C_Q_RANK], gq_ref[...]).astype(BF16)
        ckv = _rms(pc[:, C_Q_RANK:C_Q_RANK + C_KV_RANK], gkv_ref[...]).astype(BF16)
        return _dot(cq, wuq_ref[...]), _dot(ckv, wukv_ref[...])

    pabs, pcs, ups = [], [], []
    for rs in subs:
        h = (_rms(x_ref[0, rs, :], g1_ref[...]) * (1.0 + mod[1:2]) + mod[0:1]).astype(BF16)
        pcs.append(_dot(h, win_ref[:, _OFF_CQ:IN_COLS]))
        pabs.append(_dot(h, win_ref[:, 0:_OFF_CQ]))
        ups.append(up_project(pcs[-1]))

    for rs, p, pc, (q_up, kv_up) in zip(subs, pabs, pcs, ups):
        ac, as1, as2 = ac_ref[rs, :], as1_ref[rs, :], as2_ref[rs, :]
        cc, cs1, cs2 = cc_ref[rs, :], cs1_ref[rs, :], cs2_ref[rs, :]
        for j in range(4):
            blk = p[:, _OFF_QA + LANES * j:_OFF_QA + LANES * (j + 1)]
            qa_ref[0, rs, LANES * j:LANES * (j + 1)] = (_rope(blk, ac, as1, as2, 16) * scale_ab).astype(BF16)
        ka_ref[0, rs, :] = _rope(p[:, _OFF_KA:_OFF_KA + LANES], ac, as1, as2, 16).astype(BF16)
        va = p[:, _OFF_VA:_OFF_VA + LANES]
        lane = lax.broadcasted_iota(jnp.int32, va.shape, 1)
        va_ref[0, rs, 0:LANES] = jnp.where(lane < HEAD_DIM, va, jnp.where(lane == HEAD_DIM, 1.0, 0.0)).astype(BF16)
        va_ref[0, rs, LANES:2 * LANES] = jnp.where(lane >= HEAD_DIM, va, jnp.where(lane == 0, 1.0, 0.0)).astype(BF16)
        qb_ref[0, rs, :] = (p[:, _OFF_QB:_OFF_QB + 256] * scale_ab).astype(BF16)
        kb_ref[0, rs, :] = p[:, _OFF_KB:_OFF_KB + 256].astype(BF16)
        vb_ref[0, rs, :] = p[:, _OFF_VB:_OFF_VB + 256].astype(BF16)
        kr = _rope(pc[:, _OFF_KR - _OFF_CQ:], cc, cs1, cs2, 8)
        lo = lane < C_NOPE
        for hd in range(C_HEADS):
            sl = slice(LANES * hd, LANES * (hd + 1))
            qc_ref[0, rs, sl] = (_rope(q_up[:, sl], cc, cs1, cs2, 8) * scale_c).astype(BF16)
            kv = kv_up[:, sl]
            kc_ref[0, rs, sl] = jnp.where(lo, kv, kr).astype(BF16)
            vc_ref[0, rs, sl] = jnp.where(lo, pltpu.roll(kv, C_V, 1), ones_lane).astype(BF16)


def _inproj(x, layer, mod_row, mods, g1, win, gq, wuq, gkv, wukv, tabs, tm):
    b, s, d = x.shape
    nt = s // tm

    def xmap(j, bb):
        return (bb, j, 0)

    def modmap(j, bb):
        return (layer, (bb if mod_row is None else mod_row), 0, 0)

    def wmap(j, bb):
        return (layer, 0, 0)

    def tabmap(j, bb):
        return (j, 0)

    widths = (512, 128, 256, 256, 256, 256, 512, 512, 512)
    in_specs = [
        pl.BlockSpec((1, tm, d), xmap),
        pl.BlockSpec((None, 1, N_MOD, d), modmap),
        pl.BlockSpec((None, 1, d), wmap),
        pl.BlockSpec((None, d, IN_COLS), wmap),
        pl.BlockSpec((None, 1, C_Q_RANK), wmap),
        pl.BlockSpec((None, C_Q_RANK, 512), wmap),
        pl.BlockSpec((None, 1, C_KV_RANK), wmap),
        pl.BlockSpec((None, C_KV_RANK, 512), wmap),
    ] + [pl.BlockSpec((tm, LANES), tabmap) for _ in range(6)]
    return pl.pallas_call(
        functools.partial(_inproj_kernel, sub=min(tm, 256)),
        grid=(nt, b),
        in_specs=in_specs,
        out_specs=[pl.BlockSpec((1, tm, w), xmap) for w in widths],
        out_shape=[jax.ShapeDtypeStruct((b, s, w), BF16) for w in widths],
        compiler_params=_params("arbitrary", "arbitrary"),
        name="norm_inproj_rope",
    )(x, mods, g1, win, gq, wuq, gkv, wukv, *tabs)


def _lane_lo(shape):
    return lax.broadcasted_iota(jnp.int32, shape, 1) < HEAD_DIM


def _split_pair(q2):
    lo = _lane_lo(q2.shape)
    zero = jnp.zeros_like(q2)
    return jnp.concatenate([jnp.where(lo, q2, zero), jnp.where(lo, zero, q2)], axis=0)


def _merge_pair(o):
    n = o.shape[0] // 2
    return jnp.where(_lane_lo((n, o.shape[1])), o[:n], o[n:])


def _softmax_pv(parts, sink=None):
    m = parts[0][0].max(axis=-1, keepdims=True)
    for s, _ in parts[1:]:
        m = jnp.maximum(m, s.max(axis=-1, keepdims=True))
    if sink is not None:
        m = jnp.maximum(m, sink)
    denom = None
    acc = None
    for s, v in parts:
        e = jnp.exp2(s - m)
        r = e.sum(axis=-1, keepdims=True)
        denom = r if denom is None else denom + r
        o = _dot(e.astype(BF16), v)
        acc = o if acc is None else acc + o
    if sink is not None:
        denom = denom + jnp.exp2(sink - m)
    return acc / denom


def _attn_a_kernel(q_ref, k_ref, v_ref, kx_ref, vx_ref, sink_ref, o_ref, *, tq, seq, ahead):
    t = pl.program_id(1)
    nsub = q_ref.shape[1] // tq
    nloc = tq + 2 * A_WINDOW
    kx = kx_ref[0]
    lo = _lane_lo((tq, LANES))
    iota_rel = (lax.broadcasted_iota(jnp.int32, (tq, nloc), 1) - lax.broadcasted_iota(jnp.int32, (tq, nloc), 0))
    starts, bands = [], []
    for sb in range(nsub):
        qstart = (t * nsub + sb) * tq
        start = pl.multiple_of(jnp.clip(qstart - A_WINDOW, 0, seq - nloc), A_WINDOW)
        rel = iota_rel + (start - qstart)
        starts.append(start)
        bands.append(jnp.where(jnp.abs(rel) <= A_WINDOW, 0.0, NEG_INF))
    units = [(sb, j, g) for sb in range(nsub) for j in range(A_GROUP) for g in range(A_KV_HEADS)]

    def scores(u):
        sb, j, g = units[u]
        q2 = q_ref[0, sb * tq:(sb + 1) * tq, LANES * j:LANES * (j + 1)]
        zero = jnp.zeros_like(q2)
        q = jnp.where(lo, q2, zero) if g == 0 else jnp.where(lo, zero, q2)
        return _dot_t(q, k_ref[0, pl.ds(starts[sb], nloc), :]) + bands[sb], _dot_t(q, kx)

    outs = {}
    queue = [scores(u) for u in range(min(ahead, len(units)))]
    for u, (sb, j, g) in enumerate(units):
        s_loc, s_ctx = queue.pop(0)
        if u + ahead < len(units):
            queue.append(scores(u + ahead))
        vsg = v_ref[0, pl.ds(starts[sb], nloc), LANES * g:LANES * (g + 1)]
        vxg = vx_ref[0, :, LANES * g:LANES * (g + 1)]
        hd = g * A_GROUP + j
        sink = jnp.broadcast_to(sink_ref[hd:hd + 1, 0:1], (tq, 1))
        m = jnp.maximum(jnp.maximum(s_loc.max(axis=-1, keepdims=True), s_ctx.max(axis=-1, keepdims=True)), sink)
        o = _dot(jnp.exp2(s_loc - m).astype(BF16), vsg) + _dot(jnp.exp2(s_ctx - m).astype(BF16), vxg)
        denom = (o[:, HEAD_DIM:HEAD_DIM + 1] if g == 0 else o[:, 0:1]) + jnp.exp2(sink - m)
        outs[g] = o / denom
        if g == A_KV_HEADS - 1:
            o_ref[0, sb * tq:(sb + 1) * tq, LANES * j:LANES * (j + 1)] = jnp.where(lo, outs[0], outs[1]).astype(BF16)


def _attn_a(qa, ka, va, kax, vax, sinks, layer, tq, tstep):
    b, s, _ = qa.shape
    c = kax.shape[1]

    def qmap(bb, t):
        return (bb, t, 0)

    def kmap(bb, t):
        return (bb, 0, 0)

    return pl.pallas_call(
        functools.partial(_attn_a_kernel, tq=tq, seq=s, ahead=2),
        grid=(b, s // tstep),
        in_specs=[
            pl.BlockSpec((1, tstep, 512), qmap),
            pl.BlockSpec((1, s, LANES), kmap),
            pl.BlockSpec((1, s, 2 * LANES), kmap),
            pl.BlockSpec((1, c, LANES), kmap),
            pl.BlockSpec((1, c, 2 * LANES), kmap),
            pl.BlockSpec((None, A_HEADS, LANES), lambda bb, t: (layer, 0, 0)),
        ],
        out_specs=pl.BlockSpec((1, tstep, 512), qmap),
        out_shape=jax.ShapeDtypeStruct((b, s, 512), BF16),
        compiler_params=_params("arbitrary", "arbitrary"),
        name="attn_window_gqa",
    )(qa, ka, va, kax, vax, sinks)


def _attn_b_kernel(q_ref, k_ref, v_ref, kx_ref, vx_ref, bias_ref, o_ref, *, rows, ahead):
    pair, rb = pl.program_id(1), pl.program_id(2)
    kx, vx = kx_ref[0], vx_ref[0]
    nrow = q_ref.shape[1] // GRID_W

    def scores(r):
        qr = rb * nrow + r
        wr = jnp.clip(qr - NA_ROWS // 2, 0, rows - NA_ROWS)
        oi = wr - qr + (NA_ROWS - 1)
        start = pl.multiple_of(wr * GRID_W, GRID_W)
        q = _split_pair(q_ref[0, r * GRID_W:(r + 1) * GRID_W, :])
        ks = k_ref[0, pl.ds(start, NA_ROWS * GRID_W), :]
        bias = jnp.concatenate([bias_ref[pair, oi], bias_ref[pair, NA_ROWS + oi]], axis=0)
        return _dot_t(q, ks) + bias, _dot_t(q, kx), start

    queue = [scores(r) for r in range(min(ahead, nrow))]
    for r in range(nrow):
        s_loc, s_ctx, start = queue.pop(0)
        if r + ahead < nrow:
            queue.append(scores(r + ahead))
        vs = v_ref[0, pl.ds(start, NA_ROWS * GRID_W), :]
        o = _softmax_pv([(s_loc, vs), (s_ctx, vx)])
        o_ref[0, r * GRID_W:(r + 1) * GRID_W, :] = _merge_pair(o).astype(BF16)


def _attn_b(qb, kb, vb, kbx, vbx, bias, layer):
    b, s, _ = qb.shape
    c = kbx.shape[1]
    rows = s // GRID_W
    tq = min(4 * NA_ROWS * GRID_W, s)

    def qmap(bb, p, rb):
        return (bb, rb, p)

    def kmap(bb, p, rb):
        return (bb, 0, p)

    return pl.pallas_call(
        functools.partial(_attn_b_kernel, rows=rows, ahead=3),
        grid=(b, B_HEADS // 2, s // tq),
        in_specs=[
            pl.BlockSpec((1, tq, LANES), qmap),
            pl.BlockSpec((1, s, LANES), kmap),
            pl.BlockSpec((1, s, LANES), kmap),
            pl.BlockSpec((1, c, LANES), kmap),
            pl.BlockSpec((1, c, LANES), kmap),
            pl.BlockSpec((None, B_HEADS // 2, 2 * NA_ROWS, GRID_W, NA_ROWS * GRID_W),
                         lambda bb, p, rb: (layer, 0, 0, 0, 0)),
        ],
        out_specs=pl.BlockSpec((1, tq, LANES), qmap),
        out_shape=jax.ShapeDtypeStruct((b, s, 256), BF16),
        compiler_params=_params("arbitrary", "arbitrary", "arbitrary"),
        name="attn_neighbourhood",
    )(qb, kb, vb, kbx, vbx, bias)


def _na_bias_table(rpb):
    depth, h = rpb.shape[:2]
    qc = jnp.arange(GRID_W)[:, None]
    kc = jnp.arange(GRID_W)[None, :]
    wc = jnp.clip(qc - NA_COLS // 2, 0, GRID_W - NA_COLS)
    valid = (kc >= wc) & (kc < wc + NA_COLS)
    pad = GRID_W - NA_COLS
    rp = jnp.pad(rpb, ((0, 0), (0, 0), (0, 0), (pad, pad)))
    toe = jnp.stack([rp[..., GRID_W - 1 - q:2 * GRID_W - 1 - q] for q in range(GRID_W)], axis=3)
    toe = jnp.where(valid, toe * LOG2E, NEG_INF)
    t = jnp.stack([toe[:, :, o:o + NA_ROWS] for o in range(NA_ROWS)], axis=2)
    t = t.transpose(0, 1, 2, 4, 3, 5).reshape(depth, h // 2, 2 * NA_ROWS, GRID_W, NA_ROWS * GRID_W)
    return t.astype(F32)


def _attn_c_kernel(q_ref, k_ref, kx_ref, v_ref, vx_ref, o_ref, *, tq, kc, ahead):
    s = k_ref.shape[1]
    nsub = q_ref.shape[1] // tq
    chunks = [(k_ref, v_ref, c0, kc) for c0 in range(0, s, kc)] + [(kx_ref, vx_ref, 0, kx_ref.shape[1])]
    units = [(sb, ci) for sb in range(nsub) for ci in range(len(chunks))]

    def scores(u):
        sb, ci = units[u]
        kr, _, c0, n = chunks[ci]
        return [_dot_t(q_ref[0, sb * tq:(sb + 1) * tq, LANES * hh:LANES * (hh + 1)],
                       kr[0, c0:c0 + n, LANES * hh:LANES * (hh + 1)]) for hh in range(2)]

    m = [None, None]
    acc = [None, None]
    queue = [scores(u) for u in range(min(ahead, len(units)))]
    for u, (sb, ci) in enumerate(units):
        cur = queue.pop(0)
        if u + ahead < len(units):
            queue.append(scores(u + ahead))
        _, vr, c0, n = chunks[ci]
        for hh in range(2):
            sl = slice(LANES * hh, LANES * (hh + 1))
            sc = cur[hh]
            cm = sc.max(axis=-1, keepdims=True)
            if ci == 0:
                m[hh] = cm
                acc[hh] = _dot(jnp.exp2(sc - cm).astype(BF16), vr[0, c0:c0 + n, sl])
            else:
                m_new = jnp.maximum(m[hh], cm)
                acc[hh] = (jnp.exp2(m[hh] - m_new) * acc[hh]
                           + _dot(jnp.exp2(sc - m_new).astype(BF16), vr[0, c0:c0 + n, sl]))
                m[hh] = m_new
        if ci == len(chunks) - 1:
            o0 = acc[0] / acc[0][:, C_V:C_V + 1]
            o1 = acc[1] / acc[1][:, C_V:C_V + 1]
            o_ref[0, sb * tq:(sb + 1) * tq, :] = jnp.where(
                _lane_lo(o0.shape), o0, pltpu.roll(o1, C_V, 1)).astype(BF16)


def _attn_c(qc, kc, vc, kcx, vcx, tq, tstep):
    b, s, _ = qc.shape
    c = kcx.shape[1]

    def qmap(bb, p, i):
        return (bb, i, p)

    def kmap(bb, p, i):
        return (bb, 0, p)

    return pl.pallas_call(
        functools.partial(_attn_c_kernel, tq=tq, kc=512, ahead=1),
        grid=(b, C_HEADS // 2, s // tstep),
        in_specs=[
            pl.BlockSpec((1, tstep, 2 * LANES), qmap),
            pl.BlockSpec((1, s, 2 * LANES), kmap),
            pl.BlockSpec((1, c, 2 * LANES), kmap),
            pl.BlockSpec((1, s, 2 * LANES), kmap),
            pl.BlockSpec((1, c, 2 * LANES), kmap),
        ],
        out_specs=pl.BlockSpec((1, tstep, LANES), qmap),
        out_shape=jax.ShapeDtypeStruct((b, s, 256), BF16),
        compiler_params=_params("arbitrary", "arbitrary", "arbitrary"),
        name="attn_latent",
    )(qc, kc, kcx, vc, vcx)


def _attn_ctx_kernel(qa_ref, ka_ref, va_ref, qb_ref, kb_ref, vb_ref, qc_ref, kc_ref, vc_ref,
                     sink_ref, oa_ref, ob_ref, oc_ref):
    c = qa_ref.shape[1]
    half = A_HEADS // 2
    lo = _lane_lo((c, LANES))

    def blk(i):
        return slice(LANES * i, LANES * (i + 1))

    units = []
    for j in range(half):
        for g in range(A_KV_HEADS):
            def score(j=j, g=g):
                q2 = qa_ref[0, :, blk(j)]
                zero = jnp.zeros_like(q2)
                return _dot_t(jnp.where(lo, q2, zero) if g == 0 else jnp.where(lo, zero, q2), ka_ref[0])

            def fin_a(o, j=j):
                oa_ref[0, :, blk(j)] = jnp.where(lo, o[0], o[1]).astype(BF16)

            units.append((score, lambda g=g: va_ref[0, :, blk(g)], j + half * g, fin_a if g == 1 else None))
    for p in range(B_HEADS // 2):
        def score(p=p):
            return _dot_t(_split_pair(qb_ref[0, :, blk(p)]), kb_ref[0, :, blk(p)])

        def fin_b(o, p=p):
            ob_ref[0, :, blk(p)] = _merge_pair(o[0]).astype(BF16)

        units.append((score, lambda p=p: vb_ref[0, :, blk(p)], None, fin_b))
    for hd in range(C_HEADS):
        def score(hd=hd):
            return _dot_t(qc_ref[0, :, blk(hd)], kc_ref[0, :, blk(hd)])

        def fin_c(o, hd=hd):
            oc_ref[0, :, blk(hd // 2)] = jnp.where(lo, o[0], pltpu.roll(o[1], C_V, 1)).astype(BF16)

        units.append((score, lambda hd=hd: vc_ref[0, :, blk(hd)], None, fin_c if hd % 2 == 1 else None))

    pending = []
    nxt = units[0][0]()
    for u, (_, value, sink_head, finish) in enumerate(units):
        s = nxt
        if u + 1 < len(units):
            nxt = units[u + 1][0]()
        sink = None
        if sink_head is not None:
            sink = jnp.broadcast_to(sink_ref[sink_head:sink_head + 1, 0:1], (s.shape[0], 1))
        pending.append(_softmax_pv([(s, value())], sink))
        if finish is not None:
            finish(pending)
            pending = []


def _attn_ctx(parts, sinks, layer):
    b, c, _ = parts[0].shape
    widths = [a.shape[2] for a in parts]

    def bmap(bb):
        return (bb, 0, 0)

    return pl.pallas_call(
        _attn_ctx_kernel,
        grid=(b,),
        in_specs=[pl.BlockSpec((1, c, w), bmap) for w in widths]
        + [pl.BlockSpec((None, A_HEADS, LANES), lambda bb: (layer, 0, 0))],
        out_specs=[pl.BlockSpec((1, c, w), bmap) for w in (512, 256, 256)],
        out_shape=[jax.ShapeDtypeStruct((b, c, w), BF16) for w in (512, 256, 256)],
        compiler_params=_params("arbitrary"),
        name="attn_context",
    )(*parts, sinks)


def _outproj_mlp_kernel(x_ref, ma_ref, mb_ref, mc_ref, mod_ref, woa_ref, wob_ref, woc_ref,
                        g2_ref, w1_ref, w2_ref, gf_ref, o_ref, *, final, ff_chunk):
    mod = mod_ref[0]
    attn = _dot(ma_ref[0], woa_ref[...]) + _dot(mb_ref[0], wob_ref[...]) + _dot(mc_ref[0], woc_ref[...])
    x1 = x_ref[0] + mod[2:3] * attn
    h = (_rms(x1, g2_ref[...]) * (1.0 + mod[4:5]) + mod[3:4]).astype(BF16)
    d_ff = w1_ref.shape[1]
    def up(c0):
        return _dot(h, w1_ref[:, c0:c0 + ff_chunk])

    y = None
    chunk_starts = list(range(0, d_ff, ff_chunk))
    nxt = up(chunk_starts[0])
    for i, c0 in enumerate(chunk_starts):
        u = jnp.maximum(nxt, 0.0)
        if i + 1 < len(chunk_starts):
            nxt = up(chunk_starts[i + 1])
        part = _dot((u * u).astype(BF16), w2_ref[c0:c0 + ff_chunk, :])
        y = part if y is None else y + part
    x2 = x1 + mod[5:6] * y
    if final:
        x2 = _rms(x2, gf_ref[...])
    o_ref[0] = x2


def _outproj_mlp(x, ma, mb, mc, layer, mod_row, mods, woa, wob, woc, g2, w1, w2, gf, tm, final):
    b, s, d = x.shape
    d_ff = w1.shape[2]

    def xmap(bb, j):
        return (bb, j, 0)

    def modmap(bb, j):
        return (layer, (bb if mod_row is None else mod_row), 0, 0)

    def resident(shape):
        return pl.BlockSpec((None,) + shape, lambda bb, j: (layer, 0, 0), pipeline_mode=pl.Buffered(1))

    return pl.pallas_call(
        functools.partial(_outproj_mlp_kernel, final=final, ff_chunk=1024),
        grid=(b, s // tm),
        in_specs=[
            pl.BlockSpec((1, tm, d), xmap),
            pl.BlockSpec((1, tm, 512), xmap),
            pl.BlockSpec((1, tm, 256), xmap),
            pl.BlockSpec((1, tm, 256), xmap),
            pl.BlockSpec((None, 1, N_MOD, d), modmap),
            resident((512, d)), resident((256, d)), resident((256, d)),
            resident((1, d)),
            resident((d, d_ff)), resident((d_ff, d)),
            pl.BlockSpec((1, d), lambda bb, j: (0, 0)),
        ],
        out_specs=pl.BlockSpec((1, tm, d), xmap),
        out_shape=jax.ShapeDtypeStruct((b, s, d), F32),
        compiler_params=_params("arbitrary", "arbitrary"),
        name="outproj_mlp",
    )(x, ma, mb, mc, mods, woa, wob, woc, g2, w1, w2, gf)


def _rope_tables(s):
    f32 = np.float32
    tok = np.arange(s)
    row, col = (tok // GRID_W).astype(f32), (tok % GRID_W).astype(f32)

    def cs(pos, half):
        freqs = (f32(ROPE_BASE) ** (-np.arange(half, dtype=f32) / f32(half))).astype(f32)
        ang = (pos[:, None] * freqs).astype(f32)
        return np.cos(ang).astype(f32), np.sin(ang).astype(f32)

    cr, sr = cs(row, 16)
    cc, sc = cs(col, 16)
    z = np.zeros_like(sr)
    a_c = np.tile(np.concatenate([cr, cr, cc, cc], axis=1), (1, 2))
    a_s1 = np.tile(np.concatenate([-sr, z, -sc, z], axis=1), (1, 2))
    a_s2 = np.tile(np.concatenate([z, sr, z, sc], axis=1), (1, 2))
    cr, sr = cs(row, 8)
    cc, sc = cs(col, 8)
    z = np.zeros_like(sr)
    one64, zero64 = np.ones((s, C_NOPE), f32), np.zeros((s, C_NOPE), f32)
    one32, zero32 = np.ones((s, 32), f32), np.zeros((s, 32), f32)
    c_c = np.concatenate([one64, cr, cr, cc, cc, one32], axis=1)
    c_s1 = np.concatenate([zero64, -sr, z, -sc, z, zero32], axis=1)
    c_s2 = np.concatenate([zero64, z, sr, z, sc, zero32], axis=1)
    return tuple(jnp.asarray(t) for t in (a_c, a_s1, a_s2, c_c, c_s1, c_s2))


def _identity_tables(s):
    one, zero = jnp.asarray(np.ones((s, LANES), np.float32)), jnp.asarray(np.zeros((s, LANES), np.float32))
    return (one, zero, zero, one, zero, zero)


def _pad_heads(w, offs, width):
    pad = jnp.zeros(w.shape[:2] + (LANES - width,), w.dtype)
    return jnp.concatenate([jnp.concatenate([w[:, :, o:o + width], pad], axis=2) for o in offs], axis=2)


def _weight_layouts(w_in, w_uq, w_ukv, w_out):
    depth, d, _ = w_in.shape
    qa_w = w_in[:, :, :512].reshape(depth, d, A_KV_HEADS, A_GROUP, HEAD_DIM).transpose(0, 1, 3, 2, 4)
    qa_w = qa_w.reshape(depth, d, 512)
    z = lambda n: jnp.zeros((depth, d, n), w_in.dtype)
    kr_w = jnp.concatenate([z(64), w_in[:, :, 1920:1952], z(32)], axis=2)
    win = jnp.concatenate([qa_w, w_in[:, :, 512:1920], kr_w], axis=2).astype(BF16)
    hq = C_NOPE + C_ROPE
    wuq = _pad_heads(w_uq, [hq * h for h in range(C_HEADS)], hq).astype(BF16)
    wukv = w_ukv.astype(BF16)
    woa = w_out[:, :512].reshape(depth, A_KV_HEADS, A_GROUP, HEAD_DIM, d).transpose(0, 2, 1, 3, 4)
    woa = woa.reshape(depth, 512, d).astype(BF16)
    wob = w_out[:, 512:768].astype(BF16)
    woc = w_out[:, 768:].astype(BF16)
    return win, wuq, wukv, woa, wob, woc


def kernel(x, c, ctx, c_ctx, w_ada, b_ada, norm1_g, norm2_g, w_in, attn_sink, na_rpb, mla_q_norm_g,
           mla_w_uq, mla_kv_norm_g, mla_w_ukv, w_out, w_mlp_in, w_mlp_out, final_norm_g):
    b, s, d = x.shape
    n_ctx = ctx.shape[1]
    depth = w_ada.shape[0]
    assert b + 1 <= MOD_ROWS and s % 1024 == 0 and n_ctx % 128 == 0
    tm_x = 512
    tm_c = min(n_ctx, 256)
    tq_a = 256
    tq_c = 512

    cvec = jnp.concatenate([c, c_ctx[None], jnp.zeros((MOD_ROWS - b - 1, d), c.dtype)], axis=0)
    mods = _modulation(cvec, w_ada, b_ada).reshape(depth, MOD_ROWS, N_MOD, d)
    tabs_x = _rope_tables(s)
    tabs_c = _identity_tables(n_ctx)
    gf = final_norm_g.reshape(1, d)
    win, wuq, wukv, woa, wob, woc = _weight_layouts(w_in, mla_w_uq, mla_w_ukv, w_out)
    w1, w2 = w_mlp_in.astype(BF16), w_mlp_out.astype(BF16)
    sinks = jnp.broadcast_to((attn_sink * LOG2E)[:, :, None], (depth, A_HEADS, LANES)).astype(F32)
    bias = _na_bias_table(na_rpb)
    g1, g2 = norm1_g.reshape(depth, 1, d), norm2_g.reshape(depth, 1, d)
    gq, gkv = mla_q_norm_g.reshape(depth, 1, -1), mla_kv_norm_g.reshape(depth, 1, -1)

    for l in range(depth):
        last = l == depth - 1
        xs = _inproj(x, l, None, mods, g1, win, gq, wuq, gkv, wukv, tabs_x, 2 * tm_x)
        cs = _inproj(ctx, l, b, mods, g1, win, gq, wuq, gkv, wukv, tabs_c, tm_c)
        qa, ka, va, qb, kb, vb, qc, kc, vc = xs
        cqa, cka, cva, cqb, ckb, cvb, cqc, ckc, cvc = cs

        oa = _attn_a(qa, ka, va, cka, cva, sinks, l, tq_a, 4 * tq_a)
        ob = _attn_b(qb, kb, vb, ckb, cvb, bias, l)
        oc = _attn_c(qc, kc, vc, ckc, cvc, tq_c // 2, 2 * tq_c)
        x = _outproj_mlp(x, oa, ob, oc, l, None, mods, woa, wob, woc, g2, w1, w2, gf, 2 * tm_x, last)
        if not last:
            coa, cob, coc = _attn_ctx(cs, sinks, l)
            ctx = _outproj_mlp(ctx, coa, cob, coc, l, b, mods, woa, wob, woc, g2, w1, w2, gf, tm_c, False)
    return x
```

```python
import functools
import math

import jax
import jax.numpy as jnp
import numpy as np
from jax import lax
from jax.experimental import pallas as pl
from jax.experimental.pallas import tpu as pltpu

F32 = jnp.float32
BF16 = jnp.bfloat16

GRID_W = 64
HEAD_DIM = 64
A_HEADS = 8
A_KV_HEADS = 2
A_GROUP = A_HEADS // A_KV_HEADS
A_WINDOW = 128
B_HEADS = 4
NA_ROWS = 8
NA_COLS = 16
C_HEADS = 4
C_Q_RANK = 256
C_KV_RANK = 128
C_NOPE = 64
C_ROPE = 32
C_V = 64
N_MOD = 6
ROPE_BASE = 10000.0
EPS = 1e-6
NEG_INF = -1e30
LOG2E = math.log2(math.e)

LANES = 128
MOD_ROWS = 16
IN_COLS = 2048
VMEM_LIMIT = 56 * 1024 * 1024

_OFF_QA, _OFF_KA, _OFF_VA = 0, 512, 640
_OFF_QB, _OFF_KB, _OFF_VB = 768, 1024, 1280
_OFF_CQ, _OFF_CKV, _OFF_KR = 1536, 1792, 1920


def _dot(a, b):
    return jnp.dot(a, b, preferred_element_type=F32)


def _dot_t(a, b):
    return lax.dot_general(a, b, (((1,), (1,)), ((), ())), preferred_element_type=F32)


def _params(*sem):
    return pltpu.CompilerParams(dimension_semantics=sem, vmem_limit_bytes=VMEM_LIMIT)


def _mod_kernel(c_ref, w_ref, b_ref, o_ref):
    c = c_ref[...]
    s = c * (1.0 / (1.0 + jnp.exp(-c)))
    o_ref[0] = _dot(s.astype(BF16), w_ref[0].astype(BF16)) + b_ref[0]


def _modulation(cvec, w_ada, b_ada):
    depth, d, n = w_ada.shape
    tn = 1024
    return pl.pallas_call(
        _mod_kernel,
        grid=(depth, n // tn),
        in_specs=[
            pl.BlockSpec((MOD_ROWS, d), lambda l, j: (0, 0)),
            pl.BlockSpec((1, d, tn), lambda l, j: (l, 0, j)),
            pl.BlockSpec((1, 1, tn), lambda l, j: (l, 0, j)),
        ],
        out_specs=pl.BlockSpec((1, MOD_ROWS, tn), lambda l, j: (l, 0, j)),
        out_shape=jax.ShapeDtypeStruct((depth, MOD_ROWS, n), F32),
        compiler_params=_params("arbitrary", "arbitrary"),
        name="adaln_modulation",
    )(cvec, w_ada, b_ada.reshape(depth, 1, n))


def _rope(x, c, s1, s2, shift):
    return x * c + pltpu.roll(x, LANES - shift, 1) * s1 + pltpu.roll(x, shift, 1) * s2


def _rms(x, g):
    return x * lax.rsqrt(jnp.mean(x * x, axis=-1, keepdims=True) + EPS) * g


def _inproj_kernel(x_ref, mod_ref, g1_ref, win_ref, gq_ref, wuq_ref, gkv_ref, wukv_ref,
                   ac_ref, as1_ref, as2_ref, cc_ref, cs1_ref, cs2_ref,
                   qa_ref, ka_ref, va_ref, qb_ref, kb_ref, vb_ref, qc_ref, kc_ref, vc_ref, *, sub):
    tm = x_ref.shape[1]
    mod = mod_ref[0]
    scale_ab = HEAD_DIM ** -0.5 * LOG2E
    scale_c = (C_NOPE + C_ROPE) ** -0.5 * LOG2E
    ones_lane = (lax.broadcasted_iota(jnp.int32, (1, LANES), 1) == C_V).astype(F32)

    subs = [slice(r0, r0 + sub) for r0 in range(0, tm, sub)]

    def up_project(pc):
        cq = _rms(pc[:, 0:C_Q_RANK], gq_ref[...]).astype(BF16)
        ckv = _rms(pc[:, C_Q_RANK:C_Q_RANK + C_KV_RANK], gkv_ref[...]).astype(BF16)
        return _dot(cq, wuq_ref[...]), _dot(ckv, wukv_ref[...])

    pabs, pcs, ups = [], [], []
    for rs in subs:
        h = (_rms(x_ref[0, rs, :], g1_ref[...]) * (1.0 + mod[1:2]) + mod[0:1]).astype(BF16)
        pcs.append(_dot(h, win_ref[:, _OFF_CQ:IN_COLS]))
        pabs.append(_dot(h, win_ref[:, 0:_OFF_CQ]))
        ups.append(up_project(pcs[-1]))

    for rs, p, pc, (q_up, kv_up) in zip(subs, pabs, pcs, ups):
        ac, as1, as2 = ac_ref[rs, :], as1_ref[rs, :], as2_ref[rs, :]
        cc, cs1, cs2 = cc_ref[rs, :], cs1_ref[rs, :], cs2_ref[rs, :]
        for j in range(4):
            blk = p[:, _OFF_QA + LANES * j:_OFF_QA + LANES * (j + 1)]
            qa_ref[0, rs, LANES * j:LANES * (j + 1)] = (_rope(blk, ac, as1, as2, 16) * scale_ab).astype(BF16)
        ka_ref[0, rs, :] = _rope(p[:, _OFF_KA:_OFF_KA + LANES], ac, as1, as2, 16).astype(BF16)
        va = p[:, _OFF_VA:_OFF_VA + LANES]
        lane = lax.broadcasted_iota(jnp.int32, va.shape, 1)
        va_ref[0, rs, 0:LANES] = jnp.where(lane < HEAD_DIM, va, jnp.where(lane == HEAD_DIM, 1.0, 0.0)).astype(BF16)
        va_ref[0, rs, LANES:2 * LANES] = jnp.where(lane >= HEAD_DIM, va, jnp.where(lane == 0, 1.0, 0.0)).astype(BF16)
        qb_ref[0, rs, :] = (p[:, _OFF_QB:_OFF_QB + 256] * scale_ab).astype(BF16)
        kb_ref[0, rs, :] = p[:, _OFF_KB:_OFF_KB + 256].astype(BF16)
        vb_ref[0, rs, :] = p[:, _OFF_VB:_OFF_VB + 256].astype(BF16)
        kr = _rope(pc[:, _OFF_KR - _OFF_CQ:], cc, cs1, cs2, 8)
        lo = lane < C_NOPE
        for hd in range(C_HEADS):
            sl = slice(LANES * hd, LANES * (hd + 1))
            qc_ref[0, rs, sl] = (_rope(q_up[:, sl], cc, cs1, cs2, 8) * scale_c).astype(BF16)
            kv = kv_up[:, sl]
            kc_ref[0, rs, sl] = jnp.where(lo, kv, kr).astype(BF16)
            vc_ref[0, rs, sl] = jnp.where(lo, pltpu.roll(kv, C_V, 1), ones_lane).astype(BF16)


def _inproj(x, layer, mod_row, mods, g1, win, gq, wuq, gkv, wukv, tabs, tm):
    b, s, d = x.shape
    nt = s // tm

    def xmap(j, bb):
        return (bb, j, 0)

    def modmap(j, bb):
        return (layer, (bb if mod_row is None else mod_row), 0, 0)

    def wmap(j, bb):
        return (layer, 0, 0)

    def tabmap(j, bb):
        return (j, 0)

    widths = (512, 128, 256, 256, 256, 256, 512, 512, 512)
    in_specs = [
        pl.BlockSpec((1, tm, d), xmap),
        pl.BlockSpec((None, 1, N_MOD, d), modmap),
        pl.BlockSpec((None, 1, d), wmap),
        pl.BlockSpec((None, d, IN_COLS), wmap),
        pl.BlockSpec((None, 1, C_Q_RANK), wmap),
        pl.BlockSpec((None, C_Q_RANK, 512), wmap),
        pl.BlockSpec((None, 1, C_KV_RANK), wmap),
        pl.BlockSpec((None, C_KV_RANK, 512), wmap),
    ] + [pl.BlockSpec((tm, LANES), tabmap) for _ in range(6)]
    return pl.pallas_call(
        functools.partial(_inproj_kernel, sub=min(tm, 256)),
        grid=(nt, b),
        in_specs=in_specs,
        out_specs=[pl.BlockSpec((1, tm, w), xmap) for w in widths],
        out_shape=[jax.ShapeDtypeStruct((b, s, w), BF16) for w in widths],
        compiler_params=_params("arbitrary", "arbitrary"),
        name="norm_inproj_rope",
    )(x, mods, g1, win, gq, wuq, gkv, wukv, *tabs)


def _lane_lo(shape):
    return lax.broadcasted_iota(jnp.int32, shape, 1) < HEAD_DIM


def _split_pair(q2):
    lo = _lane_lo(q2.shape)
    zero = jnp.zeros_like(q2)
    return jnp.concatenate([jnp.where(lo, q2, zero), jnp.where(lo, zero, q2)], axis=0)


def _merge_pair(o):
    n = o.shape[0] // 2
    return jnp.where(_lane_lo((n, o.shape[1])), o[:n], o[n:])


def _softmax_pv(parts, sink=None):
    m = parts[0][0].max(axis=-1, keepdims=True)
    for s, _ in parts[1:]:
        m = jnp.maximum(m, s.max(axis=-1, keepdims=True))
    if sink is not None:
        m = jnp.maximum(m, sink)
    denom = None
    acc = None
    for s, v in parts:
        e = jnp.exp2(s - m)
        r = e.sum(axis=-1, keepdims=True)
        denom = r if denom is None else denom + r
        o = _dot(e.astype(BF16), v)
        acc = o if acc is None else acc + o
    if sink is not None:
        denom = denom + jnp.exp2(sink - m)
    return acc / denom


def _attn_a_kernel(q_ref, k_ref, v_ref, kx_ref, vx_ref, sink_ref, o_ref, *, tq, seq, ahead):
    t = pl.program_id(1)
    nsub = q_ref.shape[1] // tq
    nloc = tq + 2 * A_WINDOW
    kx = kx_ref[0]
    lo = _lane_lo((tq, LANES))
    iota_rel = (lax.broadcasted_iota(jnp.int32, (tq, nloc), 1) - lax.broadcasted_iota(jnp.int32, (tq, nloc), 0))
    starts, bands = [], []
    for sb in range(nsub):
        qstart = (t * nsub + sb) * tq
        start = pl.multiple_of(jnp.clip(qstart - A_WINDOW, 0, seq - nloc), A_WINDOW)
        rel = iota_rel + (start - qstart)
        starts.append(start)
        bands.append(jnp.where(jnp.abs(rel) <= A_WINDOW, 0.0, NEG_INF))
    units = [(sb, j, g) for sb in range(nsub) for j in range(A_GROUP) for g in range(A_KV_HEADS)]

    def scores(u):
        sb, j, g = units[u]
        q2 = q_ref[0, sb * tq:(sb + 1) * tq, LANES * j:LANES * (j + 1)]
        zero = jnp.zeros_like(q2)
        q = jnp.where(lo, q2, zero) if g == 0 else jnp.where(lo, zero, q2)
        return _dot_t(q, k_ref[0, pl.ds(starts[sb], nloc), :]) + bands[sb], _dot_t(q, kx)

    outs = {}
    queue = [scores(u) for u in range(min(ahead, len(units)))]
    for u, (sb, j, g) in enumerate(units):
        s_loc, s_ctx = queue.pop(0)
        if u + ahead < len(units):
            queue.append(scores(u + ahead))
        vsg = v_ref[0, pl.ds(starts[sb], nloc), LANES * g:LANES * (g + 1)]
        vxg = vx_ref[0, :, LANES * g:LANES * (g + 1)]
        hd = g * A_GROUP + j
        sink = jnp.broadcast_to(sink_ref[hd:hd + 1, 0:1], (tq, 1))
        m = jnp.maximum(jnp.maximum(s_loc.max(axis=-1, keepdims=True), s_ctx.max(axis=-1, keepdims=True)), sink)
        o = _dot(jnp.exp2(s_loc - m).astype(BF16), vsg) + _dot(jnp.exp2(s_ctx - m).astype(BF16), vxg)
        denom = (o[:, HEAD_DIM:HEAD_DIM + 1] if g == 0 else o[:, 0:1]) + jnp.exp2(sink - m)
        outs[g] = o / denom
        if g == A_KV_HEADS - 1:
            o_ref[0, sb * tq:(sb + 1) * tq, LANES * j:LANES * (j + 1)] = jnp.where(lo, outs[0], outs[1]).astype(BF16)


def _attn_a(qa, ka, va, kax, vax, sinks, layer, tq, tstep):
    b, s, _ = qa.shape
    c = kax.shape[1]

    def qmap(bb, t):
        return (bb, t, 0)

    def kmap(bb, t):
        return (bb, 0, 0)

    return pl.pallas_call(
        functools.partial(_attn_a_kernel, tq=tq, seq=s, ahead=2),
        grid=(b, s // tstep),
        in_specs=[
            pl.BlockSpec((1, tstep, 512), qmap),
            pl.BlockSpec((1, s, LANES), kmap),
            pl.BlockSpec((1, s, 2 * LANES), kmap),
            pl.BlockSpec((1, c, LANES), kmap),
            pl.BlockSpec((1, c, 2 * LANES), kmap),
            pl.BlockSpec((None, A_HEADS, LANES), lambda bb, t: (layer, 0, 0)),
        ],
        out_specs=pl.BlockSpec((1, tstep, 512), qmap),
        out_shape=jax.ShapeDtypeStruct((b, s, 512), BF16),
        compiler_params=_params("arbitrary", "arbitrary"),
        name="attn_window_gqa",
    )(qa, ka, va, kax, vax, sinks)


def _attn_b_kernel(q_ref, k_ref, v_ref, kx_ref, vx_ref, bias_ref, o_ref, *, rows, ahead):
    pair, rb = pl.program_id(1), pl.program_id(2)
    kx, vx = kx_ref[0], vx_ref[0]
    nrow = q_ref.shape[1] // GRID_W

    def scores(r):
        qr = rb * nrow + r
        wr = jnp.clip(qr - NA_ROWS // 2, 0, rows - NA_ROWS)
        oi = wr - qr + (NA_ROWS - 1)
        start = pl.multiple_of(wr * GRID_W, GRID_W)
        q = _split_pair(q_ref[0, r * GRID_W:(r + 1) * GRID_W, :])
        ks = k_ref[0, pl.ds(start, NA_ROWS * GRID_W), :]
        bias = jnp.concatenate([bias_ref[pair, oi], bias_ref[pair, NA_ROWS + oi]], axis=0)
        return _dot_t(q, ks) + bias, _dot_t(q, kx), start

    queue = [scores(r) for r in range(min(ahead, nrow))]
    for r in range(nrow):
        s_loc, s_ctx, start = queue.pop(0)
        if r + ahead < nrow:
            queue.append(scores(r + ahead))
        vs = v_ref[0, pl.ds(start, NA_ROWS * GRID_W), :]
        o = _softmax_pv([(s_loc, vs), (s_ctx, vx)])
        o_ref[0, r * GRID_W:(r + 1) * GRID_W, :] = _merge_pair(o).astype(BF16)


def _attn_b(qb, kb, vb, kbx, vbx, bias, layer):
    b, s, _ = qb.shape
    c = kbx.shape[1]
    rows = s // GRID_W
    tq = min(4 * NA_ROWS * GRID_W, s)

    def qmap(bb, p, rb):
        return (bb, rb, p)

    def kmap(bb, p, rb):
        return (bb, 0, p)

    return pl.pallas_call(
        functools.partial(_attn_b_kernel, rows=rows, ahead=3),
        grid=(b, B_HEADS // 2, s // tq),
        in_specs=[
            pl.BlockSpec((1, tq, LANES), qmap),
            pl.BlockSpec((1, s, LANES), kmap),
            pl.BlockSpec((1, s, LANES), kmap),
            pl.BlockSpec((1, c, LANES), kmap),
            pl.BlockSpec((1, c, LANES), kmap),
            pl.BlockSpec((None, B_HEADS // 2, 2 * NA_ROWS, GRID_W, NA_ROWS * GRID_W),
                         lambda bb, p, rb: (layer, 0, 0, 0, 0)),
        ],
        out_specs=pl.BlockSpec((1, tq, LANES), qmap),
        out_shape=jax.ShapeDtypeStruct((b, s, 256), BF16),
        compiler_params=_params("arbitrary", "arbitrary", "arbitrary"),
        name="attn_neighbourhood",
    )(qb, kb, vb, kbx, vbx, bias)


def _na_bias_table(rpb):
    depth, h = rpb.shape[:2]
    qc = jnp.arange(GRID_W)[:, None]
    kc = jnp.arange(GRID_W)[None, :]
    wc = jnp.clip(qc - NA_COLS // 2, 0, GRID_W - NA_COLS)
    valid = (kc >= wc) & (kc < wc + NA_COLS)
    pad = GRID_W - NA_COLS
    rp = jnp.pad(rpb, ((0, 0), (0, 0), (0, 0), (pad, pad)))
    toe = jnp.stack([rp[..., GRID_W - 1 - q:2 * GRID_W - 1 - q] for q in range(GRID_W)], axis=3)
    toe = jnp.where(valid, toe * LOG2E, NEG_INF)
    t = jnp.stack([jnp.stack([toe[:, :, o + j] for j in range(NA_ROWS)], axis=3) for o in range(NA_ROWS)], axis=2)
    return t.reshape(depth, h // 2, 2 * NA_ROWS, GRID_W, NA_ROWS * GRID_W).astype(F32)


def _attn_c_kernel(q_ref, k_ref, kx_ref, v_ref, vx_ref, o_ref, *, tq, kc, ahead):
    s = k_ref.shape[1]
    nsub = q_ref.shape[1] // tq
    chunks = [(k_ref, v_ref, c0, kc) for c0 in range(0, s, kc)] + [(kx_ref, vx_ref, 0, kx_ref.shape[1])]
    units = [(sb, ci) for sb in range(nsub) for ci in range(len(chunks))]

    def scores(u):
        sb, ci = units[u]
        kr, _, c0, n = chunks[ci]
        return [_dot_t(q_ref[0, sb * tq:(sb + 1) * tq, LANES * hh:LANES * (hh + 1)],
                       kr[0, c0:c0 + n, LANES * hh:LANES * (hh + 1)]) for hh in range(2)]

    m = [None, None]
    acc = [None, None]
    queue = [scores(u) for u in range(min(ahead, len(units)))]
    for u, (sb, ci) in enumerate(units):
        cur = queue.pop(0)
        if u + ahead < len(units):
            queue.append(scores(u + ahead))
        _, vr, c0, n = chunks[ci]
        for hh in range(2):
            sl = slice(LANES * hh, LANES * (hh + 1))
            sc = cur[hh]
            cm = sc.max(axis=-1, keepdims=True)
            if ci == 0:
                m[hh] = cm
                acc[hh] = _dot(jnp.exp2(sc - cm).astype(BF16), vr[0, c0:c0 + n, sl])
            else:
                m_new = jnp.maximum(m[hh], cm)
                acc[hh] = (jnp.exp2(m[hh] - m_new) * acc[hh]
                           + _dot(jnp.exp2(sc - m_new).astype(BF16), vr[0, c0:c0 + n, sl]))
                m[hh] = m_new
        if ci == len(chunks) - 1:
            o0 = acc[0] / acc[0][:, C_V:C_V + 1]
            o1 = acc[1] / acc[1][:, C_V:C_V + 1]
            o_ref[0, sb * tq:(sb + 1) * tq, :] = jnp.where(
                _lane_lo(o0.shape), o0, pltpu.roll(o1, C_V, 1)).astype(BF16)


def _attn_c(qc, kc, vc, kcx, vcx, tq, tstep):
    b, s, _ = qc.shape
    c = kcx.shape[1]

    def qmap(bb, p, i):
        return (bb, i, p)

    def kmap(bb, p, i):
        return (bb, 0, p)

    return pl.pallas_call(
        functools.partial(_attn_c_kernel, tq=tq, kc=512, ahead=1),
        grid=(b, C_HEADS // 2, s // tstep),
        in_specs=[
            pl.BlockSpec((1, tstep, 2 * LANES), qmap),
            pl.BlockSpec((1, s, 2 * LANES), kmap),
            pl.BlockSpec((1, c, 2 * LANES), kmap),
            pl.BlockSpec((1, s, 2 * LANES), kmap),
            pl.BlockSpec((1, c, 2 * LANES), kmap),
        ],
        out_specs=pl.BlockSpec((1, tstep, LANES), qmap),
        out_shape=jax.ShapeDtypeStruct((b, s, 256), BF16),
        compiler_params=_params("arbitrary", "arbitrary", "arbitrary"),
        name="attn_latent",
    )(qc, kc, kcx, vc, vcx)


def _attn_ctx_kernel(qa_ref, ka_ref, va_ref, qb_ref, kb_ref, vb_ref, qc_ref, kc_ref, vc_ref,
                     sink_ref, oa_ref, ob_ref, oc_ref):
    c = qa_ref.shape[1]
    half = A_HEADS // 2
    lo = _lane_lo((c, LANES))

    def blk(i):
        return slice(LANES * i, LANES * (i + 1))

    units = []
    for j in range(half):
        for g in range(A_KV_HEADS):
            def score(j=j, g=g):
                q2 = qa_ref[0, :, blk(j)]
                zero = jnp.zeros_like(q2)
                return _dot_t(jnp.where(lo, q2, zero) if g == 0 else jnp.where(lo, zero, q2), ka_ref[0])

            def fin_a(o, j=j):
                oa_ref[0, :, blk(j)] = jnp.where(lo, o[0], o[1]).astype(BF16)

            units.append((score, lambda g=g: va_ref[0, :, blk(g)], j + half * g, fin_a if g == 1 else None))
    for p in range(B_HEADS // 2):
        def score(p=p):
            return _dot_t(_split_pair(qb_ref[0, :, blk(p)]), kb_ref[0, :, blk(p)])

        def fin_b(o, p=p):
            ob_ref[0, :, blk(p)] = _merge_pair(o[0]).astype(BF16)

        units.append((score, lambda p=p: vb_ref[0, :, blk(p)], None, fin_b))
    for hd in range(C_HEADS):
        def score(hd=hd):
            return _dot_t(qc_ref[0, :, blk(hd)], kc_ref[0, :, blk(hd)])

        def fin_c(o, hd=hd):
            oc_ref[0, :, blk(hd // 2)] = jnp.where(lo, o[0], pltpu.roll(o[1], C_V, 1)).astype(BF16)

        units.append((score, lambda hd=hd: vc_ref[0, :, blk(hd)], None, fin_c if hd % 2 == 1 else None))

    pending = []
    nxt = units[0][0]()
    for u, (_, value, sink_head, finish) in enumerate(units):
        s = nxt
        if u + 1 < len(units):
            nxt = units[u + 1][0]()
        sink = None
        if sink_head is not None:
            sink = jnp.broadcast_to(sink_ref[sink_head:sink_head + 1, 0:1], (s.shape[0], 1))
        pending.append(_softmax_pv([(s, value())], sink))
        if finish is not None:
            finish(pending)
            pending = []


def _attn_ctx(parts, sinks, layer):
    b, c, _ = parts[0].shape
    widths = [a.shape[2] for a in parts]

    def bmap(bb):
        return (bb, 0, 0)

    return pl.pallas_call(
        _attn_ctx_kernel,
        grid=(b,),
        in_specs=[pl.BlockSpec((1, c, w), bmap) for w in widths]
        + [pl.BlockSpec((None, A_HEADS, LANES), lambda bb: (layer, 0, 0))],
        out_specs=[pl.BlockSpec((1, c, w), bmap) for w in (512, 256, 256)],
        out_shape=[jax.ShapeDtypeStruct((b, c, w), BF16) for w in (512, 256, 256)],
        compiler_params=_params("arbitrary"),
        name="attn_context",
    )(*parts, sinks)


def _outproj_mlp_kernel(x_ref, ma_ref, mb_ref, mc_ref, mod_ref, woa_ref, wob_ref, woc_ref,
                        g2_ref, w1_ref, w2_ref, gf_ref, o_ref, *, final, ff_chunk):
    mod = mod_ref[0]
    attn = _dot(ma_ref[0], woa_ref[...]) + _dot(mb_ref[0], wob_ref[...]) + _dot(mc_ref[0], woc_ref[...])
    x1 = x_ref[0] + mod[2:3] * attn
    h = (_rms(x1, g2_ref[...]) * (1.0 + mod[4:5]) + mod[3:4]).astype(BF16)
    d_ff = w1_ref.shape[1]
    def up(c0):
        return _dot(h, w1_ref[:, c0:c0 + ff_chunk])

    y = None
    chunk_starts = list(range(0, d_ff, ff_chunk))
    nxt = up(chunk_starts[0])
    for i, c0 in enumerate(chunk_starts):
        u = jnp.maximum(nxt, 0.0)
        if i + 1 < len(chunk_starts):
            nxt = up(chunk_starts[i + 1])
        part = _dot((u * u).astype(BF16), w2_ref[c0:c0 + ff_chunk, :])
        y = part if y is None else y + part
    x2 = x1 + mod[5:6] * y
    if final:
        x2 = _rms(x2, gf_ref[...])
    o_ref[0] = x2


def _outproj_mlp(x, ma, mb, mc, layer, mod_row, mods, woa, wob, woc, g2, w1, w2, gf, tm, final):
    b, s, d = x.shape
    d_ff = w1.shape[2]

    def xmap(bb, j):
        return (bb, j, 0)

    def modmap(bb, j):
        return (layer, (bb if mod_row is None else mod_row), 0, 0)

    def resident(shape, row_block=0):
        return pl.BlockSpec((None,) + shape, lambda bb, j: (layer, row_block, 0), pipeline_mode=pl.Buffered(1))

    return pl.pallas_call(
        functools.partial(_outproj_mlp_kernel, final=final, ff_chunk=1024),
        grid=(b, s // tm),
        in_specs=[
            pl.BlockSpec((1, tm, d), xmap),
            pl.BlockSpec((1, tm, 512), xmap),
            pl.BlockSpec((1, tm, 256), xmap),
            pl.BlockSpec((1, tm, 256), xmap),
            pl.BlockSpec((None, 1, N_MOD, d), modmap),
            resident((512, d)), resident((256, d), 2), resident((256, d), 3),
            resident((1, d)),
            resident((d, d_ff)), resident((d_ff, d)),
            pl.BlockSpec((1, d), lambda bb, j: (0, 0)),
        ],
        out_specs=pl.BlockSpec((1, tm, d), xmap),
        out_shape=jax.ShapeDtypeStruct((b, s, d), F32),
        compiler_params=_params("arbitrary", "arbitrary"),
        name="outproj_mlp",
    )(x, ma, mb, mc, mods, woa, wob, woc, g2, w1, w2, gf)


def _rope_tables(s):
    f32 = np.float32
    tok = np.arange(s)
    row, col = (tok // GRID_W).astype(f32), (tok % GRID_W).astype(f32)

    def cs(pos, half):
        freqs = (f32(ROPE_BASE) ** (-np.arange(half, dtype=f32) / f32(half))).astype(f32)
        ang = (pos[:, None] * freqs).astype(f32)
        return np.cos(ang).astype(f32), np.sin(ang).astype(f32)

    cr, sr = cs(row, 16)
    cc, sc = cs(col, 16)
    z = np.zeros_like(sr)
    a_c = np.tile(np.concatenate([cr, cr, cc, cc], axis=1), (1, 2))
    a_s1 = np.tile(np.concatenate([-sr, z, -sc, z], axis=1), (1, 2))
    a_s2 = np.tile(np.concatenate([z, sr, z, sc], axis=1), (1, 2))
    cr, sr = cs(row, 8)
    cc, sc = cs(col, 8)
    z = np.zeros_like(sr)
    one64, zero64 = np.ones((s, C_NOPE), f32), np.zeros((s, C_NOPE), f32)
    one32, zero32 = np.ones((s, 32), f32), np.zeros((s, 32), f32)
    c_c = np.concatenate([one64, cr, cr, cc, cc, one32], axis=1)
    c_s1 = np.concatenate([zero64, -sr, z, -sc, z, zero32], axis=1)
    c_s2 = np.concatenate([zero64, z, sr, z, sc, zero32], axis=1)
    return tuple(jnp.asarray(t) for t in (a_c, a_s1, a_s2, c_c, c_s1, c_s2))


def _identity_tables(s):
    one, zero = jnp.asarray(np.ones((s, LANES), np.float32)), jnp.asarray(np.zeros((s, LANES), np.float32))
    return (one, zero, zero, one, zero, zero)


def _pad_heads(w, offs, width):
    pad = jnp.zeros(w.shape[:2] + (LANES - width,), w.dtype)
    return jnp.concatenate([jnp.concatenate([w[:, :, o:o + width], pad], axis=2) for o in offs], axis=2)


def _weight_layouts(w_in, w_uq, w_ukv, w_out):
    depth, d, _ = w_in.shape
    w_in, w_out = w_in.astype(BF16), w_out.astype(BF16)
    qa_w = w_in[:, :, :512].reshape(depth, d, A_KV_HEADS, A_GROUP, HEAD_DIM).transpose(0, 1, 3, 2, 4)
    qa_w = qa_w.reshape(depth, d, 512)
    z = lambda n: jnp.zeros((depth, d, n), w_in.dtype)
    win = jnp.concatenate([qa_w, w_in[:, :, 512:1920], z(64), w_in[:, :, 1920:1952], z(32)], axis=2)
    hq = C_NOPE + C_ROPE
    wuq = _pad_heads(w_uq, [hq * h for h in range(C_HEADS)], hq).astype(BF16)
    wukv = w_ukv.astype(BF16)
    woa = w_out[:, :512].reshape(depth, A_KV_HEADS, A_GROUP, HEAD_DIM, d).transpose(0, 2, 1, 3, 4)
    woa = woa.reshape(depth, 512, d)
    wob = woc = w_out
    return win, wuq, wukv, woa, wob, woc


def kernel(x, c, ctx, c_ctx, w_ada, b_ada, norm1_g, norm2_g, w_in, attn_sink, na_rpb, mla_q_norm_g,
           mla_w_uq, mla_kv_norm_g, mla_w_ukv, w_out, w_mlp_in, w_mlp_out, final_norm_g):
    b, s, d = x.shape
    n_ctx = ctx.shape[1]
    depth = w_ada.shape[0]
    assert b + 1 <= MOD_ROWS and s % 1024 == 0 and n_ctx % 128 == 0
    tm_x = 512
    tm_c = min(n_ctx, 256)
    tq_a = 256
    tq_c = 512

    cvec = jnp.concatenate([c, c_ctx[None], jnp.zeros((MOD_ROWS - b - 1, d), c.dtype)], axis=0)
    mods = _modulation(cvec, w_ada, b_ada).reshape(depth, MOD_ROWS, N_MOD, d)
    tabs_x = _rope_tables(s)
    tabs_c = _identity_tables(n_ctx)
    gf = final_norm_g.reshape(1, d)
    win, wuq, wukv, woa, wob, woc = _weight_layouts(w_in, mla_w_uq, mla_w_ukv, w_out)
    w1, w2 = w_mlp_in.astype(BF16), w_mlp_out.astype(BF16)
    sinks = jnp.broadcast_to((attn_sink * LOG2E)[:, :, None], (depth, A_HEADS, LANES)).astype(F32)
    bias = _na_bias_table(na_rpb)
    g1, g2 = norm1_g.reshape(depth, 1, d), norm2_g.reshape(depth, 1, d)
    gq, gkv = mla_q_norm_g.reshape(depth, 1, -1), mla_kv_norm_g.reshape(depth, 1, -1)

    for l in range(depth):
        last = l == depth - 1
        xs = _inproj(x, l, None, mods, g1, win, gq, wuq, gkv, wukv, tabs_x, 2 * tm_x)
        cs = _inproj(ctx, l, b, mods, g1, win, gq, wuq, gkv, wukv, tabs_c, tm_c)
        qa, ka, va, qb, kb, vb, qc, kc, vc = xs
        cqa, cka, cva, cqb, ckb, cvb, cqc, ckc, cvc = cs

        oa = _attn_a(qa, ka, va, cka, cva, sinks, l, tq_a, 4 * tq_a)
        ob = _attn_b(qb, kb, vb, ckb, cvb, bias, l)
        oc = _attn_c(qc, kc, vc, ckc, cvc, tq_c // 2, 2 * tq_c)
        x = _outproj_mlp(x, oa, ob, oc, l, None, mods, woa, wob, woc, g2, w1, w2, gf, 2 * tm_x, last)
        if not last:
            coa, cob, coc = _attn_ctx(cs, sinks, l)
            ctx = _outproj_mlp(ctx, coa, cob, coc, l, b, mods, woa, wob, woc, g2, w1, w2, gf, tm_c, False)
    return x
```

```python
import functools
import math

import jax
import jax.numpy as jnp
import numpy as np
from jax import lax
from jax.experimental import pallas as pl
from jax.experimental.pallas import tpu as pltpu

F32 = jnp.float32
BF16 = jnp.bfloat16

GRID_W = 64
HEAD_DIM = 64
A_HEADS = 8
A_KV_HEADS = 2
A_GROUP = A_HEADS // A_KV_HEADS
A_WINDOW = 128
B_HEADS = 4
NA_ROWS = 8
NA_COLS = 16
C_HEADS = 4
C_Q_RANK = 256
C_KV_RANK = 128
C_NOPE = 64
C_ROPE = 32
C_V = 64
N_MOD = 6
ROPE_BASE = 10000.0
EPS = 1e-6
NEG_INF = -1e30
LOG2E = math.log2(math.e)

LANES = 128
MOD_ROWS = 16
IN_COLS = 2048
VMEM_LIMIT = 56 * 1024 * 1024

_OFF_QA, _OFF_KA, _OFF_VA = 0, 512, 640
_OFF_QB, _OFF_KB, _OFF_VB = 768, 1024, 1280
_OFF_CQ, _OFF_CKV, _OFF_KR = 1536, 1792, 1920


def _dot(a, b):
    return jnp.dot(a, b, preferred_element_type=F32)


def _dot_t(a, b):
    return lax.dot_general(a, b, (((1,), (1,)), ((), ())), preferred_element_type=F32)


def _params(*sem):
    return pltpu.CompilerParams(dimension_semantics=sem, vmem_limit_bytes=VMEM_LIMIT)


def _mod_kernel(c_ref, w_ref, b_ref, o_ref):
    c = c_ref[...]
    s = c * (1.0 / (1.0 + jnp.exp(-c)))
    o_ref[0] = _dot(s.astype(BF16), w_ref[0].astype(BF16)) + b_ref[0]


def _modulation(cvec, w_ada, b_ada):
    depth, d, n = w_ada.shape
    tn = 1024
    return pl.pallas_call(
        _mod_kernel,
        grid=(depth, n // tn),
        in_specs=[
            pl.BlockSpec((MOD_ROWS, d), lambda l, j: (0, 0)),
            pl.BlockSpec((1, d, tn), lambda l, j: (l, 0, j)),
            pl.BlockSpec((1, 1, tn), lambda l, j: (l, 0, j)),
        ],
        out_specs=pl.BlockSpec((1, MOD_ROWS, tn), lambda l, j: (l, 0, j)),
        out_shape=jax.ShapeDtypeStruct((depth, MOD_ROWS, n), F32),
        compiler_params=_params("arbitrary", "arbitrary"),
        name="adaln_modulation",
    )(cvec, w_ada, b_ada.reshape(depth, 1, n))


def _rope(x, c, s1, s2, shift):
    return x * c + pltpu.roll(x, LANES - shift, 1) * s1 + pltpu.roll(x, shift, 1) * s2


def _rms(x, g):
    return x * lax.rsqrt(jnp.mean(x * x, axis=-1, keepdims=True) + EPS) * g


def _inproj_kernel(x_ref, mod_ref, g1_ref, win_ref, gq_ref, wuq_ref, gkv_ref, wukv_ref,
                   ac_ref, as1_ref, as2_ref, cc_ref, cs1_ref, cs2_ref,
                   qa_ref, ka_ref, va_ref, qb_ref, kb_ref, vb_ref, qc_ref, kc_ref, vc_ref, *, sub):
    tm = x_ref.shape[1]
    mod = mod_ref[0]
    scale_ab = HEAD_DIM ** -0.5 * LOG2E
    scale_c = (C_NOPE + C_ROPE) ** -0.5 * LOG2E
    ones_lane = (lax.broadcasted_iota(jnp.int32, (1, LANES), 1) == C_V).astype(F32)

    subs = [slice(r0, r0 + sub) for r0 in range(0, tm, sub)]

    def up_project(pc):
        cq = _rms(pc[:, 0:C_Q_RANK], gq_ref[...]).astype(BF16)
        ckv = _rms(pc[:, C_Q_RANK:C_Q_RANK + C_KV_RANK], gkv_ref[...]).astype(BF16)
        return _dot(cq, wuq_ref[...]), _dot(ckv, wukv_ref[...])

    pabs, pcs, ups = [], [], []
    for rs in subs:
        h = (_rms(x_ref[0, rs, :], g1_ref[...]) * (1.0 + mod[1:2]) + mod[0:1]).astype(BF16)
        pcs.append(_dot(h, win_ref[:, _OFF_CQ:IN_COLS]))
        pabs.append(_dot(h, win_ref[:, 0:_OFF_CQ]))
        ups.append(up_project(pcs[-1]))

    for rs, p, pc, (q_up, kv_up) in zip(subs, pabs, pcs, ups):
        ac, as1, as2 = ac_ref[rs, :], as1_ref[rs, :], as2_ref[rs, :]
        cc, cs1, cs2 = cc_ref[rs, :], cs1_ref[rs, :], cs2_ref[rs, :]
        for j in range(4):
            blk = p[:, _OFF_QA + LANES * j:_OFF_QA + LANES * (j + 1)]
            qa_ref[0, rs, LANES * j:LANES * (j + 1)] = (_rope(blk, ac, as1, as2, 16) * scale_ab).astype(BF16)
        ka_ref[0, rs, :] = _rope(p[:, _OFF_KA:_OFF_KA + LANES], ac, as1, as2, 16).astype(BF16)
        va = p[:, _OFF_VA:_OFF_VA + LANES]
        lane = lax.broadcasted_iota(jnp.int32, va.shape, 1)
        va_ref[0, rs, 0:LANES] = jnp.where(lane < HEAD_DIM, va, jnp.where(lane == HEAD_DIM, 1.0, 0.0)).astype(BF16)
        va_ref[0, rs, LANES:2 * LANES] = jnp.where(lane >= HEAD_DIM, va, jnp.where(lane == 0, 1.0, 0.0)).astype(BF16)
        qb_ref[0, rs, :] = (p[:, _OFF_QB:_OFF_QB + 256] * scale_ab).astype(BF16)
        kb_ref[0, rs, :] = p[:, _OFF_KB:_OFF_KB + 256].astype(BF16)
        vb_ref[0, rs, :] = p[:, _OFF_VB:_OFF_VB + 256].astype(BF16)
        kr = _rope(pc[:, _OFF_KR - _OFF_CQ:], cc, cs1, cs2, 8)
        lo = lane < C_NOPE
        for hd in range(C_HEADS):
            sl = slice(LANES * hd, LANES * (hd + 1))
            qc_ref[0, rs, sl] = (_rope(q_up[:, sl], cc, cs1, cs2, 8) * scale_c).astype(BF16)
            kv = kv_up[:, sl]
            kc_ref[0, rs, sl] = jnp.where(lo, kv, kr).astype(BF16)
            vc_ref[0, rs, sl] = jnp.where(lo, pltpu.roll(kv, C_V, 1), ones_lane).astype(BF16)


def _inproj(x, layer, mod_row, mods, g1, win, gq, wuq, gkv, wukv, tabs, tm):
    b, s, d = x.shape
    nt = s // tm

    def xmap(j, bb):
        return (bb, j, 0)

    def modmap(j, bb):
        return (layer, (bb if mod_row is None else mod_row), 0, 0)

    def wmap(j, bb):
        return (layer, 0, 0)

    def tabmap(j, bb):
        return (j, 0)

    widths = (512, 128, 256, 256, 256, 256, 512, 512, 512)
    in_specs = [
        pl.BlockSpec((1, tm, d), xmap),
        pl.BlockSpec((None, 1, N_MOD, d), modmap),
        pl.BlockSpec((None, 1, d), wmap),
        pl.BlockSpec((None, d, IN_COLS), wmap),
        pl.BlockSpec((None, 1, C_Q_RANK), wmap),
        pl.BlockSpec((None, C_Q_RANK, 512), wmap),
        pl.BlockSpec((None, 1, C_KV_RANK), wmap),
        pl.BlockSpec((None, C_KV_RANK, 512), wmap),
    ] + [pl.BlockSpec((tm, LANES), tabmap) for _ in range(6)]
    return pl.pallas_call(
        functools.partial(_inproj_kernel, sub=min(tm, 256)),
        grid=(nt, b),
        in_specs=in_specs,
        out_specs=[pl.BlockSpec((1, tm, w), xmap) for w in widths],
        out_shape=[jax.ShapeDtypeStruct((b, s, w), BF16) for w in widths],
        compiler_params=_params("arbitrary", "arbitrary"),
        name="norm_inproj_rope",
    )(x, mods, g1, win, gq, wuq, gkv, wukv, *tabs)


def _lane_lo(shape):
    return lax.broadcasted_iota(jnp.int32, shape, 1) < HEAD_DIM


def _split_pair(q2):
    lo = _lane_lo(q2.shape)
    zero = jnp.zeros_like(q2)
    return jnp.concatenate([jnp.where(lo, q2, zero), jnp.where(lo, zero, q2)], axis=0)


def _merge_pair(o):
    n = o.shape[0] // 2
    return jnp.where(_lane_lo((n, o.shape[1])), o[:n], o[n:])


def _softmax_pv(parts, sink=None):
    m = parts[0][0].max(axis=-1, keepdims=True)
    for s, _ in parts[1:]:
        m = jnp.maximum(m, s.max(axis=-1, keepdims=True))
    if sink is not None:
        m = jnp.maximum(m, sink)
    denom = None
    acc = None
    for s, v in parts:
        e = jnp.exp2(s - m)
        r = e.sum(axis=-1, keepdims=True)
        denom = r if denom is None else denom + r
        o = _dot(e.astype(BF16), v)
        acc = o if acc is None else acc + o
    if sink is not None:
        denom = denom + jnp.exp2(sink - m)
    return acc / denom


def _attn_a_kernel(q_ref, k_ref, v_ref, kx_ref, vx_ref, sink_ref, o_ref, *, tq, seq, ahead):
    t = pl.program_id(1)
    nsub = q_ref.shape[1] // tq
    nloc = tq + 2 * A_WINDOW
    kx = kx_ref[0]
    lo = _lane_lo((tq, LANES))
    iota_rel = (lax.broadcasted_iota(jnp.int32, (tq, nloc), 1) - lax.broadcasted_iota(jnp.int32, (tq, nloc), 0))
    starts, bands = [], []
    for sb in range(nsub):
        qstart = (t * nsub + sb) * tq
        start = pl.multiple_of(jnp.clip(qstart - A_WINDOW, 0, seq - nloc), A_WINDOW)
        rel = iota_rel + (start - qstart)
        starts.append(start)
        bands.append(jnp.where(jnp.abs(rel) <= A_WINDOW, 0.0, NEG_INF))
    units = [(sb, j, g) for sb in range(nsub) for j in range(A_GROUP) for g in range(A_KV_HEADS)]

    def scores(u):
        sb, j, g = units[u]
        q2 = q_ref[0, sb * tq:(sb + 1) * tq, LANES * j:LANES * (j + 1)]
        zero = jnp.zeros_like(q2)
        q = jnp.where(lo, q2, zero) if g == 0 else jnp.where(lo, zero, q2)
        return _dot_t(q, k_ref[0, pl.ds(starts[sb], nloc), :]) + bands[sb], _dot_t(q, kx)

    outs = {}
    queue = [scores(u) for u in range(min(ahead, len(units)))]
    for u, (sb, j, g) in enumerate(units):
        s_loc, s_ctx = queue.pop(0)
        if u + ahead < len(units):
            queue.append(scores(u + ahead))
        vsg = v_ref[0, pl.ds(starts[sb], nloc), LANES * g:LANES * (g + 1)]
        vxg = vx_ref[0, :, LANES * g:LANES * (g + 1)]
        hd = g * A_GROUP + j
        sink = jnp.broadcast_to(sink_ref[hd:hd + 1, 0:1], (tq, 1))
        m = jnp.maximum(jnp.maximum(s_loc.max(axis=-1, keepdims=True), s_ctx.max(axis=-1, keepdims=True)), sink)
        o = _dot(jnp.exp2(s_loc - m).astype(BF16), vsg) + _dot(jnp.exp2(s_ctx - m).astype(BF16), vxg)
        denom = (o[:, HEAD_DIM:HEAD_DIM + 1] if g == 0 else o[:, 0:1]) + jnp.exp2(sink - m)
        outs[g] = o / denom
        if g == A_KV_HEADS - 1:
            o_ref[0, sb * tq:(sb + 1) * tq, LANES * j:LANES * (j + 1)] = jnp.where(lo, outs[0], outs[1]).astype(BF16)


def _attn_a(qa, ka, va, kax, vax, sinks, layer, tq, tstep):
    b, s, _ = qa.shape
    c = kax.shape[1]

    def qmap(bb, t):
        return (bb, t, 0)

    def kmap(bb, t):
        return (bb, 0, 0)

    return pl.pallas_call(
        functools.partial(_attn_a_kernel, tq=tq, seq=s, ahead=2),
        grid=(b, s // tstep),
        in_specs=[
            pl.BlockSpec((1, tstep, 512), qmap),
            pl.BlockSpec((1, s, LANES), kmap),
            pl.BlockSpec((1, s, 2 * LANES), kmap),
            pl.BlockSpec((1, c, LANES), kmap),
            pl.BlockSpec((1, c, 2 * LANES), kmap),
            pl.BlockSpec((None, A_HEADS, LANES), lambda bb, t: (layer, 0, 0)),
        ],
        out_specs=pl.BlockSpec((1, tstep, 512), qmap),
        out_shape=jax.ShapeDtypeStruct((b, s, 512), BF16),
        compiler_params=_params("arbitrary", "arbitrary"),
        name="attn_window_gqa",
    )(qa, ka, va, kax, vax, sinks)


def _attn_b_kernel(q_ref, k_ref, v_ref, kx_ref, vx_ref, bias_ref, o_ref, *, rows, ahead):
    pair, rb = pl.program_id(1), pl.program_id(2)
    kx, vx = kx_ref[0], vx_ref[0]
    nrow = q_ref.shape[1] // GRID_W

    def scores(r):
        qr = rb * nrow + r
        wr = jnp.clip(qr - NA_ROWS // 2, 0, rows - NA_ROWS)
        oi = wr - qr + (NA_ROWS - 1)
        start = pl.multiple_of(wr * GRID_W, GRID_W)
        q = _split_pair(q_ref[0, r * GRID_W:(r + 1) * GRID_W, :])
        ks = k_ref[0, pl.ds(start, NA_ROWS * GRID_W), :]
        bias = jnp.concatenate(
            [jnp.concatenate([bias_ref[2 * pair + hh, oi + 2 * t] for t in range(NA_ROWS // 2)], axis=1)
             for hh in range(2)], axis=0)
        return _dot_t(q, ks) + bias, _dot_t(q, kx), start

    queue = [scores(r) for r in range(min(ahead, nrow))]
    for r in range(nrow):
        s_loc, s_ctx, start = queue.pop(0)
        if r + ahead < nrow:
            queue.append(scores(r + ahead))
        vs = v_ref[0, pl.ds(start, NA_ROWS * GRID_W), :]
        o = _softmax_pv([(s_loc, vs), (s_ctx, vx)])
        o_ref[0, r * GRID_W:(r + 1) * GRID_W, :] = _merge_pair(o).astype(BF16)


def _attn_b(qb, kb, vb, kbx, vbx, bias, layer):
    b, s, _ = qb.shape
    c = kbx.shape[1]
    rows = s // GRID_W
    tq = min(4 * NA_ROWS * GRID_W, s)

    def qmap(bb, p, rb):
        return (bb, rb, p)

    def kmap(bb, p, rb):
        return (bb, 0, p)

    return pl.pallas_call(
        functools.partial(_attn_b_kernel, rows=rows, ahead=3),
        grid=(b, B_HEADS // 2, s // tq),
        in_specs=[
            pl.BlockSpec((1, tq, LANES), qmap),
            pl.BlockSpec((1, s, LANES), kmap),
            pl.BlockSpec((1, s, LANES), kmap),
            pl.BlockSpec((1, c, LANES), kmap),
            pl.BlockSpec((1, c, LANES), kmap),
            pl.BlockSpec((None, B_HEADS, 2 * NA_ROWS - 2, GRID_W, 2 * GRID_W),
                         lambda bb, p, rb: (layer, 0, 0, 0, 0)),
        ],
        out_specs=pl.BlockSpec((1, tq, LANES), qmap),
        out_shape=jax.ShapeDtypeStruct((b, s, 256), BF16),
        compiler_params=_params("arbitrary", "arbitrary", "arbitrary"),
        name="attn_neighbourhood",
    )(qb, kb, vb, kbx, vbx, bias)


def _na_bias_table(rpb):
    depth, h = rpb.shape[:2]
    qc = jnp.arange(GRID_W)[:, None]
    kc = jnp.arange(GRID_W)[None, :]
    wc = jnp.clip(qc - NA_COLS // 2, 0, GRID_W - NA_COLS)
    valid = (kc >= wc) & (kc < wc + NA_COLS)
    pad = GRID_W - NA_COLS
    rp = jnp.pad(rpb, ((0, 0), (0, 0), (0, 0), (pad, pad)))
    toe = jnp.stack([rp[..., GRID_W - 1 - q:2 * GRID_W - 1 - q] for q in range(GRID_W)], axis=3)
    toe = jnp.where(valid, toe * LOG2E, NEG_INF)
    return jnp.concatenate([toe[:, :, :-1], toe[:, :, 1:]], axis=-1).astype(F32)


def _attn_c_kernel(q_ref, k_ref, kx_ref, v_ref, vx_ref, o_ref, *, tq, kc, ahead):
    s = k_ref.shape[1]
    nsub = q_ref.shape[1] // tq
    chunks = [(k_ref, v_ref, c0, kc) for c0 in range(0, s, kc)] + [(kx_ref, vx_ref, 0, kx_ref.shape[1])]
    units = [(sb, ci) for sb in range(nsub) for ci in range(len(chunks))]

    def scores(u):
        sb, ci = units[u]
        kr, _, c0, n = chunks[ci]
        return [_dot_t(q_ref[0, sb * tq:(sb + 1) * tq, LANES * hh:LANES * (hh + 1)],
                       kr[0, c0:c0 + n, LANES * hh:LANES * (hh + 1)]) for hh in range(2)]

    m = [None, None]
    acc = [None, None]
    queue = [scores(u) for u in range(min(ahead, len(units)))]
    for u, (sb, ci) in enumerate(units):
        cur = queue.pop(0)
        if u + ahead < len(units):
            queue.append(scores(u + ahead))
        _, vr, c0, n = chunks[ci]
        for hh in range(2):
            sl = slice(LANES * hh, LANES * (hh + 1))
            sc = cur[hh]
            cm = sc.max(axis=-1, keepdims=True)
            if ci == 0:
                m[hh] = cm
                acc[hh] = _dot(jnp.exp2(sc - cm).astype(BF16), vr[0, c0:c0 + n, sl])
            else:
                m_new = jnp.maximum(m[hh], cm)
                acc[hh] = (jnp.exp2(m[hh] - m_new) * acc[hh]
                           + _dot(jnp.exp2(sc - m_new).astype(BF16), vr[0, c0:c0 + n, sl]))
                m[hh] = m_new
        if ci == len(chunks) - 1:
            o0 = acc[0] / acc[0][:, C_V:C_V + 1]
            o1 = acc[1] / acc[1][:, C_V:C_V + 1]
            o_ref[0, sb * tq:(sb + 1) * tq, :] = jnp.where(
                _lane_lo(o0.shape), o0, pltpu.roll(o1, C_V, 1)).astype(BF16)


def _attn_c(qc, kc, vc, kcx, vcx, tq, tstep):
    b, s, _ = qc.shape
    c = kcx.shape[1]

    def qmap(bb, p, i):
        return (bb, i, p)

    def kmap(bb, p, i):
        return (bb, 0, p)

    return pl.pallas_call(
        functools.partial(_attn_c_kernel, tq=tq, kc=512, ahead=1),
        grid=(b, C_HEADS // 2, s // tstep),
        in_specs=[
            pl.BlockSpec((1, tstep, 2 * LANES), qmap),
            pl.BlockSpec((1, s, 2 * LANES), kmap),
            pl.BlockSpec((1, c, 2 * LANES), kmap),
            pl.BlockSpec((1, s, 2 * LANES), kmap),
            pl.BlockSpec((1, c, 2 * LANES), kmap),
        ],
        out_specs=pl.BlockSpec((1, tstep, LANES), qmap),
        out_shape=jax.ShapeDtypeStruct((b, s, 256), BF16),
        compiler_params=_params("arbitrary", "arbitrary", "arbitrary"),
        name="attn_latent",
    )(qc, kc, kcx, vc, vcx)


def _attn_ctx_kernel(qa_ref, ka_ref, va_ref, qb_ref, kb_ref, vb_ref, qc_ref, kc_ref, vc_ref,
                     sink_ref, oa_ref, ob_ref, oc_ref):
    c = qa_ref.shape[1]
    half = A_HEADS // 2
    lo = _lane_lo((c, LANES))

    def blk(i):
        return slice(LANES * i, LANES * (i + 1))

    units = []
    for j in range(half):
        for g in range(A_KV_HEADS):
            def score(j=j, g=g):
                q2 = qa_ref[0, :, blk(j)]
                zero = jnp.zeros_like(q2)
                return _dot_t(jnp.where(lo, q2, zero) if g == 0 else jnp.where(lo, zero, q2), ka_ref[0])

            def fin_a(o, j=j):
                oa_ref[0, :, blk(j)] = jnp.where(lo, o[0], o[1]).astype(BF16)

            units.append((score, lambda g=g: va_ref[0, :, blk(g)], j + half * g, fin_a if g == 1 else None))
    for p in range(B_HEADS // 2):
        def score(p=p):
            return _dot_t(_split_pair(qb_ref[0, :, blk(p)]), kb_ref[0, :, blk(p)])

        def fin_b(o, p=p):
            ob_ref[0, :, blk(p)] = _merge_pair(o[0]).astype(BF16)

        units.append((score, lambda p=p: vb_ref[0, :, blk(p)], None, fin_b))
    for hd in range(C_HEADS):
        def score(hd=hd):
            return _dot_t(qc_ref[0, :, blk(hd)], kc_ref[0, :, blk(hd)])

        def fin_c(o, hd=hd):
            oc_ref[0, :, blk(hd // 2)] = jnp.where(lo, o[0], pltpu.roll(o[1], C_V, 1)).astype(BF16)

        units.append((score, lambda hd=hd: vc_ref[0, :, blk(hd)], None, fin_c if hd % 2 == 1 else None))

    pending = []
    nxt = units[0][0]()
    for u, (_, value, sink_head, finish) in enumerate(units):
        s = nxt
        if u + 1 < len(units):
            nxt = units[u + 1][0]()
        sink = None
        if sink_head is not None:
            sink = jnp.broadcast_to(sink_ref[sink_head:sink_head + 1, 0:1], (s.shape[0], 1))
        pending.append(_softmax_pv([(s, value())], sink))
        if finish is not None:
            finish(pending)
            pending = []


def _attn_ctx(parts, sinks, layer):
    b, c, _ = parts[0].shape
    widths = [a.shape[2] for a in parts]

    def bmap(bb):
        return (bb, 0, 0)

    return pl.pallas_call(
        _attn_ctx_kernel,
        grid=(b,),
        in_specs=[pl.BlockSpec((1, c, w), bmap) for w in widths]
        + [pl.BlockSpec((None, A_HEADS, LANES), lambda bb: (layer, 0, 0))],
        out_specs=[pl.BlockSpec((1, c, w), bmap) for w in (512, 256, 256)],
        out_shape=[jax.ShapeDtypeStruct((b, c, w), BF16) for w in (512, 256, 256)],
        compiler_params=_params("arbitrary"),
        name="attn_context",
    )(*parts, sinks)


def _outproj_mlp_kernel(x_ref, ma_ref, mb_ref, mc_ref, mod_ref, woa_ref, wob_ref, woc_ref,
                        g2_ref, w1_ref, w2_ref, gf_ref, o_ref, *, final, ff_chunk):
    mod = mod_ref[0]
    attn = _dot(ma_ref[0], woa_ref[...]) + _dot(mb_ref[0], wob_ref[...]) + _dot(mc_ref[0], woc_ref[...])
    x1 = x_ref[0] + mod[2:3] * attn
    h = (_rms(x1, g2_ref[...]) * (1.0 + mod[4:5]) + mod[3:4]).astype(BF16)
    d_ff = w1_ref.shape[1]
    def up(c0):
        return _dot(h, w1_ref[:, c0:c0 + ff_chunk])

    y = None
    chunk_starts = list(range(0, d_ff, ff_chunk))
    nxt = up(chunk_starts[0])
    for i, c0 in enumerate(chunk_starts):
        u = jnp.maximum(nxt, 0.0)
        if i + 1 < len(chunk_starts):
            nxt = up(chunk_starts[i + 1])
        part = _dot((u * u).astype(BF16), w2_ref[c0:c0 + ff_chunk, :])
        y = part if y is None else y + part
    x2 = x1 + mod[5:6] * y
    if final:
        x2 = _rms(x2, gf_ref[...])
    o_ref[0] = x2


def _outproj_mlp(x, ma, mb, mc, layer, mod_row, mods, woa, wob, woc, g2, w1, w2, gf, tm, final):
    b, s, d = x.shape
    d_ff = w1.shape[2]

    def xmap(bb, j):
        return (bb, j, 0)

    def modmap(bb, j):
        return (layer, (bb if mod_row is None else mod_row), 0, 0)

    def resident(shape, row_block=0):
        return pl.BlockSpec((None,) + shape, lambda bb, j: (layer, row_block, 0), pipeline_mode=pl.Buffered(1))

    return pl.pallas_call(
        functools.partial(_outproj_mlp_kernel, final=final, ff_chunk=1024),
        grid=(b, s // tm),
        in_specs=[
            pl.BlockSpec((1, tm, d), xmap),
            pl.BlockSpec((1, tm, 512), xmap),
            pl.BlockSpec((1, tm, 256), xmap),
            pl.BlockSpec((1, tm, 256), xmap),
            pl.BlockSpec((None, 1, N_MOD, d), modmap),
            resident((512, d)), resident((256, d), 2), resident((256, d), 3),
            resident((1, d)),
            resident((d, d_ff)), resident((d_ff, d)),
            pl.BlockSpec((1, d), lambda bb, j: (0, 0)),
        ],
        out_specs=pl.BlockSpec((1, tm, d), xmap),
        out_shape=jax.ShapeDtypeStruct((b, s, d), F32),
        compiler_params=_params("arbitrary", "arbitrary"),
        name="outproj_mlp",
    )(x, ma, mb, mc, mods, woa, wob, woc, g2, w1, w2, gf)


def _rope_tables(s):
    f32 = np.float32
    tok = np.arange(s)
    row, col = (tok // GRID_W).astype(f32), (tok % GRID_W).astype(f32)

    def cs(pos, half):
        freqs = (f32(ROPE_BASE) ** (-np.arange(half, dtype=f32) / f32(half))).astype(f32)
        ang = (pos[:, None] * freqs).astype(f32)
        return np.cos(ang).astype(f32), np.sin(ang).astype(f32)

    cr, sr = cs(row, 16)
    cc, sc = cs(col, 16)
    z = np.zeros_like(sr)
    a_c = np.tile(np.concatenate([cr, cr, cc, cc], axis=1), (1, 2))
    a_s1 = np.tile(np.concatenate([-sr, z, -sc, z], axis=1), (1, 2))
    a_s2 = np.tile(np.concatenate([z, sr, z, sc], axis=1), (1, 2))
    cr, sr = cs(row, 8)
    cc, sc = cs(col, 8)
    z = np.zeros_like(sr)
    one64, zero64 = np.ones((s, C_NOPE), f32), np.zeros((s, C_NOPE), f32)
    one32, zero32 = np.ones((s, 32), f32), np.zeros((s, 32), f32)
    c_c = np.concatenate([one64, cr, cr, cc, cc, one32], axis=1)
    c_s1 = np.concatenate([zero64, -sr, z, -sc, z, zero32], axis=1)
    c_s2 = np.concatenate([zero64, z, sr, z, sc, zero32], axis=1)
    return tuple(jnp.asarray(t) for t in (a_c, a_s1, a_s2, c_c, c_s1, c_s2))


def _identity_tables(s):
    one, zero = jnp.asarray(np.ones((s, LANES), np.float32)), jnp.asarray(np.zeros((s, LANES), np.float32))
    return (one, zero, zero, one, zero, zero)


def _pad_heads(w, offs, width):
    pad = jnp.zeros(w.shape[:2] + (LANES - width,), w.dtype)
    return jnp.concatenate([jnp.concatenate([w[:, :, o:o + width], pad], axis=2) for o in offs], axis=2)


def _weight_layouts(w_in, w_uq, w_ukv, w_out):
    depth, d, _ = w_in.shape
    w_in, w_out = w_in.astype(BF16), w_out.astype(BF16)
    qa_w = w_in[:, :, :512].reshape(depth, d, A_KV_HEADS, A_GROUP, HEAD_DIM).transpose(0, 1, 3, 2, 4)
    qa_w = qa_w.reshape(depth, d, 512)
    z = lambda n: jnp.zeros((depth, d, n), w_in.dtype)
    win = jnp.concatenate([qa_w, w_in[:, :, 512:1920], z(64), w_in[:, :, 1920:1952], z(32)], axis=2)
    hq = C_NOPE + C_ROPE
    wuq = _pad_heads(w_uq, [hq * h for h in range(C_HEADS)], hq).astype(BF16)
    wukv = w_ukv.astype(BF16)
    woa = w_out[:, :512].reshape(depth, A_KV_HEADS, A_GROUP, HEAD_DIM, d).transpose(0, 2, 1, 3, 4)
    woa = woa.reshape(depth, 512, d)
    wob = woc = w_out
    return win, wuq, wukv, woa, wob, woc


def kernel(x, c, ctx, c_ctx, w_ada, b_ada, norm1_g, norm2_g, w_in, attn_sink, na_rpb, mla_q_norm_g,
           mla_w_uq, mla_kv_norm_g, mla_w_ukv, w_out, w_mlp_in, w_mlp_out, final_norm_g):
    b, s, d = x.shape
    n_ctx = ctx.shape[1]
    depth = w_ada.shape[0]
    assert b + 1 <= MOD_ROWS and s % 1024 == 0 and n_ctx % 128 == 0
    tm_x = 512
    tm_c = min(n_ctx, 256)
    tq_a = 256
    tq_c = 512

    cvec = jnp.concatenate([c, c_ctx[None], jnp.zeros((MOD_ROWS - b - 1, d), c.dtype)], axis=0)
    mods = _modulation(cvec, w_ada, b_ada).reshape(depth, MOD_ROWS, N_MOD, d)
    tabs_x = _rope_tables(s)
    tabs_c = _identity_tables(n_ctx)
    gf = final_norm_g.reshape(1, d)
    win, wuq, wukv, woa, wob, woc = _weight_layouts(w_in, mla_w_uq, mla_w_ukv, w_out)
    w1, w2 = w_mlp_in.astype(BF16), w_mlp_out.astype(BF16)
    sinks = jnp.broadcast_to((attn_sink * LOG2E)[:, :, None], (depth, A_HEADS, LANES)).astype(F32)
    bias = _na_bias_table(na_rpb)
    g1, g2 = norm1_g.reshape(depth, 1, d), norm2_g.reshape(depth, 1, d)
    gq, gkv = mla_q_norm_g.reshape(depth, 1, -1), mla_kv_norm_g.reshape(depth, 1, -1)

    for l in range(depth):
        last = l == depth - 1
        xs = _inproj(x, l, None, mods, g1, win, gq, wuq, gkv, wukv, tabs_x, 2 * tm_x)
        cs = _inproj(ctx, l, b, mods, g1, win, gq, wuq, gkv, wukv, tabs_c, tm_c)
        qa, ka, va, qb, kb, vb, qc, kc, vc = xs
        cqa, cka, cva, cqb, ckb, cvb, cqc, ckc, cvc = cs

        oa = _attn_a(qa, ka, va, cka, cva, sinks, l, tq_a, 4 * tq_a)
        ob = _attn_b(qb, kb, vb, ckb, cvb, bias, l)
        oc = _attn_c(qc, kc, vc, ckc, cvc, tq_c // 2, 2 * tq_c)
        x = _outproj_mlp(x, oa, ob, oc, l, None, mods, woa, wob, woc, g2, w1, w2, gf, 2 * tm_x, last)
        if not last:
            coa, cob, coc = _attn_ctx(cs, sinks, l)
            ctx = _outproj_mlp(ctx, coa, cob, coc, l, b, mods, woa, wob, woc, g2, w1, w2, gf, tm_c, False)
    return x
```

```python
import functools
import math

import jax
import jax.numpy as jnp
import numpy as np
from jax import lax
from jax.experimental import pallas as pl
from jax.experimental.pallas import tpu as pltpu

F32 = jnp.float32
BF16 = jnp.bfloat16

GRID_W = 64
HEAD_DIM = 64
A_HEADS = 8
A_KV_HEADS = 2
A_GROUP = A_HEADS // A_KV_HEADS
A_WINDOW = 128
B_HEADS = 4
NA_ROWS = 8
NA_COLS = 16
C_HEADS = 4
C_Q_RANK = 256
C_KV_RANK = 128
C_NOPE = 64
C_ROPE = 32
C_V = 64
N_MOD = 6
ROPE_BASE = 10000.0
EPS = 1e-6
NEG_INF = -1e30
LOG2E = math.log2(math.e)

LANES = 128
MOD_ROWS = 16
IN_COLS = 2048
VMEM_LIMIT = 56 * 1024 * 1024

_OFF_QA, _OFF_KA, _OFF_VA = 0, 512, 640
_OFF_QB, _OFF_KB, _OFF_VB = 768, 1024, 1280
_OFF_CQ, _OFF_CKV, _OFF_KR = 1536, 1792, 1920


def _dot(a, b):
    return jnp.dot(a, b, preferred_element_type=F32)


def _dot_t(a, b):
    return lax.dot_general(a, b, (((1,), (1,)), ((), ())), preferred_element_type=F32)


def _params(*sem):
    return pltpu.CompilerParams(dimension_semantics=sem, vmem_limit_bytes=VMEM_LIMIT)


def _mod_kernel(c_ref, w_ref, b_ref, o_ref):
    c = c_ref[...]
    s = c * (1.0 / (1.0 + jnp.exp(-c)))
    o_ref[0] = _dot(s.astype(BF16), w_ref[0].astype(BF16)) + b_ref[0]


def _modulation(cvec, w_ada, b_ada):
    depth, d, n = w_ada.shape
    tn = 1024
    return pl.pallas_call(
        _mod_kernel,
        grid=(depth, n // tn),
        in_specs=[
            pl.BlockSpec((MOD_ROWS, d), lambda l, j: (0, 0)),
            pl.BlockSpec((1, d, tn), lambda l, j: (l, 0, j)),
            pl.BlockSpec((1, 1, tn), lambda l, j: (l, 0, j)),
        ],
        out_specs=pl.BlockSpec((1, MOD_ROWS, tn), lambda l, j: (l, 0, j)),
        out_shape=jax.ShapeDtypeStruct((depth, MOD_ROWS, n), F32),
        compiler_params=_params("arbitrary", "arbitrary"),
        name="adaln_modulation",
    )(cvec, w_ada, b_ada.reshape(depth, 1, n))


def _rope(x, c, s1, s2, shift):
    return x * c + pltpu.roll(x, LANES - shift, 1) * s1 + pltpu.roll(x, shift, 1) * s2


def _rms(x, g):
    return x * lax.rsqrt(jnp.mean(x * x, axis=-1, keepdims=True) + EPS) * g


def _inproj_kernel(x_ref, mod_ref, g1_ref, win_ref, gq_ref, wuq_ref, gkv_ref, wukv_ref,
                   ac_ref, as1_ref, as2_ref, cc_ref, cs1_ref, cs2_ref,
                   qa_ref, ka_ref, va_ref, qb_ref, kb_ref, vb_ref, qc_ref, kc_ref, vc_ref, *, sub):
    tm = x_ref.shape[1]
    mod = mod_ref[0]
    scale_ab = HEAD_DIM ** -0.5 * LOG2E
    scale_c = (C_NOPE + C_ROPE) ** -0.5 * LOG2E
    ones_lane = (lax.broadcasted_iota(jnp.int32, (1, LANES), 1) == C_V).astype(F32)

    subs = [slice(r0, r0 + sub) for r0 in range(0, tm, sub)]

    def up_project(pc):
        cq = _rms(pc[:, 0:C_Q_RANK], gq_ref[...]).astype(BF16)
        ckv = _rms(pc[:, C_Q_RANK:C_Q_RANK + C_KV_RANK], gkv_ref[...]).astype(BF16)
        return _dot(cq, wuq_ref[...]), _dot(ckv, wukv_ref[...])

    pabs, pcs, ups = [], [], []
    for rs in subs:
        h = (_rms(x_ref[0, rs, :], g1_ref[...]) * (1.0 + mod[1:2]) + mod[0:1]).astype(BF16)
        pcs.append(_dot(h, win_ref[:, _OFF_CQ:IN_COLS]))
        pabs.append(_dot(h, win_ref[:, 0:_OFF_CQ]))
        ups.append(up_project(pcs[-1]))

    for rs, p, pc, (q_up, kv_up) in zip(subs, pabs, pcs, ups):
        ac, as1, as2 = ac_ref[rs, :], as1_ref[rs, :], as2_ref[rs, :]
        cc, cs1, cs2 = cc_ref[rs, :], cs1_ref[rs, :], cs2_ref[rs, :]
        for j in range(4):
            blk = p[:, _OFF_QA + LANES * j:_OFF_QA + LANES * (j + 1)]
            qa_ref[0, rs, LANES * j:LANES * (j + 1)] = (_rope(blk, ac, as1, as2, 16) * scale_ab).astype(BF16)
        ka_ref[0, rs, :] = _rope(p[:, _OFF_KA:_OFF_KA + LANES], ac, as1, as2, 16).astype(BF16)
        va = p[:, _OFF_VA:_OFF_VA + LANES]
        lane = lax.broadcasted_iota(jnp.int32, va.shape, 1)
        va_ref[0, rs, 0:LANES] = jnp.where(lane < HEAD_DIM, va, jnp.where(lane == HEAD_DIM, 1.0, 0.0)).astype(BF16)
        va_ref[0, rs, LANES:2 * LANES] = jnp.where(lane >= HEAD_DIM, va, jnp.where(lane == 0, 1.0, 0.0)).astype(BF16)
        qb_ref[0, rs, :] = (p[:, _OFF_QB:_OFF_QB + 256] * scale_ab).astype(BF16)
        kb_ref[0, rs, :] = p[:, _OFF_KB:_OFF_KB + 256].astype(BF16)
        vb_ref[0, rs, :] = p[:, _OFF_VB:_OFF_VB + 256].astype(BF16)
        kr = _rope(pc[:, _OFF_KR - _OFF_CQ:], cc, cs1, cs2, 8)
        lo = lane < C_NOPE
        for hd in range(C_HEADS):
            sl = slice(LANES * hd, LANES * (hd + 1))
            qc_ref[0, rs, sl] = (_rope(q_up[:, sl], cc, cs1, cs2, 8) * scale_c).astype(BF16)
            kv = kv_up[:, sl]
            kc_ref[0, rs, sl] = jnp.where(lo, kv, kr).astype(BF16)
            vc_ref[0, rs, sl] = jnp.where(lo, pltpu.roll(kv, C_V, 1), ones_lane).astype(BF16)


def _inproj(x, layer, mod_row, mods, g1, win, gq, wuq, gkv, wukv, tabs, tm):
    b, s, d = x.shape
    nt = s // tm

    def xmap(j, bb):
        return (bb, j, 0)

    def modmap(j, bb):
        return (layer, (bb if mod_row is None else mod_row), 0, 0)

    def wmap(j, bb):
        return (layer, 0, 0)

    def tabmap(j, bb):
        return (j, 0)

    widths = (512, 128, 256, 256, 256, 256, 512, 512, 512)
    in_specs = [
        pl.BlockSpec((1, tm, d), xmap),
        pl.BlockSpec((None, 1, N_MOD, d), modmap),
        pl.BlockSpec((None, 1, d), wmap),
        pl.BlockSpec((None, d, IN_COLS), wmap),
        pl.BlockSpec((None, 1, C_Q_RANK), wmap),
        pl.BlockSpec((None, C_Q_RANK, 512), wmap),
        pl.BlockSpec((None, 1, C_KV_RANK), wmap),
        pl.BlockSpec((None, C_KV_RANK, 512), wmap),
    ] + [pl.BlockSpec((tm, LANES), tabmap) for _ in range(6)]
    return pl.pallas_call(
        functools.partial(_inproj_kernel, sub=min(tm, 256)),
        grid=(nt, b),
        in_specs=in_specs,
        out_specs=[pl.BlockSpec((1, tm, w), xmap) for w in widths],
        out_shape=[jax.ShapeDtypeStruct((b, s, w), BF16) for w in widths],
        compiler_params=_params("arbitrary", "arbitrary"),
        name="norm_inproj_rope",
    )(x, mods, g1, win, gq, wuq, gkv, wukv, *tabs)


def _lane_lo(shape):
    return lax.broadcasted_iota(jnp.int32, shape, 1) < HEAD_DIM


def _split_pair(q2):
    lo = _lane_lo(q2.shape)
    zero = jnp.zeros_like(q2)
    return jnp.concatenate([jnp.where(lo, q2, zero), jnp.where(lo, zero, q2)], axis=0)


def _merge_pair(o):
    n = o.shape[0] // 2
    return jnp.where(_lane_lo((n, o.shape[1])), o[:n], o[n:])


def _softmax_pv(parts, sink=None):
    m = parts[0][0].max(axis=-1, keepdims=True)
    for s, _ in parts[1:]:
        m = jnp.maximum(m, s.max(axis=-1, keepdims=True))
    if sink is not None:
        m = jnp.maximum(m, sink)
    denom = None
    acc = None
    for s, v in parts:
        e = jnp.exp2(s - m)
        r = e.sum(axis=-1, keepdims=True)
        denom = r if denom is None else denom + r
        o = _dot(e.astype(BF16), v)
        acc = o if acc is None else acc + o
    if sink is not None:
        denom = denom + jnp.exp2(sink - m)
    return acc / denom


def _attn_a_kernel(q_ref, k_ref, v_ref, kx_ref, vx_ref, sink_ref, o_ref, *, tq, seq, ahead):
    t = pl.program_id(1)
    nsub = q_ref.shape[1] // tq
    nloc = tq + 2 * A_WINDOW
    kx = kx_ref[0]
    lo = _lane_lo((tq, LANES))
    iota_rel = (lax.broadcasted_iota(jnp.int32, (tq, nloc), 1) - lax.broadcasted_iota(jnp.int32, (tq, nloc), 0))
    starts, bands = [], []
    for sb in range(nsub):
        qstart = (t * nsub + sb) * tq
        start = pl.multiple_of(jnp.clip(qstart - A_WINDOW, 0, seq - nloc), A_WINDOW)
        rel = iota_rel + (start - qstart)
        starts.append(start)
        bands.append(jnp.where(jnp.abs(rel) <= A_WINDOW, 0.0, NEG_INF))
    units = [(sb, j, g) for sb in range(nsub) for j in range(A_GROUP) for g in range(A_KV_HEADS)]

    def scores(u):
        sb, j, g = units[u]
        q2 = q_ref[0, sb * tq:(sb + 1) * tq, LANES * j:LANES * (j + 1)]
        zero = jnp.zeros_like(q2)
        q = jnp.where(lo, q2, zero) if g == 0 else jnp.where(lo, zero, q2)
        return _dot_t(q, k_ref[0, pl.ds(starts[sb], nloc), :]) + bands[sb], _dot_t(q, kx)

    outs = {}
    queue = [scores(u) for u in range(min(ahead, len(units)))]
    for u, (sb, j, g) in enumerate(units):
        s_loc, s_ctx = queue.pop(0)
        if u + ahead < len(units):
            queue.append(scores(u + ahead))
        vsg = v_ref[0, pl.ds(starts[sb], nloc), LANES * g:LANES * (g + 1)]
        vxg = vx_ref[0, :, LANES * g:LANES * (g + 1)]
        hd = g * A_GROUP + j
        sink = jnp.broadcast_to(sink_ref[hd:hd + 1, 0:1], (tq, 1))
        m = jnp.maximum(jnp.maximum(s_loc.max(axis=-1, keepdims=True), s_ctx.max(axis=-1, keepdims=True)), sink)
        o = _dot(jnp.exp2(s_loc - m).astype(BF16), vsg) + _dot(jnp.exp2(s_ctx - m).astype(BF16), vxg)
        denom = (o[:, HEAD_DIM:HEAD_DIM + 1] if g == 0 else o[:, 0:1]) + jnp.exp2(sink - m)
        outs[g] = o / denom
        if g == A_KV_HEADS - 1:
            o_ref[0, sb * tq:(sb + 1) * tq, LANES * j:LANES * (j + 1)] = jnp.where(lo, outs[0], outs[1]).astype(BF16)


def _attn_a(qa, ka, va, kax, vax, sinks, layer, tq, tstep):
    b, s, _ = qa.shape
    c = kax.shape[1]

    def qmap(bb, t):
        return (bb, t, 0)

    def kmap(bb, t):
        return (bb, 0, 0)

    return pl.pallas_call(
        functools.partial(_attn_a_kernel, tq=tq, seq=s, ahead=2),
        grid=(b, s // tstep),
        in_specs=[
            pl.BlockSpec((1, tstep, 512), qmap),
            pl.BlockSpec((1, s, LANES), kmap),
            pl.BlockSpec((1, s, 2 * LANES), kmap),
            pl.BlockSpec((1, c, LANES), kmap),
            pl.BlockSpec((1, c, 2 * LANES), kmap),
            pl.BlockSpec((None, A_HEADS, LANES), lambda bb, t: (layer, 0, 0)),
        ],
        out_specs=pl.BlockSpec((1, tstep, 512), qmap),
        out_shape=jax.ShapeDtypeStruct((b, s, 512), BF16),
        compiler_params=_params("arbitrary", "arbitrary"),
        name="attn_window_gqa",
    )(qa, ka, va, kax, vax, sinks)


def _expand_rel_bias(rpb_ref, bias_ref):
    shape = (GRID_W, LANES)
    qc = lax.broadcasted_iota(jnp.int32, shape, 0)
    lane = lax.broadcasted_iota(jnp.int32, shape, 1)
    kc = lane & (GRID_W - 1)
    wc = jnp.clip(qc - NA_COLS // 2, 0, GRID_W - NA_COLS)
    colmask = jnp.where((kc >= wc) & (kc < wc + NA_COLS), 0.0, NEG_INF)
    lo = lane < GRID_W
    nrel = 2 * NA_ROWS - 1
    for hh in range(2):
        rows = [jnp.broadcast_to(rpb_ref[hh, d:d + 1, :], shape) for d in range(nrel)]
        t_lo = [pltpu.roll(r, GRID_W + 1, 1, stride=1, stride_axis=0) for r in rows[:-1]]
        t_hi = [pltpu.roll(r, 1, 1, stride=1, stride_axis=0) for r in rows[1:]]
        for d in range(nrel - 1):
            bias_ref[hh, d] = jnp.where(lo, t_lo[d], t_hi[d]) + colmask


def _attn_b_kernel(q_ref, k_ref, v_ref, kx_ref, vx_ref, rpb_ref, o_ref, bias_ref, *, rows, ahead):
    rb = pl.program_id(2)
    kx, vx = kx_ref[0], vx_ref[0]
    nrow = q_ref.shape[1] // GRID_W

    @pl.when(rb == 0)
    def _():
        _expand_rel_bias(rpb_ref, bias_ref)

    def scores(r):
        qr = rb * nrow + r
        wr = jnp.clip(qr - NA_ROWS // 2, 0, rows - NA_ROWS)
        oi = wr - qr + (NA_ROWS - 1)
        start = pl.multiple_of(wr * GRID_W, GRID_W)
        q = _split_pair(q_ref[0, r * GRID_W:(r + 1) * GRID_W, :])
        ks = k_ref[0, pl.ds(start, NA_ROWS * GRID_W), :]
        bias = jnp.concatenate(
            [jnp.concatenate([bias_ref[hh, oi + 2 * t] for t in range(NA_ROWS // 2)], axis=1)
             for hh in range(2)], axis=0)
        return _dot_t(q, ks) + bias, _dot_t(q, kx), start

    queue = [scores(r) for r in range(min(ahead, nrow))]
    for r in range(nrow):
        s_loc, s_ctx, start = queue.pop(0)
        if r + ahead < nrow:
            queue.append(scores(r + ahead))
        vs = v_ref[0, pl.ds(start, NA_ROWS * GRID_W), :]
        o = _softmax_pv([(s_loc, vs), (s_ctx, vx)])
        o_ref[0, r * GRID_W:(r + 1) * GRID_W, :] = _merge_pair(o).astype(BF16)


def _attn_b(qb, kb, vb, kbx, vbx, rpb, layer):
    b, s, _ = qb.shape
    c = kbx.shape[1]
    rows = s // GRID_W
    tq = min(4 * NA_ROWS * GRID_W, s)

    def qmap(bb, p, rb):
        return (bb, rb, p)

    def kmap(bb, p, rb):
        return (bb, 0, p)

    return pl.pallas_call(
        functools.partial(_attn_b_kernel, rows=rows, ahead=3),
        grid=(b, B_HEADS // 2, s // tq),
        in_specs=[
            pl.BlockSpec((1, tq, LANES), qmap),
            pl.BlockSpec((1, s, LANES), kmap),
            pl.BlockSpec((1, s, LANES), kmap),
            pl.BlockSpec((1, c, LANES), kmap),
            pl.BlockSpec((1, c, LANES), kmap),
            pl.BlockSpec((None, 2, 2 * NA_ROWS, LANES), lambda bb, p, rb: (layer, p, 0, 0)),
        ],
        out_specs=pl.BlockSpec((1, tq, LANES), qmap),
        out_shape=jax.ShapeDtypeStruct((b, s, 256), BF16),
        scratch_shapes=[pltpu.VMEM((2, 2 * NA_ROWS - 2, GRID_W, LANES), F32)],
        compiler_params=_params("arbitrary", "arbitrary", "arbitrary"),
        name="attn_neighbourhood",
    )(qb, kb, vb, kbx, vbx, rpb)


def _rel_bias_rows(rpb):
    pad = GRID_W - NA_COLS
    return jnp.pad(rpb * LOG2E, ((0, 0), (0, 0), (0, 1), (pad, pad + 1))).astype(F32)


def _attn_c_kernel(q_ref, k_ref, kx_ref, v_ref, vx_ref, o_ref, *, tq, kc, ahead):
    s = k_ref.shape[1]
    nsub = q_ref.shape[1] // tq
    chunks = [(k_ref, v_ref, c0, kc) for c0 in range(0, s, kc)] + [(kx_ref, vx_ref, 0, kx_ref.shape[1])]
    units = [(sb, ci) for sb in range(nsub) for ci in range(len(chunks))]

    def scores(u):
        sb, ci = units[u]
        kr, _, c0, n = chunks[ci]
        return [_dot_t(q_ref[0, sb * tq:(sb + 1) * tq, LANES * hh:LANES * (hh + 1)],
                       kr[0, c0:c0 + n, LANES * hh:LANES * (hh + 1)]) for hh in range(2)]

    m = [None, None]
    acc = [None, None]
    queue = [scores(u) for u in range(min(ahead, len(units)))]
    for u, (sb, ci) in enumerate(units):
        cur = queue.pop(0)
        if u + ahead < len(units):
            queue.append(scores(u + ahead))
        _, vr, c0, n = chunks[ci]
        for hh in range(2):
            sl = slice(LANES * hh, LANES * (hh + 1))
            sc = cur[hh]
            cm = sc.max(axis=-1, keepdims=True)
            if ci == 0:
                m[hh] = cm
                acc[hh] = _dot(jnp.exp2(sc - cm).astype(BF16), vr[0, c0:c0 + n, sl])
            else:
                m_new = jnp.maximum(m[hh], cm)
                acc[hh] = (jnp.exp2(m[hh] - m_new) * acc[hh]
                           + _dot(jnp.exp2(sc - m_new).astype(BF16), vr[0, c0:c0 + n, sl]))
                m[hh] = m_new
        if ci == len(chunks) - 1:
            o0 = acc[0] / acc[0][:, C_V:C_V + 1]
            o1 = acc[1] / acc[1][:, C_V:C_V + 1]
            o_ref[0, sb * tq:(sb + 1) * tq, :] = jnp.where(
                _lane_lo(o0.shape), o0, pltpu.roll(o1, C_V, 1)).astype(BF16)


def _attn_c(qc, kc, vc, kcx, vcx, tq, tstep):
    b, s, _ = qc.shape
    c = kcx.shape[1]

    def qmap(bb, p, i):
        return (bb, i, p)

    def kmap(bb, p, i):
        return (bb, 0, p)

    return pl.pallas_call(
        functools.partial(_attn_c_kernel, tq=tq, kc=512, ahead=1),
        grid=(b, C_HEADS // 2, s // tstep),
        in_specs=[
            pl.BlockSpec((1, tstep, 2 * LANES), qmap),
            pl.BlockSpec((1, s, 2 * LANES), kmap),
            pl.BlockSpec((1, c, 2 * LANES), kmap),
            pl.BlockSpec((1, s, 2 * LANES), kmap),
            pl.BlockSpec((1, c, 2 * LANES), kmap),
        ],
        out_specs=pl.BlockSpec((1, tstep, LANES), qmap),
        out_shape=jax.ShapeDtypeStruct((b, s, 256), BF16),
        compiler_params=_params("arbitrary", "arbitrary", "arbitrary"),
        name="attn_latent",
    )(qc, kc, kcx, vc, vcx)


def _attn_ctx_kernel(qa_ref, ka_ref, va_ref, qb_ref, kb_ref, vb_ref, qc_ref, kc_ref, vc_ref,
                     sink_ref, oa_ref, ob_ref, oc_ref):
    c = qa_ref.shape[1]
    half = A_HEADS // 2
    lo = _lane_lo((c, LANES))

    def blk(i):
        return slice(LANES * i, LANES * (i + 1))

    units = []
    for j in range(half):
        for g in range(A_KV_HEADS):
            def score(j=j, g=g):
                q2 = qa_ref[0, :, blk(j)]
                zero = jnp.zeros_like(q2)
                return _dot_t(jnp.where(lo, q2, zero) if g == 0 else jnp.where(lo, zero, q2), ka_ref[0])

            def fin_a(o, j=j):
                oa_ref[0, :, blk(j)] = jnp.where(lo, o[0], o[1]).astype(BF16)

            units.append((score, lambda g=g: va_ref[0, :, blk(g)], j + half * g, fin_a if g == 1 else None))
    for p in range(B_HEADS // 2):
        def score(p=p):
            return _dot_t(_split_pair(qb_ref[0, :, blk(p)]), kb_ref[0, :, blk(p)])

        def fin_b(o, p=p):
            ob_ref[0, :, blk(p)] = _merge_pair(o[0]).astype(BF16)

        units.append((score, lambda p=p: vb_ref[0, :, blk(p)], None, fin_b))
    for hd in range(C_HEADS):
        def score(hd=hd):
            return _dot_t(qc_ref[0, :, blk(hd)], kc_ref[0, :, blk(hd)])

        def fin_c(o, hd=hd):
            oc_ref[0, :, blk(hd // 2)] = jnp.where(lo, o[0], pltpu.roll(o[1], C_V, 1)).astype(BF16)

        units.append((score, lambda hd=hd: vc_ref[0, :, blk(hd)], None, fin_c if hd % 2 == 1 else None))

    pending = []
    nxt = units[0][0]()
    for u, (_, value, sink_head, finish) in enumerate(units):
        s = nxt
        if u + 1 < len(units):
            nxt = units[u + 1][0]()
        sink = None
        if sink_head is not None:
            sink = jnp.broadcast_to(sink_ref[sink_head:sink_head + 1, 0:1], (s.shape[0], 1))
        pending.append(_softmax_pv([(s, value())], sink))
        if finish is not None:
            finish(pending)
            pending = []


def _attn_ctx(parts, sinks, layer):
    b, c, _ = parts[0].shape
    widths = [a.shape[2] for a in parts]

    def bmap(bb):
        return (bb, 0, 0)

    return pl.pallas_call(
        _attn_ctx_kernel,
        grid=(b,),
        in_specs=[pl.BlockSpec((1, c, w), bmap) for w in widths]
        + [pl.BlockSpec((None, A_HEADS, LANES), lambda bb: (layer, 0, 0))],
        out_specs=[pl.BlockSpec((1, c, w), bmap) for w in (512, 256, 256)],
        out_shape=[jax.ShapeDtypeStruct((b, c, w), BF16) for w in (512, 256, 256)],
        compiler_params=_params("arbitrary"),
        name="attn_context",
    )(*parts, sinks)


def _outproj_mlp_kernel(x_ref, ma_ref, mb_ref, mc_ref, mod_ref, woa_ref, wob_ref, woc_ref,
                        g2_ref, w1_ref, w2_ref, gf_ref, o_ref, *, final, ff_chunk):
    mod = mod_ref[0]
    attn = _dot(ma_ref[0], woa_ref[...]) + _dot(mb_ref[0], wob_ref[...]) + _dot(mc_ref[0], woc_ref[...])
    x1 = x_ref[0] + mod[2:3] * attn
    h = (_rms(x1, g2_ref[...]) * (1.0 + mod[4:5]) + mod[3:4]).astype(BF16)
    d_ff = w1_ref.shape[1]
    def up(c0):
        return _dot(h, w1_ref[:, c0:c0 + ff_chunk])

    y = None
    chunk_starts = list(range(0, d_ff, ff_chunk))
    nxt = up(chunk_starts[0])
    for i, c0 in enumerate(chunk_starts):
        u = jnp.maximum(nxt, 0.0)
        if i + 1 < len(chunk_starts):
            nxt = up(chunk_starts[i + 1])
        part = _dot((u * u).astype(BF16), w2_ref[c0:c0 + ff_chunk, :])
        y = part if y is None else y + part
    x2 = x1 + mod[5:6] * y
    if final:
        x2 = _rms(x2, gf_ref[...])
    o_ref[0] = x2


def _outproj_mlp(x, ma, mb, mc, layer, mod_row, mods, woa, wob, woc, g2, w1, w2, gf, tm, final):
    b, s, d = x.shape
    d_ff = w1.shape[2]

    def xmap(bb, j):
        return (bb, j, 0)

    def modmap(bb, j):
        return (layer, (bb if mod_row is None else mod_row), 0, 0)

    def resident(shape, row_block=0):
        return pl.BlockSpec((None,) + shape, lambda bb, j: (layer, row_block, 0), pipeline_mode=pl.Buffered(1))

    return pl.pallas_call(
        functools.partial(_outproj_mlp_kernel, final=final, ff_chunk=1024),
        grid=(b, s // tm),
        in_specs=[
            pl.BlockSpec((1, tm, d), xmap),
            pl.BlockSpec((1, tm, 512), xmap),
            pl.BlockSpec((1, tm, 256), xmap),
            pl.BlockSpec((1, tm, 256), xmap),
            pl.BlockSpec((None, 1, N_MOD, d), modmap),
            resident((512, d)), resident((256, d), 2), resident((256, d), 3),
            resident((1, d)),
            resident((d, d_ff)), resident((d_ff, d)),
            pl.BlockSpec((1, d), lambda bb, j: (0, 0)),
        ],
        out_specs=pl.BlockSpec((1, tm, d), xmap),
        out_shape=jax.ShapeDtypeStruct((b, s, d), F32),
        compiler_params=_params("arbitrary", "arbitrary"),
        name="outproj_mlp",
    )(x, ma, mb, mc, mods, woa, wob, woc, g2, w1, w2, gf)


def _rope_tables(s):
    f32 = np.float32
    tok = np.arange(s)
    row, col = (tok // GRID_W).astype(f32), (tok % GRID_W).astype(f32)

    def cs(pos, half):
        freqs = (f32(ROPE_BASE) ** (-np.arange(half, dtype=f32) / f32(half))).astype(f32)
        ang = (pos[:, None] * freqs).astype(f32)
        return np.cos(ang).astype(f32), np.sin(ang).astype(f32)

    cr, sr = cs(row, 16)
    cc, sc = cs(col, 16)
    z = np.zeros_like(sr)
    a_c = np.tile(np.concatenate([cr, cr, cc, cc], axis=1), (1, 2))
    a_s1 = np.tile(np.concatenate([-sr, z, -sc, z], axis=1), (1, 2))
    a_s2 = np.tile(np.concatenate([z, sr, z, sc], axis=1), (1, 2))
    cr, sr = cs(row, 8)
    cc, sc = cs(col, 8)
    z = np.zeros_like(sr)
    one64, zero64 = np.ones((s, C_NOPE), f32), np.zeros((s, C_NOPE), f32)
    one32, zero32 = np.ones((s, 32), f32), np.zeros((s, 32), f32)
    c_c = np.concatenate([one64, cr, cr, cc, cc, one32], axis=1)
    c_s1 = np.concatenate([zero64, -sr, z, -sc, z, zero32], axis=1)
    c_s2 = np.concatenate([zero64, z, sr, z, sc, zero32], axis=1)
    return tuple(jnp.asarray(t) for t in (a_c, a_s1, a_s2, c_c, c_s1, c_s2))


def _identity_tables(s):
    one, zero = jnp.asarray(np.ones((s, LANES), np.float32)), jnp.asarray(np.zeros((s, LANES), np.float32))
    return (one, zero, zero, one, zero, zero)


def _pad_heads(w, offs, width):
    pad = jnp.zeros(w.shape[:2] + (LANES - width,), w.dtype)
    return jnp.concatenate([jnp.concatenate([w[:, :, o:o + width], pad], axis=2) for o in offs], axis=2)


def _weight_layouts(w_in, w_uq, w_ukv, w_out):
    depth, d, _ = w_in.shape
    w_in, w_out = w_in.astype(BF16), w_out.astype(BF16)
    qa_w = w_in[:, :, :512].reshape(depth, d, A_KV_HEADS, A_GROUP, HEAD_DIM).transpose(0, 1, 3, 2, 4)
    qa_w = qa_w.reshape(depth, d, 512)
    z = lambda n: jnp.zeros((depth, d, n), w_in.dtype)
    win = jnp.concatenate([qa_w, w_in[:, :, 512:1920], z(64), w_in[:, :, 1920:1952], z(32)], axis=2)
    hq = C_NOPE + C_ROPE
    wuq = _pad_heads(w_uq, [hq * h for h in range(C_HEADS)], hq).astype(BF16)
    wukv = w_ukv.astype(BF16)
    woa = w_out[:, :512].reshape(depth, A_KV_HEADS, A_GROUP, HEAD_DIM, d).transpose(0, 2, 1, 3, 4)
    woa = woa.reshape(depth, 512, d)
    wob = woc = w_out
    return win, wuq, wukv, woa, wob, woc


def kernel(x, c, ctx, c_ctx, w_ada, b_ada, norm1_g, norm2_g, w_in, attn_sink, na_rpb, mla_q_norm_g,
           mla_w_uq, mla_kv_norm_g, mla_w_ukv, w_out, w_mlp_in, w_mlp_out, final_norm_g):
    b, s, d = x.shape
    n_ctx = ctx.shape[1]
    depth = w_ada.shape[0]
    assert b + 1 <= MOD_ROWS and s % 1024 == 0 and n_ctx % 128 == 0
    tm_x = 512
    tm_c = min(n_ctx, 256)
    tq_a = 256
    tq_c = 512

    cvec = jnp.concatenate([c, c_ctx[None], jnp.zeros((MOD_ROWS - b - 1, d), c.dtype)], axis=0)
    mods = _modulation(cvec, w_ada, b_ada).reshape(depth, MOD_ROWS, N_MOD, d)
    tabs_x = _rope_tables(s)
    tabs_c = _identity_tables(n_ctx)
    gf = final_norm_g.reshape(1, d)
    win, wuq, wukv, woa, wob, woc = _weight_layouts(w_in, mla_w_uq, mla_w_ukv, w_out)
    w1, w2 = w_mlp_in.astype(BF16), w_mlp_out.astype(BF16)
    sinks = jnp.broadcast_to((attn_sink * LOG2E)[:, :, None], (depth, A_HEADS, LANES)).astype(F32)
    bias = _rel_bias_rows(na_rpb)
    g1, g2 = norm1_g.reshape(depth, 1, d), norm2_g.reshape(depth, 1, d)
    gq, gkv = mla_q_norm_g.reshape(depth, 1, -1), mla_kv_norm_g.reshape(depth, 1, -1)

    for l in range(depth):
        last = l == depth - 1
        xs = _inproj(x, l, None, mods, g1, win, gq, wuq, gkv, wukv, tabs_x, 2 * tm_x)
        cs = _inproj(ctx, l, b, mods, g1, win, gq, wuq, gkv, wukv, tabs_c, tm_c)
        qa, ka, va, qb, kb, vb, qc, kc, vc = xs
        cqa, cka, cva, cqb, ckb, cvb, cqc, ckc, cvc = cs

        oa = _attn_a(qa, ka, va, cka, cva, sinks, l, tq_a, 4 * tq_a)
        ob = _attn_b(qb, kb, vb, ckb, cvb, bias, l)
        oc = _attn_c(qc, kc, vc, ckc, cvc, tq_c // 2, 2 * tq_c)
        x = _outproj_mlp(x, oa, ob, oc, l, None, mods, woa, wob, woc, g2, w1, w2, gf, 2 * tm_x, last)
        if not last:
            coa, cob, coc = _attn_ctx(cs, sinks, l)
            ctx = _outproj_mlp(ctx, coa, cob, coc, l, b, mods, woa, wob, woc, g2, w1, w2, gf, tm_c, False)
    return x
```

```python
import functools
import math

import jax
import jax.numpy as jnp
import numpy as np
from jax import lax
from jax.experimental import pallas as pl
from jax.experimental.pallas import tpu as pltpu

F32 = jnp.float32
BF16 = jnp.bfloat16

GRID_W = 64
HEAD_DIM = 64
A_HEADS = 8
A_KV_HEADS = 2
A_GROUP = A_HEADS // A_KV_HEADS
A_WINDOW = 128
B_HEADS = 4
NA_ROWS = 8
NA_COLS = 16
C_HEADS = 4
C_Q_RANK = 256
C_KV_RANK = 128
C_NOPE = 64
C_ROPE = 32
C_V = 64
N_MOD = 6
ROPE_BASE = 10000.0
EPS = 1e-6
NEG_INF = -1e30
LOG2E = math.log2(math.e)

LANES = 128
MOD_ROWS = 16
IN_COLS = 2048
VMEM_LIMIT = 56 * 1024 * 1024

_OFF_QA, _OFF_KA, _OFF_VA = 0, 512, 640
_OFF_QB, _OFF_KB, _OFF_VB = 768, 1024, 1280
_OFF_CQ, _OFF_CKV, _OFF_KR = 1536, 1792, 1920


def _dot(a, b):
    return jnp.dot(a, b, preferred_element_type=F32)


def _dot_t(a, b):
    return lax.dot_general(a, b, (((1,), (1,)), ((), ())), preferred_element_type=F32)


def _params(*sem):
    return pltpu.CompilerParams(dimension_semantics=sem, vmem_limit_bytes=VMEM_LIMIT)


def _mod_kernel(c_ref, w_ref, b_ref, o_ref):
    c = c_ref[...]
    s = c * (1.0 / (1.0 + jnp.exp(-c)))
    o_ref[0] = _dot(s.astype(BF16), w_ref[0].astype(BF16)) + b_ref[0]


def _modulation(cvec, w_ada, b_ada):
    depth, d, n = w_ada.shape
    tn = 1024
    return pl.pallas_call(
        _mod_kernel,
        grid=(depth, n // tn),
        in_specs=[
            pl.BlockSpec((MOD_ROWS, d), lambda l, j: (0, 0)),
            pl.BlockSpec((1, d, tn), lambda l, j: (l, 0, j)),
            pl.BlockSpec((1, 1, tn), lambda l, j: (l, 0, j)),
        ],
        out_specs=pl.BlockSpec((1, MOD_ROWS, tn), lambda l, j: (l, 0, j)),
        out_shape=jax.ShapeDtypeStruct((depth, MOD_ROWS, n), F32),
        compiler_params=_params("arbitrary", "arbitrary"),
        name="adaln_modulation",
    )(cvec, w_ada, b_ada.reshape(depth, 1, n))


def _rope(x, c, s1, s2, shift):
    return x * c + pltpu.roll(x, LANES - shift, 1) * s1 + pltpu.roll(x, shift, 1) * s2


def _rms(x, g):
    return x * lax.rsqrt(jnp.mean(x * x, axis=-1, keepdims=True) + EPS) * g


def _inproj_kernel(x_ref, mod_ref, g1_ref, win_ref, gq_ref, wuq_ref, gkv_ref, wukv_ref,
                   ac_ref, as1_ref, as2_ref, cc_ref, cs1_ref, cs2_ref,
                   qa_ref, ka_ref, va_ref, qb_ref, kb_ref, vb_ref, qc_ref, kc_ref, vc_ref, *, sub):
    tm = x_ref.shape[1]
    mod = mod_ref[0]
    scale_ab = HEAD_DIM ** -0.5 * LOG2E
    scale_c = (C_NOPE + C_ROPE) ** -0.5 * LOG2E
    ones_lane = (lax.broadcasted_iota(jnp.int32, (1, LANES), 1) == C_V).astype(F32)

    subs = [slice(r0, r0 + sub) for r0 in range(0, tm, sub)]

    def up_project(pc):
        cq = _rms(pc[:, 0:C_Q_RANK], gq_ref[...]).astype(BF16)
        ckv = _rms(pc[:, C_Q_RANK:C_Q_RANK + C_KV_RANK], gkv_ref[...]).astype(BF16)
        return _dot(cq, wuq_ref[...]), _dot(ckv, wukv_ref[...])

    pabs, pcs, ups = [], [], []
    for rs in subs:
        h = (_rms(x_ref[0, rs, :], g1_ref[...]) * (1.0 + mod[1:2]) + mod[0:1]).astype(BF16)
        pcs.append(_dot(h, win_ref[:, _OFF_CQ:IN_COLS]))
        pabs.append(_dot(h, win_ref[:, 0:_OFF_CQ]))
        ups.append(up_project(pcs[-1]))

    for rs, p, pc, (q_up, kv_up) in zip(subs, pabs, pcs, ups):
        ac, as1, as2 = ac_ref[rs, :], as1_ref[rs, :], as2_ref[rs, :]
        cc, cs1, cs2 = cc_ref[rs, :], cs1_ref[rs, :], cs2_ref[rs, :]
        for j in range(4):
            blk = p[:, _OFF_QA + LANES * j:_OFF_QA + LANES * (j + 1)]
            qa_ref[0, rs, LANES * j:LANES * (j + 1)] = (_rope(blk, ac, as1, as2, 16) * scale_ab).astype(BF16)
        ka_ref[0, rs, :] = _rope(p[:, _OFF_KA:_OFF_KA + LANES], ac, as1, as2, 16).astype(BF16)
        va = p[:, _OFF_VA:_OFF_VA + LANES]
        lane = lax.broadcasted_iota(jnp.int32, va.shape, 1)
        va_ref[0, rs, 0:LANES] = jnp.where(lane < HEAD_DIM, va, jnp.where(lane == HEAD_DIM, 1.0, 0.0)).astype(BF16)
        va_ref[0, rs, LANES:2 * LANES] = jnp.where(lane >= HEAD_DIM, va, jnp.where(lane == 0, 1.0, 0.0)).astype(BF16)
        qb_ref[0, rs, :] = (p[:, _OFF_QB:_OFF_QB + 256] * scale_ab).astype(BF16)
        kb_ref[0, rs, :] = p[:, _OFF_KB:_OFF_KB + 256].astype(BF16)
        vb_ref[0, rs, :] = p[:, _OFF_VB:_OFF_VB + 256].astype(BF16)
        kr = _rope(pc[:, _OFF_KR - _OFF_CQ:], cc, cs1, cs2, 8)
        lo = lane < C_NOPE
        for hd in range(C_HEADS):
            sl = slice(LANES * hd, LANES * (hd + 1))
            qc_ref[0, rs, sl] = (_rope(q_up[:, sl], cc, cs1, cs2, 8) * scale_c).astype(BF16)
            kv = kv_up[:, sl]
            kc_ref[0, rs, sl] = jnp.where(lo, kv, kr).astype(BF16)
            vc_ref[0, rs, sl] = jnp.where(lo, pltpu.roll(kv, C_V, 1), ones_lane).astype(BF16)


def _inproj(x, layer, mod_row, mods, g1, win, gq, wuq, gkv, wukv, tabs, tm):
    b, s, d = x.shape
    nt = s // tm

    def xmap(j, bb):
        return (bb, j, 0)

    def modmap(j, bb):
        return (layer, (bb if mod_row is None else mod_row), 0, 0)

    def wmap(j, bb):
        return (layer, 0, 0)

    def tabmap(j, bb):
        return (j, 0)

    widths = (512, 128, 256, 256, 256, 256, 512, 512, 512)
    in_specs = [
        pl.BlockSpec((1, tm, d), xmap),
        pl.BlockSpec((None, 1, N_MOD, d), modmap),
        pl.BlockSpec((None, 1, d), wmap),
        pl.BlockSpec((None, d, IN_COLS), wmap),
        pl.BlockSpec((None, 1, C_Q_RANK), wmap),
        pl.BlockSpec((None, C_Q_RANK, 512), wmap),
        pl.BlockSpec((None, 1, C_KV_RANK), wmap),
        pl.BlockSpec((None, C_KV_RANK, 512), wmap),
    ] + [pl.BlockSpec((tm, LANES), tabmap) for _ in range(6)]
    return pl.pallas_call(
        functools.partial(_inproj_kernel, sub=min(tm, 256)),
        grid=(nt, b),
        in_specs=in_specs,
        out_specs=[pl.BlockSpec((1, tm, w), xmap) for w in widths],
        out_shape=[jax.ShapeDtypeStruct((b, s, w), BF16) for w in widths],
        compiler_params=_params("arbitrary", "arbitrary"),
        name="norm_inproj_rope",
    )(x, mods, g1, win, gq, wuq, gkv, wukv, *tabs)


def _lane_lo(shape):
    return lax.broadcasted_iota(jnp.int32, shape, 1) < HEAD_DIM


def _split_pair(q2):
    lo = _lane_lo(q2.shape)
    zero = jnp.zeros_like(q2)
    return jnp.concatenate([jnp.where(lo, q2, zero), jnp.where(lo, zero, q2)], axis=0)


def _merge_pair(o):
    n = o.shape[0] // 2
    return jnp.where(_lane_lo((n, o.shape[1])), o[:n], o[n:])


def _softmax_pv(parts, sink=None):
    m = parts[0][0].max(axis=-1, keepdims=True)
    for s, _ in parts[1:]:
        m = jnp.maximum(m, s.max(axis=-1, keepdims=True))
    if sink is not None:
        m = jnp.maximum(m, sink)
    denom = None
    acc = None
    for s, v in parts:
        e = jnp.exp2(s - m)
        r = e.sum(axis=-1, keepdims=True)
        denom = r if denom is None else denom + r
        o = _dot(e.astype(BF16), v)
        acc = o if acc is None else acc + o
    if sink is not None:
        denom = denom + jnp.exp2(sink - m)
    return acc / denom


def _attn_a_kernel(q_ref, k_ref, v_ref, kx_ref, vx_ref, sink_ref, o_ref, *, tq, seq, ahead):
    t = pl.program_id(1)
    nsub = q_ref.shape[1] // tq
    nloc = tq + 2 * A_WINDOW
    kx = kx_ref[0]
    lo = _lane_lo((tq, LANES))
    iota_rel = (lax.broadcasted_iota(jnp.int32, (tq, nloc), 1) - lax.broadcasted_iota(jnp.int32, (tq, nloc), 0))
    starts, bands = [], []
    for sb in range(nsub):
        qstart = (t * nsub + sb) * tq
        start = pl.multiple_of(jnp.clip(qstart - A_WINDOW, 0, seq - nloc), A_WINDOW)
        rel = iota_rel + (start - qstart)
        starts.append(start)
        bands.append(jnp.where(jnp.abs(rel) <= A_WINDOW, 0.0, NEG_INF))
    units = [(sb, j, g) for sb in range(nsub) for j in range(A_GROUP) for g in range(A_KV_HEADS)]

    def scores(u):
        sb, j, g = units[u]
        q2 = q_ref[0, sb * tq:(sb + 1) * tq, LANES * j:LANES * (j + 1)]
        zero = jnp.zeros_like(q2)
        q = jnp.where(lo, q2, zero) if g == 0 else jnp.where(lo, zero, q2)
        return _dot_t(q, k_ref[0, pl.ds(starts[sb], nloc), :]) + bands[sb], _dot_t(q, kx)

    outs = {}
    queue = [scores(u) for u in range(min(ahead, len(units)))]
    for u, (sb, j, g) in enumerate(units):
        s_loc, s_ctx = queue.pop(0)
        if u + ahead < len(units):
            queue.append(scores(u + ahead))
        vsg = v_ref[0, pl.ds(starts[sb], nloc), LANES * g:LANES * (g + 1)]
        vxg = vx_ref[0, :, LANES * g:LANES * (g + 1)]
        hd = g * A_GROUP + j
        sink = jnp.broadcast_to(sink_ref[hd:hd + 1, 0:1], (tq, 1))
        m = jnp.maximum(jnp.maximum(s_loc.max(axis=-1, keepdims=True), s_ctx.max(axis=-1, keepdims=True)), sink)
        o = _dot(jnp.exp2(s_loc - m).astype(BF16), vsg) + _dot(jnp.exp2(s_ctx - m).astype(BF16), vxg)
        denom = (o[:, HEAD_DIM:HEAD_DIM + 1] if g == 0 else o[:, 0:1]) + jnp.exp2(sink - m)
        outs[g] = o / denom
        if g == A_KV_HEADS - 1:
            o_ref[0, sb * tq:(sb + 1) * tq, LANES * j:LANES * (j + 1)] = jnp.where(lo, outs[0], outs[1]).astype(BF16)


def _attn_a(qa, ka, va, kax, vax, sinks, layer, tq, tstep):
    b, s, _ = qa.shape
    c = kax.shape[1]

    def qmap(bb, t):
        return (bb, t, 0)

    def kmap(bb, t):
        return (bb, 0, 0)

    return pl.pallas_call(
        functools.partial(_attn_a_kernel, tq=tq, seq=s, ahead=2),
        grid=(b, s // tstep),
        in_specs=[
            pl.BlockSpec((1, tstep, 512), qmap),
            pl.BlockSpec((1, s, LANES), kmap),
            pl.BlockSpec((1, s, 2 * LANES), kmap),
            pl.BlockSpec((1, c, LANES), kmap),
            pl.BlockSpec((1, c, 2 * LANES), kmap),
            pl.BlockSpec((None, A_HEADS, LANES), lambda bb, t: (layer, 0, 0)),
        ],
        out_specs=pl.BlockSpec((1, tstep, 512), qmap),
        out_shape=jax.ShapeDtypeStruct((b, s, 512), BF16),
        compiler_params=_params("arbitrary", "arbitrary"),
        name="attn_window_gqa",
    )(qa, ka, va, kax, vax, sinks)


def _expand_rel_bias(rpb_ref, bias_ref):
    shape = (GRID_W, LANES)
    qc = lax.broadcasted_iota(jnp.int32, shape, 0)
    lane = lax.broadcasted_iota(jnp.int32, shape, 1)
    kc = lane & (GRID_W - 1)
    wc = jnp.clip(qc - NA_COLS // 2, 0, GRID_W - NA_COLS)
    colmask = jnp.where((kc >= wc) & (kc < wc + NA_COLS), 0.0, NEG_INF)
    lo = lane < GRID_W
    nrel = 2 * NA_ROWS - 1
    for hh in range(2):
        rows = [jnp.broadcast_to(rpb_ref[hh, d:d + 1, :], shape) for d in range(nrel)]
        t_lo = [pltpu.roll(r, GRID_W + 1, 1, stride=1, stride_axis=0) for r in rows[:-1]]
        t_hi = [pltpu.roll(r, 1, 1, stride=1, stride_axis=0) for r in rows[1:]]
        for d in range(nrel - 1):
            bias_ref[hh, d] = jnp.where(lo, t_lo[d], t_hi[d]) + colmask


def _attn_b_kernel(q_ref, k_ref, v_ref, kx_ref, vx_ref, rpb_ref, o_ref, bias_ref, *, rows, ahead):
    rb = pl.program_id(2)
    kx, vx = kx_ref[0], vx_ref[0]
    nrow = q_ref.shape[1] // GRID_W

    @pl.when((pl.program_id(1) == 0) & (rb == 0))
    def _():
        _expand_rel_bias(rpb_ref, bias_ref)

    def scores(r):
        qr = rb * nrow + r
        wr = jnp.clip(qr - NA_ROWS // 2, 0, rows - NA_ROWS)
        oi = wr - qr + (NA_ROWS - 1)
        start = pl.multiple_of(wr * GRID_W, GRID_W)
        q = _split_pair(q_ref[0, r * GRID_W:(r + 1) * GRID_W, :])
        ks = k_ref[0, pl.ds(start, NA_ROWS * GRID_W), :]
        bias = jnp.concatenate(
            [jnp.concatenate([bias_ref[hh, oi + 2 * t] for t in range(NA_ROWS // 2)], axis=1)
             for hh in range(2)], axis=0)
        return _dot_t(q, ks) + bias, _dot_t(q, kx), start

    queue = [scores(r) for r in range(min(ahead, nrow))]
    for r in range(nrow):
        s_loc, s_ctx, start = queue.pop(0)
        if r + ahead < nrow:
            queue.append(scores(r + ahead))
        vs = v_ref[0, pl.ds(start, NA_ROWS * GRID_W), :]
        o = _softmax_pv([(s_loc, vs), (s_ctx, vx)])
        o_ref[0, r * GRID_W:(r + 1) * GRID_W, :] = _merge_pair(o).astype(BF16)


def _attn_b(qb, kb, vb, kbx, vbx, rpb, layer):
    b, s, _ = qb.shape
    c = kbx.shape[1]
    rows = s // GRID_W
    tq = min(4 * NA_ROWS * GRID_W, s)

    def qmap(p, bb, rb):
        return (bb, rb, p)

    def kmap(p, bb, rb):
        return (bb, 0, p)

    return pl.pallas_call(
        functools.partial(_attn_b_kernel, rows=rows, ahead=3),
        grid=(B_HEADS // 2, b, s // tq),
        in_specs=[
            pl.BlockSpec((1, tq, LANES), qmap),
            pl.BlockSpec((1, s, LANES), kmap),
            pl.BlockSpec((1, s, LANES), kmap),
            pl.BlockSpec((1, c, LANES), kmap),
            pl.BlockSpec((1, c, LANES), kmap),
            pl.BlockSpec((None, 2, 2 * NA_ROWS, LANES), lambda p, bb, rb: (layer, p, 0, 0)),
        ],
        out_specs=pl.BlockSpec((1, tq, LANES), qmap),
        out_shape=jax.ShapeDtypeStruct((b, s, 256), BF16),
        scratch_shapes=[pltpu.VMEM((2, 2 * NA_ROWS - 2, GRID_W, LANES), F32)],
        compiler_params=_params("arbitrary", "arbitrary", "arbitrary"),
        name="attn_neighbourhood",
    )(qb, kb, vb, kbx, vbx, rpb)


def _rel_bias_rows(rpb):
    pad = GRID_W - NA_COLS
    return jnp.pad(rpb * LOG2E, ((0, 0), (0, 0), (0, 1), (pad, pad + 1))).astype(F32)


def _attn_c_kernel(q_ref, k_ref, kx_ref, v_ref, vx_ref, o_ref, *, tq, kc, ahead):
    s = k_ref.shape[1]
    nsub = q_ref.shape[1] // tq
    chunks = [(k_ref, v_ref, c0, kc) for c0 in range(0, s, kc)] + [(kx_ref, vx_ref, 0, kx_ref.shape[1])]
    units = [(sb, ci) for sb in range(nsub) for ci in range(len(chunks))]

    def scores(u):
        sb, ci = units[u]
        kr, _, c0, n = chunks[ci]
        return [_dot_t(q_ref[0, sb * tq:(sb + 1) * tq, LANES * hh:LANES * (hh + 1)],
                       kr[0, c0:c0 + n, LANES * hh:LANES * (hh + 1)]) for hh in range(2)]

    m = [None, None]
    acc = [None, None]
    queue = [scores(u) for u in range(min(ahead, len(units)))]
    for u, (sb, ci) in enumerate(units):
        cur = queue.pop(0)
        if u + ahead < len(units):
            queue.append(scores(u + ahead))
        _, vr, c0, n = chunks[ci]
        for hh in range(2):
            sl = slice(LANES * hh, LANES * (hh + 1))
            sc = cur[hh]
            cm = sc.max(axis=-1, keepdims=True)
            if ci == 0:
                m[hh] = cm
                acc[hh] = _dot(jnp.exp2(sc - cm).astype(BF16), vr[0, c0:c0 + n, sl])
            else:
                m_new = jnp.maximum(m[hh], cm)
                acc[hh] = (jnp.exp2(m[hh] - m_new) * acc[hh]
                           + _dot(jnp.exp2(sc - m_new).astype(BF16), vr[0, c0:c0 + n, sl]))
                m[hh] = m_new
        if ci == len(chunks) - 1:
            o0 = acc[0] / acc[0][:, C_V:C_V + 1]
            o1 = acc[1] / acc[1][:, C_V:C_V + 1]
            o_ref[0, sb * tq:(sb + 1) * tq, :] = jnp.where(
                _lane_lo(o0.shape), o0, pltpu.roll(o1, C_V, 1)).astype(BF16)


def _attn_c(qc, kc, vc, kcx, vcx, tq, tstep):
    b, s, _ = qc.shape
    c = kcx.shape[1]

    def qmap(bb, p, i):
        return (bb, i, p)

    def kmap(bb, p, i):
        return (bb, 0, p)

    return pl.pallas_call(
        functools.partial(_attn_c_kernel, tq=tq, kc=512, ahead=1),
        grid=(b, C_HEADS // 2, s // tstep),
        in_specs=[
            pl.BlockSpec((1, tstep, 2 * LANES), qmap),
            pl.BlockSpec((1, s, 2 * LANES), kmap),
            pl.BlockSpec((1, c, 2 * LANES), kmap),
            pl.BlockSpec((1, s, 2 * LANES), kmap),
            pl.BlockSpec((1, c, 2 * LANES), kmap),
        ],
        out_specs=pl.BlockSpec((1, tstep, LANES), qmap),
        out_shape=jax.ShapeDtypeStruct((b, s, 256), BF16),
        compiler_params=_params("arbitrary", "arbitrary", "arbitrary"),
        name="attn_latent",
    )(qc, kc, kcx, vc, vcx)


def _attn_ctx_kernel(qa_ref, ka_ref, va_ref, qb_ref, kb_ref, vb_ref, qc_ref, kc_ref, vc_ref,
                     sink_ref, oa_ref, ob_ref, oc_ref):
    c = qa_ref.shape[1]
    half = A_HEADS // 2
    lo = _lane_lo((c, LANES))

    def blk(i):
        return slice(LANES * i, LANES * (i + 1))

    units = []
    for j in range(half):
        for g in range(A_KV_HEADS):
            def score(j=j, g=g):
                q2 = qa_ref[0, :, blk(j)]
                zero = jnp.zeros_like(q2)
                return _dot_t(jnp.where(lo, q2, zero) if g == 0 else jnp.where(lo, zero, q2), ka_ref[0])

            def fin_a(o, j=j):
                oa_ref[0, :, blk(j)] = jnp.where(lo, o[0], o[1]).astype(BF16)

            units.append((score, lambda g=g: va_ref[0, :, blk(g)], j + half * g, fin_a if g == 1 else None))
    for p in range(B_HEADS // 2):
        def score(p=p):
            return _dot_t(_split_pair(qb_ref[0, :, blk(p)]), kb_ref[0, :, blk(p)])

        def fin_b(o, p=p):
            ob_ref[0, :, blk(p)] = _merge_pair(o[0]).astype(BF16)

        units.append((score, lambda p=p: vb_ref[0, :, blk(p)], None, fin_b))
    for hd in range(C_HEADS):
        def score(hd=hd):
            return _dot_t(qc_ref[0, :, blk(hd)], kc_ref[0, :, blk(hd)])

        def fin_c(o, hd=hd):
            oc_ref[0, :, blk(hd // 2)] = jnp.where(lo, o[0], pltpu.roll(o[1], C_V, 1)).astype(BF16)

        units.append((score, lambda hd=hd: vc_ref[0, :, blk(hd)], None, fin_c if hd % 2 == 1 else None))

    pending = []
    nxt = units[0][0]()
    for u, (_, value, sink_head, finish) in enumerate(units):
        s = nxt
        if u + 1 < len(units):
            nxt = units[u + 1][0]()
        sink = None
        if sink_head is not None:
            sink = jnp.broadcast_to(sink_ref[sink_head:sink_head + 1, 0:1], (s.shape[0], 1))
        pending.append(_softmax_pv([(s, value())], sink))
        if finish is not None:
            finish(pending)
            pending = []


def _attn_ctx(parts, sinks, layer):
    b, c, _ = parts[0].shape
    widths = [a.shape[2] for a in parts]

    def bmap(bb):
        return (bb, 0, 0)

    return pl.pallas_call(
        _attn_ctx_kernel,
        grid=(b,),
        in_specs=[pl.BlockSpec((1, c, w), bmap) for w in widths]
        + [pl.BlockSpec((None, A_HEADS, LANES), lambda bb: (layer, 0, 0))],
        out_specs=[pl.BlockSpec((1, c, w), bmap) for w in (512, 256, 256)],
        out_shape=[jax.ShapeDtypeStruct((b, c, w), BF16) for w in (512, 256, 256)],
        compiler_params=_params("arbitrary"),
        name="attn_context",
    )(*parts, sinks)


def _outproj_mlp_kernel(x_ref, ma_ref, mb_ref, mc_ref, mod_ref, woa_ref, wob_ref, woc_ref,
                        g2_ref, w1_ref, w2_ref, gf_ref, o_ref, *, final, ff_chunk):
    mod = mod_ref[0]
    attn = _dot(ma_ref[0], woa_ref[...]) + _dot(mb_ref[0], wob_ref[...]) + _dot(mc_ref[0], woc_ref[...])
    x1 = x_ref[0] + mod[2:3] * attn
    h = (_rms(x1, g2_ref[...]) * (1.0 + mod[4:5]) + mod[3:4]).astype(BF16)
    d_ff = w1_ref.shape[1]
    def up(c0):
        return _dot(h, w1_ref[:, c0:c0 + ff_chunk])

    y = None
    chunk_starts = list(range(0, d_ff, ff_chunk))
    nxt = up(chunk_starts[0])
    for i, c0 in enumerate(chunk_starts):
        u = jnp.maximum(nxt, 0.0)
        if i + 1 < len(chunk_starts):
            nxt = up(chunk_starts[i + 1])
        part = _dot((u * u).astype(BF16), w2_ref[c0:c0 + ff_chunk, :])
        y = part if y is None else y + part
    x2 = x1 + mod[5:6] * y
    if final:
        x2 = _rms(x2, gf_ref[...])
    o_ref[0] = x2


def _outproj_mlp(x, ma, mb, mc, layer, mod_row, mods, woa, wob, woc, g2, w1, w2, gf, tm, final):
    b, s, d = x.shape
    d_ff = w1.shape[2]

    def xmap(bb, j):
        return (bb, j, 0)

    def modmap(bb, j):
        return (layer, (bb if mod_row is None else mod_row), 0, 0)

    def resident(shape, row_block=0):
        return pl.BlockSpec((None,) + shape, lambda bb, j: (layer, row_block, 0), pipeline_mode=pl.Buffered(1))

    return pl.pallas_call(
        functools.partial(_outproj_mlp_kernel, final=final, ff_chunk=1024),
        grid=(b, s // tm),
        in_specs=[
            pl.BlockSpec((1, tm, d), xmap),
            pl.BlockSpec((1, tm, 512), xmap),
            pl.BlockSpec((1, tm, 256), xmap),
            pl.BlockSpec((1, tm, 256), xmap),
            pl.BlockSpec((None, 1, N_MOD, d), modmap),
            resident((512, d)), resident((256, d), 2), resident((256, d), 3),
            resident((1, d)),
            resident((d, d_ff)), resident((d_ff, d)),
            pl.BlockSpec((1, d), lambda bb, j: (0, 0)),
        ],
        out_specs=pl.BlockSpec((1, tm, d), xmap),
        out_shape=jax.ShapeDtypeStruct((b, s, d), F32),
        compiler_params=_params("arbitrary", "arbitrary"),
        name="outproj_mlp",
    )(x, ma, mb, mc, mods, woa, wob, woc, g2, w1, w2, gf)


def _rope_tables(s):
    f32 = np.float32
    tok = np.arange(s)
    row, col = (tok // GRID_W).astype(f32), (tok % GRID_W).astype(f32)

    def cs(pos, half):
        freqs = (f32(ROPE_BASE) ** (-np.arange(half, dtype=f32) / f32(half))).astype(f32)
        ang = (pos[:, None] * freqs).astype(f32)
        return np.cos(ang).astype(f32), np.sin(ang).astype(f32)

    cr, sr = cs(row, 16)
    cc, sc = cs(col, 16)
    z = np.zeros_like(sr)
    a_c = np.tile(np.concatenate([cr, cr, cc, cc], axis=1), (1, 2))
    a_s1 = np.tile(np.concatenate([-sr, z, -sc, z], axis=1), (1, 2))
    a_s2 = np.tile(np.concatenate([z, sr, z, sc], axis=1), (1, 2))
    cr, sr = cs(row, 8)
    cc, sc = cs(col, 8)
    z = np.zeros_like(sr)
    one64, zero64 = np.ones((s, C_NOPE), f32), np.zeros((s, C_NOPE), f32)
    one32, zero32 = np.ones((s, 32), f32), np.zeros((s, 32), f32)
    c_c = np.concatenate([one64, cr, cr, cc, cc, one32], axis=1)
    c_s1 = np.concatenate([zero64, -sr, z, -sc, z, zero32], axis=1)
    c_s2 = np.concatenate([zero64, z, sr, z, sc, zero32], axis=1)
    return tuple(jnp.asarray(t) for t in (a_c, a_s1, a_s2, c_c, c_s1, c_s2))


def _identity_tables(s):
    one, zero = jnp.asarray(np.ones((s, LANES), np.float32)), jnp.asarray(np.zeros((s, LANES), np.float32))
    return (one, zero, zero, one, zero, zero)


def _pad_heads(w, offs, width):
    pad = jnp.zeros(w.shape[:2] + (LANES - width,), w.dtype)
    return jnp.concatenate([jnp.concatenate([w[:, :, o:o + width], pad], axis=2) for o in offs], axis=2)


def _weight_layouts(w_in, w_uq, w_ukv, w_out):
    depth, d, _ = w_in.shape
    w_in, w_out = w_in.astype(BF16), w_out.astype(BF16)
    qa_w = w_in[:, :, :512].reshape(depth, d, A_KV_HEADS, A_GROUP, HEAD_DIM).transpose(0, 1, 3, 2, 4)
    qa_w = qa_w.reshape(depth, d, 512)
    z = lambda n: jnp.zeros((depth, d, n), w_in.dtype)
    win = jnp.concatenate([qa_w, w_in[:, :, 512:1920], z(64), w_in[:, :, 1920:1952], z(32)], axis=2)
    hq = C_NOPE + C_ROPE
    wuq = _pad_heads(w_uq, [hq * h for h in range(C_HEADS)], hq).astype(BF16)
    wukv = w_ukv.astype(BF16)
    woa = w_out[:, :512].reshape(depth, A_KV_HEADS, A_GROUP, HEAD_DIM, d).transpose(0, 2, 1, 3, 4)
    woa = woa.reshape(depth, 512, d)
    wob = woc = w_out
    return win, wuq, wukv, woa, wob, woc


def kernel(x, c, ctx, c_ctx, w_ada, b_ada, norm1_g, norm2_g, w_in, attn_sink, na_rpb, mla_q_norm_g,
           mla_w_uq, mla_kv_norm_g, mla_w_ukv, w_out, w_mlp_in, w_mlp_out, final_norm_g):
    b, s, d = x.shape
    n_ctx = ctx.shape[1]
    depth = w_ada.shape[0]
    assert b + 1 <= MOD_ROWS and s % 1024 == 0 and n_ctx % 128 == 0
    tm_x = 512
    tm_c = min(n_ctx, 256)
    tq_a = 256
    tq_c = 512

    cvec = jnp.concatenate([c, c_ctx[None], jnp.zeros((MOD_ROWS - b - 1, d), c.dtype)], axis=0)
    mods = _modulation(cvec, w_ada, b_ada).reshape(depth, MOD_ROWS, N_MOD, d)
    tabs_x = _rope_tables(s)
    tabs_c = _identity_tables(n_ctx)
    gf = final_norm_g.reshape(1, d)
    win, wuq, wukv, woa, wob, woc = _weight_layouts(w_in, mla_w_uq, mla_w_ukv, w_out)
    w1, w2 = w_mlp_in.astype(BF16), w_mlp_out.astype(BF16)
    sinks = jnp.broadcast_to((attn_sink * LOG2E)[:, :, None], (depth, A_HEADS, LANES)).astype(F32)
    bias = _rel_bias_rows(na_rpb)
    g1, g2 = norm1_g.reshape(depth, 1, d), norm2_g.reshape(depth, 1, d)
    gq, gkv = mla_q_norm_g.reshape(depth, 1, -1), mla_kv_norm_g.reshape(depth, 1, -1)

    for l in range(depth):
        last = l == depth - 1
        xs = _inproj(x, l, None, mods, g1, win, gq, wuq, gkv, wukv, tabs_x, 2 * tm_x)
        cs = _inproj(ctx, l, b, mods, g1, win, gq, wuq, gkv, wukv, tabs_c, tm_c)
        qa, ka, va, qb, kb, vb, qc, kc, vc = xs
        cqa, cka, cva, cqb, ckb, cvb, cqc, ckc, cvc = cs

        oa = _attn_a(qa, ka, va, cka, cva, sinks, l, tq_a, 4 * tq_a)
        ob = _attn_b(qb, kb, vb, ckb, cvb, bias, l)
        oc = _attn_c(qc, kc, vc, ckc, cvc, tq_c // 2, 2 * tq_c)
        x = _outproj_mlp(x, oa, ob, oc, l, None, mods, woa, wob, woc, g2, w1, w2, gf, 2 * tm_x, last)
        if not last:
            coa, cob, coc = _attn_ctx(cs, sinks, l)
            ctx = _outproj_mlp(ctx, coa, cob, coc, l, b, mods, woa, wob, woc, g2, w1, w2, gf, tm_c, False)
    return x
```

```python
import functools
import math

import jax
import jax.numpy as jnp
import numpy as np
from jax import lax
from jax.experimental import pallas as pl
from jax.experimental.pallas import tpu as pltpu

F32 = jnp.float32
BF16 = jnp.bfloat16

GRID_W = 64
HEAD_DIM = 64
A_HEADS = 8
A_KV_HEADS = 2
A_GROUP = A_HEADS // A_KV_HEADS
A_WINDOW = 128
B_HEADS = 4
NA_ROWS = 8
NA_COLS = 16
C_HEADS = 4
C_Q_RANK = 256
C_KV_RANK = 128
C_NOPE = 64
C_ROPE = 32
C_V = 64
N_MOD = 6
ROPE_BASE = 10000.0
EPS = 1e-6
NEG_INF = -1e30
LOG2E = math.log2(math.e)

LANES = 128
MOD_ROWS = 16
IN_COLS = 2048
VMEM_LIMIT = 56 * 1024 * 1024

_OFF_QA, _OFF_KA, _OFF_VA = 0, 512, 640
_OFF_QB, _OFF_KB, _OFF_VB = 768, 1024, 1280
_OFF_CQ, _OFF_CKV, _OFF_KR = 1536, 1792, 1920


def _dot(a, b):
    return jnp.dot(a, b, preferred_element_type=F32)


def _dot_t(a, b):
    return lax.dot_general(a, b, (((1,), (1,)), ((), ())), preferred_element_type=F32)


def _params(*sem):
    return pltpu.CompilerParams(dimension_semantics=sem, vmem_limit_bytes=VMEM_LIMIT)


def _mod_kernel(c_ref, w_ref, b_ref, o_ref):
    c = c_ref[...]
    s = c * (1.0 / (1.0 + jnp.exp(-c)))
    o_ref[0] = _dot(s.astype(BF16), w_ref[0].astype(BF16)) + b_ref[0]


def _modulation(cvec, w_ada, b_ada):
    depth, d, n = w_ada.shape
    tn = 2048
    return pl.pallas_call(
        _mod_kernel,
        grid=(depth, n // tn),
        in_specs=[
            pl.BlockSpec((MOD_ROWS, d), lambda l, j: (0, 0)),
            pl.BlockSpec((1, d, tn), lambda l, j: (l, 0, j)),
            pl.BlockSpec((1, 1, tn), lambda l, j: (l, 0, j)),
        ],
        out_specs=pl.BlockSpec((1, MOD_ROWS, tn), lambda l, j: (l, 0, j)),
        out_shape=jax.ShapeDtypeStruct((depth, MOD_ROWS, n), F32),
        compiler_params=_params("arbitrary", "arbitrary"),
        name="adaln_modulation",
    )(cvec, w_ada, b_ada.reshape(depth, 1, n))


def _rope(x, c, s1, s2, shift):
    return x * c + pltpu.roll(x, LANES - shift, 1) * s1 + pltpu.roll(x, shift, 1) * s2


def _rms(x, g):
    return x * lax.rsqrt(jnp.mean(x * x, axis=-1, keepdims=True) + EPS) * g


def _inproj_kernel(x_ref, mod_ref, g1_ref, win_ref, gq_ref, wuq_ref, gkv_ref, wukv_ref,
                   ac_ref, as1_ref, as2_ref, cc_ref, cs1_ref, cs2_ref,
                   qa_ref, ka_ref, va_ref, qb_ref, kb_ref, vb_ref, qc_ref, kc_ref, vc_ref, *, sub):
    tm = x_ref.shape[1]
    mod = mod_ref[0]
    scale_ab = HEAD_DIM ** -0.5 * LOG2E
    scale_c = (C_NOPE + C_ROPE) ** -0.5 * LOG2E
    ones_lane = (lax.broadcasted_iota(jnp.int32, (1, LANES), 1) == C_V).astype(F32)
    ones_lane0 = (lax.broadcasted_iota(jnp.int32, (1, LANES), 1) == 0).astype(F32)

    subs = [slice(r0, r0 + sub) for r0 in range(0, tm, sub)]

    def up_project(pc):
        cq = _rms(pc[:, 0:C_Q_RANK], gq_ref[...]).astype(BF16)
        ckv = _rms(pc[:, C_Q_RANK:C_Q_RANK + C_KV_RANK], gkv_ref[...]).astype(BF16)
        return _dot(cq, wuq_ref[...]), _dot(ckv, wukv_ref[...])

    pabs, pcs, ups = [], [], []
    for rs in subs:
        h = (_rms(x_ref[0, rs, :], g1_ref[...]) * (1.0 + mod[1:2]) + mod[0:1]).astype(BF16)
        pcs.append(_dot(h, win_ref[:, _OFF_CQ:IN_COLS]))
        pa = _dot(h, win_ref[:, 0:_OFF_QB])
        ups.append(up_project(pcs[-1]))
        pabs.append(jnp.concatenate([pa, _dot(h, win_ref[:, _OFF_QB:_OFF_CQ])], axis=1))

    for rs, p, pc, (q_up, kv_up) in zip(subs, pabs, pcs, ups):
        ac, as1, as2 = ac_ref[rs, :], as1_ref[rs, :], as2_ref[rs, :]
        cc, cs1, cs2 = cc_ref[rs, :], cs1_ref[rs, :], cs2_ref[rs, :]
        for j in range(4):
            blk = p[:, _OFF_QA + LANES * j:_OFF_QA + LANES * (j + 1)]
            qa_ref[0, rs, LANES * j:LANES * (j + 1)] = (_rope(blk, ac, as1, as2, 16) * scale_ab).astype(BF16)
        ka_ref[0, rs, :] = _rope(p[:, _OFF_KA:_OFF_KA + LANES], ac, as1, as2, 16).astype(BF16)
        va = p[:, _OFF_VA:_OFF_VA + LANES]
        lane = lax.broadcasted_iota(jnp.int32, va.shape, 1)
        va_ref[0, rs, 0:LANES] = jnp.where(lane < HEAD_DIM, va, ones_lane).astype(BF16)
        va_ref[0, rs, LANES:2 * LANES] = jnp.where(lane >= HEAD_DIM, va, ones_lane0).astype(BF16)
        qb_ref[0, rs, :] = (p[:, _OFF_QB:_OFF_QB + 256] * scale_ab).astype(BF16)
        kb_ref[0, rs, :] = p[:, _OFF_KB:_OFF_KB + 256].astype(BF16)
        vb_ref[0, rs, :] = p[:, _OFF_VB:_OFF_VB + 256].astype(BF16)
        kr = _rope(pc[:, _OFF_KR - _OFF_CQ:], cc, cs1, cs2, 8)
        lo = lane < C_NOPE
        for hd in range(C_HEADS):
            sl = slice(LANES * hd, LANES * (hd + 1))
            qc_ref[0, rs, sl] = (_rope(q_up[:, sl], cc, cs1, cs2, 8) * scale_c).astype(BF16)
            kv = kv_up[:, sl]
            kc_ref[0, rs, sl] = jnp.where(lo, kv, kr).astype(BF16)
            vc_ref[0, rs, sl] = jnp.where(lo, pltpu.roll(kv, C_V, 1), ones_lane).astype(BF16)


def _inproj(x, layer, mod_row, mods, g1, win, gq, wuq, gkv, wukv, tabs, tm):
    b, s, d = x.shape
    nt = s // tm

    def xmap(j, bb):
        return (bb, j, 0)

    def modmap(j, bb):
        return (layer, (bb if mod_row is None else mod_row), 0, 0)

    def wmap(j, bb):
        return (layer, 0, 0)

    def tabmap(j, bb):
        return (j, 0)

    widths = (512, 128, 256, 256, 256, 256, 512, 512, 512)
    in_specs = [
        pl.BlockSpec((1, tm, d), xmap),
        pl.BlockSpec((None, 1, N_MOD, d), modmap),
        pl.BlockSpec((None, 1, d), wmap),
        pl.BlockSpec((None, d, IN_COLS), wmap),
        pl.BlockSpec((None, 1, C_Q_RANK), wmap),
        pl.BlockSpec((None, C_Q_RANK, 512), wmap),
        pl.BlockSpec((None, 1, C_KV_RANK), wmap),
        pl.BlockSpec((None, C_KV_RANK, 512), wmap),
    ] + [pl.BlockSpec((tm, LANES), tabmap) for _ in range(6)]
    return pl.pallas_call(
        functools.partial(_inproj_kernel, sub=min(tm, 256)),
        grid=(nt, b),
        in_specs=in_specs,
        out_specs=[pl.BlockSpec((1, tm, w), xmap) for w in widths],
        out_shape=[jax.ShapeDtypeStruct((b, s, w), BF16) for w in widths],
        compiler_params=_params("arbitrary", "arbitrary"),
        name="norm_inproj_rope",
    )(x, mods, g1, win, gq, wuq, gkv, wukv, *tabs)


def _lane_lo(shape):
    return lax.broadcasted_iota(jnp.int32, shape, 1) < HEAD_DIM


def _split_pair(q2):
    lo = _lane_lo(q2.shape)
    zero = jnp.zeros_like(q2)
    return jnp.concatenate([jnp.where(lo, q2, zero), jnp.where(lo, zero, q2)], axis=0)


def _merge_pair(o):
    n = o.shape[0] // 2
    return jnp.where(_lane_lo((n, o.shape[1])), o[:n], o[n:])


def _softmax_pv(parts, sink=None):
    m = parts[0][0].max(axis=-1, keepdims=True)
    for s, _ in parts[1:]:
        m = jnp.maximum(m, s.max(axis=-1, keepdims=True))
    if sink is not None:
        m = jnp.maximum(m, sink)
    denom = None
    acc = None
    for s, v in parts:
        e = jnp.exp2(s - m)
        r = e.sum(axis=-1, keepdims=True)
        denom = r if denom is None else denom + r
        o = _dot(e.astype(BF16), v)
        acc = o if acc is None else acc + o
    if sink is not None:
        denom = denom + jnp.exp2(sink - m)
    return acc / denom


def _attn_a_kernel(q_ref, k_ref, v_ref, kx_ref, vx_ref, sink_ref, o_ref, *, tq, seq, ahead):
    t = pl.program_id(1)
    nsub = q_ref.shape[1] // tq
    nloc = tq + 2 * A_WINDOW
    kx = kx_ref[0]
    lo = _lane_lo((tq, LANES))
    iota_rel = (lax.broadcasted_iota(jnp.int32, (tq, nloc), 1) - lax.broadcasted_iota(jnp.int32, (tq, nloc), 0))
    starts, bands = [], []
    for sb in range(nsub):
        qstart = (t * nsub + sb) * tq
        start = pl.multiple_of(jnp.clip(qstart - A_WINDOW, 0, seq - nloc), A_WINDOW)
        rel = iota_rel + (start - qstart)
        starts.append(start)
        bands.append(jnp.where(jnp.abs(rel) <= A_WINDOW, 0.0, NEG_INF))
    units = [(sb, j, g) for sb in range(nsub) for j in range(A_GROUP) for g in range(A_KV_HEADS)]

    def scores(u):
        sb, j, g = units[u]
        q2 = q_ref[0, sb * tq:(sb + 1) * tq, LANES * j:LANES * (j + 1)]
        zero = jnp.zeros_like(q2)
        q = jnp.where(lo, q2, zero) if g == 0 else jnp.where(lo, zero, q2)
        return _dot_t(q, k_ref[0, pl.ds(starts[sb], nloc), :]) + bands[sb], _dot_t(q, kx)

    outs = {}
    queue = [scores(u) for u in range(min(ahead, len(units)))]
    for u, (sb, j, g) in enumerate(units):
        s_loc, s_ctx = queue.pop(0)
        if u + ahead < len(units):
            queue.append(scores(u + ahead))
        vsg = v_ref[0, pl.ds(starts[sb], nloc), LANES * g:LANES * (g + 1)]
        vxg = vx_ref[0, :, LANES * g:LANES * (g + 1)]
        hd = g * A_GROUP + j
        sink = jnp.broadcast_to(sink_ref[hd:hd + 1, 0:1], (tq, 1))
        m = jnp.maximum(jnp.maximum(s_loc.max(axis=-1, keepdims=True), s_ctx.max(axis=-1, keepdims=True)), sink)
        o = _dot(jnp.exp2(s_loc - m).astype(BF16), vsg) + _dot(jnp.exp2(s_ctx - m).astype(BF16), vxg)
        denom = (o[:, HEAD_DIM:HEAD_DIM + 1] if g == 0 else o[:, 0:1]) + jnp.exp2(sink - m)
        outs[g] = o / denom
        if g == A_KV_HEADS - 1:
            o_ref[0, sb * tq:(sb + 1) * tq, LANES * j:LANES * (j + 1)] = jnp.where(lo, outs[0], outs[1]).astype(BF16)


def _attn_a(qa, ka, va, kax, vax, sinks, layer, tq, tstep):
    b, s, _ = qa.shape
    c = kax.shape[1]

    def qmap(bb, t):
        return (bb, t, 0)

    def kmap(bb, t):
        return (bb, 0, 0)

    return pl.pallas_call(
        functools.partial(_attn_a_kernel, tq=tq, seq=s, ahead=2),
        grid=(b, s // tstep),
        in_specs=[
            pl.BlockSpec((1, tstep, 512), qmap),
            pl.BlockSpec((1, s, LANES), kmap),
            pl.BlockSpec((1, s, 2 * LANES), kmap),
            pl.BlockSpec((1, c, LANES), kmap),
            pl.BlockSpec((1, c, 2 * LANES), kmap),
            pl.BlockSpec((None, A_HEADS, LANES), lambda bb, t: (layer, 0, 0)),
        ],
        out_specs=pl.BlockSpec((1, tstep, 512), qmap),
        out_shape=jax.ShapeDtypeStruct((b, s, 512), BF16),
        compiler_params=_params("arbitrary", "arbitrary"),
        name="attn_window_gqa",
    )(qa, ka, va, kax, vax, sinks)


def _expand_rel_bias(rpb_ref, bias_ref):
    shape = (GRID_W, LANES)
    qc = lax.broadcasted_iota(jnp.int32, shape, 0)
    lane = lax.broadcasted_iota(jnp.int32, shape, 1)
    kc = lane & (GRID_W - 1)
    wc = jnp.clip(qc - NA_COLS // 2, 0, GRID_W - NA_COLS)
    colmask = jnp.where((kc >= wc) & (kc < wc + NA_COLS), 0.0, NEG_INF)
    lo = lane < GRID_W
    nrel = 2 * NA_ROWS - 1
    for hh in range(2):
        rows = [jnp.broadcast_to(rpb_ref[hh, d:d + 1, :], shape) for d in range(nrel)]
        t_lo = [pltpu.roll(r, GRID_W + 1, 1, stride=1, stride_axis=0) for r in rows[:-1]]
        t_hi = [pltpu.roll(r, 1, 1, stride=1, stride_axis=0) for r in rows[1:]]
        for d in range(nrel - 1):
            bias_ref[hh, d] = jnp.where(lo, t_lo[d], t_hi[d]) + colmask


def _attn_b_kernel(q_ref, k_ref, v_ref, kx_ref, vx_ref, rpb_ref, o_ref, bias_ref, *, rows, ahead):
    rb = pl.program_id(2)
    kx, vx = kx_ref[0], vx_ref[0]
    nrow = q_ref.shape[1] // GRID_W

    @pl.when((pl.program_id(1) == 0) & (rb == 0))
    def _():
        _expand_rel_bias(rpb_ref, bias_ref)

    def scores(r):
        qr = rb * nrow + r
        wr = jnp.clip(qr - NA_ROWS // 2, 0, rows - NA_ROWS)
        oi = wr - qr + (NA_ROWS - 1)
        start = pl.multiple_of(wr * GRID_W, GRID_W)
        q = _split_pair(q_ref[0, r * GRID_W:(r + 1) * GRID_W, :])
        ks = k_ref[0, pl.ds(start, NA_ROWS * GRID_W), :]
        bias = jnp.concatenate(
            [jnp.concatenate([bias_ref[hh, oi + 2 * t] for t in range(NA_ROWS // 2)], axis=1)
             for hh in range(2)], axis=0)
        return _dot_t(q, ks) + bias, _dot_t(q, kx), start

    queue = [scores(r) for r in range(min(ahead, nrow))]
    for r in range(nrow):
        s_loc, s_ctx, start = queue.pop(0)
        if r + ahead < nrow:
            queue.append(scores(r + ahead))
        vs = v_ref[0, pl.ds(start, NA_ROWS * GRID_W), :]
        o = _softmax_pv([(s_loc, vs), (s_ctx, vx)])
        o_ref[0, r * GRID_W:(r + 1) * GRID_W, :] = _merge_pair(o).astype(BF16)


def _attn_b(qb, kb, vb, kbx, vbx, rpb, layer):
    b, s, _ = qb.shape
    c = kbx.shape[1]
    rows = s // GRID_W
    tq = min(4 * NA_ROWS * GRID_W, s)

    def qmap(p, bb, rb):
        return (bb, rb, p)

    def kmap(p, bb, rb):
        return (bb, 0, p)

    return pl.pallas_call(
        functools.partial(_attn_b_kernel, rows=rows, ahead=3),
        grid=(B_HEADS // 2, b, s // tq),
        in_specs=[
            pl.BlockSpec((1, tq, LANES), qmap),
            pl.BlockSpec((1, s, LANES), kmap),
            pl.BlockSpec((1, s, LANES), kmap),
            pl.BlockSpec((1, c, LANES), kmap),
            pl.BlockSpec((1, c, LANES), kmap),
            pl.BlockSpec((None, 2, 2 * NA_ROWS, LANES), lambda p, bb, rb: (layer, p, 0, 0)),
        ],
        out_specs=pl.BlockSpec((1, tq, LANES), qmap),
        out_shape=jax.ShapeDtypeStruct((b, s, 256), BF16),
        scratch_shapes=[pltpu.VMEM((2, 2 * NA_ROWS - 2, GRID_W, LANES), F32)],
        compiler_params=_params("arbitrary", "arbitrary", "arbitrary"),
        name="attn_neighbourhood",
    )(qb, kb, vb, kbx, vbx, rpb)


def _rel_bias_rows(rpb):
    pad = GRID_W - NA_COLS
    return jnp.pad(rpb * LOG2E, ((0, 0), (0, 0), (0, 1), (pad, pad + 1))).astype(F32)


def _attn_c_kernel(q_ref, k_ref, kx_ref, v_ref, vx_ref, o_ref, *, tq, kc, ahead):
    s = k_ref.shape[1]
    nsub = q_ref.shape[1] // tq
    chunks = [(k_ref, v_ref, c0, kc) for c0 in range(0, s, kc)] + [(kx_ref, vx_ref, 0, kx_ref.shape[1])]
    units = [(sb, ci) for sb in range(nsub) for ci in range(len(chunks))]

    def scores(u):
        sb, ci = units[u]
        kr, _, c0, n = chunks[ci]
        return [_dot_t(q_ref[0, sb * tq:(sb + 1) * tq, LANES * hh:LANES * (hh + 1)],
                       kr[0, c0:c0 + n, LANES * hh:LANES * (hh + 1)]) for hh in range(2)]

    m = [None, None]
    acc = [None, None]
    queue = [scores(u) for u in range(min(ahead, len(units)))]
    for u, (sb, ci) in enumerate(units):
        cur = queue.pop(0)
        if u + ahead < len(units):
            queue.append(scores(u + ahead))
        _, vr, c0, n = chunks[ci]
        for hh in range(2):
            sl = slice(LANES * hh, LANES * (hh + 1))
            sc = cur[hh]
            cm = sc.max(axis=-1, keepdims=True)
            if ci == 0:
                m[hh] = cm
                acc[hh] = _dot(jnp.exp2(sc - cm).astype(BF16), vr[0, c0:c0 + n, sl])
            else:
                m_new = jnp.maximum(m[hh], cm)
                acc[hh] = (jnp.exp2(m[hh] - m_new) * acc[hh]
                           + _dot(jnp.exp2(sc - m_new).astype(BF16), vr[0, c0:c0 + n, sl]))
                m[hh] = m_new
        if ci == len(chunks) - 1:
            o0 = acc[0] / acc[0][:, C_V:C_V + 1]
            o1 = acc[1] / acc[1][:, C_V:C_V + 1]
            o_ref[0, sb * tq:(sb + 1) * tq, :] = jnp.where(
                _lane_lo(o0.shape), o0, pltpu.roll(o1, C_V, 1)).astype(BF16)


def _attn_c(qc, kc, vc, kcx, vcx, tq, tstep):
    b, s, _ = qc.shape
    c = kcx.shape[1]

    def qmap(bb, p, i):
        return (bb, i, p)

    def kmap(bb, p, i):
        return (bb, 0, p)

    return pl.pallas_call(
        functools.partial(_attn_c_kernel, tq=tq, kc=512, ahead=1),
        grid=(b, C_HEADS // 2, s // tstep),
        in_specs=[
            pl.BlockSpec((1, tstep, 2 * LANES), qmap),
            pl.BlockSpec((1, s, 2 * LANES), kmap),
            pl.BlockSpec((1, c, 2 * LANES), kmap),
            pl.BlockSpec((1, s, 2 * LANES), kmap),
            pl.BlockSpec((1, c, 2 * LANES), kmap),
        ],
        out_specs=pl.BlockSpec((1, tstep, LANES), qmap),
        out_shape=jax.ShapeDtypeStruct((b, s, 256), BF16),
        compiler_params=_params("arbitrary", "arbitrary", "arbitrary"),
        name="attn_latent",
    )(qc, kc, kcx, vc, vcx)


def _attn_ctx_kernel(qa_ref, ka_ref, va_ref, qb_ref, kb_ref, vb_ref, qc_ref, kc_ref, vc_ref,
                     sink_ref, oa_ref, ob_ref, oc_ref):
    c = qa_ref.shape[1]
    half = A_HEADS // 2
    lo = _lane_lo((c, LANES))

    def blk(i):
        return slice(LANES * i, LANES * (i + 1))

    units = []
    for j in range(half):
        for g in range(A_KV_HEADS):
            def score(j=j, g=g):
                q2 = qa_ref[0, :, blk(j)]
                zero = jnp.zeros_like(q2)
                return _dot_t(jnp.where(lo, q2, zero) if g == 0 else jnp.where(lo, zero, q2), ka_ref[0])

            def fin_a(o, j=j):
                oa_ref[0, :, blk(j)] = jnp.where(lo, o[0], o[1]).astype(BF16)

            units.append((score, lambda g=g: va_ref[0, :, blk(g)], j + half * g, fin_a if g == 1 else None))
    for p in range(B_HEADS // 2):
        def score(p=p):
            return _dot_t(_split_pair(qb_ref[0, :, blk(p)]), kb_ref[0, :, blk(p)])

        def fin_b(o, p=p):
            ob_ref[0, :, blk(p)] = _merge_pair(o[0]).astype(BF16)

        units.append((score, lambda p=p: vb_ref[0, :, blk(p)], None, fin_b))
    for hd in range(C_HEADS):
        def score(hd=hd):
            return _dot_t(qc_ref[0, :, blk(hd)], kc_ref[0, :, blk(hd)])

        def fin_c(o, hd=hd):
            oc_ref[0, :, blk(hd // 2)] = jnp.where(lo, o[0], pltpu.roll(o[1], C_V, 1)).astype(BF16)

        units.append((score, lambda hd=hd: vc_ref[0, :, blk(hd)], None, fin_c if hd % 2 == 1 else None))

    pending = []
    nxt = units[0][0]()
    for u, (_, value, sink_head, finish) in enumerate(units):
        s = nxt
        if u + 1 < len(units):
            nxt = units[u + 1][0]()
        sink = None
        if sink_head is not None:
            sink = jnp.broadcast_to(sink_ref[sink_head:sink_head + 1, 0:1], (s.shape[0], 1))
        pending.append(_softmax_pv([(s, value())], sink))
        if finish is not None:
            finish(pending)
            pending = []


def _attn_ctx(parts, sinks, layer):
    b, c, _ = parts[0].shape
    widths = [a.shape[2] for a in parts]

    def bmap(bb):
        return (bb, 0, 0)

    return pl.pallas_call(
        _attn_ctx_kernel,
        grid=(b,),
        in_specs=[pl.BlockSpec((1, c, w), bmap) for w in widths]
        + [pl.BlockSpec((None, A_HEADS, LANES), lambda bb: (layer, 0, 0))],
        out_specs=[pl.BlockSpec((1, c, w), bmap) for w in (512, 256, 256)],
        out_shape=[jax.ShapeDtypeStruct((b, c, w), BF16) for w in (512, 256, 256)],
        compiler_params=_params("arbitrary"),
        name="attn_context",
    )(*parts, sinks)


def _outproj_mlp_kernel(x_ref, ma_ref, mb_ref, mc_ref, mod_ref, woa_ref, wob_ref, woc_ref,
                        g2_ref, w1_ref, w2_ref, gf_ref, o_ref, *, final, ff_chunk):
    mod = mod_ref[0]
    attn = _dot(ma_ref[0], woa_ref[...]) + _dot(mb_ref[0], wob_ref[...]) + _dot(mc_ref[0], woc_ref[...])
    x1 = x_ref[0] + mod[2:3] * attn
    h = (_rms(x1, g2_ref[...]) * (1.0 + mod[4:5]) + mod[3:4]).astype(BF16)
    d_ff = w1_ref.shape[1]
    def up(c0):
        return _dot(h, w1_ref[:, c0:c0 + ff_chunk])

    y = None
    chunk_starts = list(range(0, d_ff, ff_chunk))
    nxt = up(chunk_starts[0])
    for i, c0 in enumerate(chunk_starts):
        u = jnp.maximum(nxt, 0.0)
        if i + 1 < len(chunk_starts):
            nxt = up(chunk_starts[i + 1])
        part = _dot((u * u).astype(BF16), w2_ref[c0:c0 + ff_chunk, :])
        y = part if y is None else y + part
    x2 = x1 + mod[5:6] * y
    if final:
        x2 = _rms(x2, gf_ref[...])
    o_ref[0] = x2


def _outproj_mlp(x, ma, mb, mc, layer, mod_row, mods, woa, wob, woc, g2, w1, w2, gf, tm, final):
    b, s, d = x.shape
    d_ff = w1.shape[2]

    def xmap(bb, j):
        return (bb, j, 0)

    def modmap(bb, j):
        return (layer, (bb if mod_row is None else mod_row), 0, 0)

    def resident(shape, row_block=0):
        return pl.BlockSpec((None,) + shape, lambda bb, j: (layer, row_block, 0), pipeline_mode=pl.Buffered(1))

    return pl.pallas_call(
        functools.partial(_outproj_mlp_kernel, final=final, ff_chunk=1024),
        grid=(b, s // tm),
        in_specs=[
            pl.BlockSpec((1, tm, d), xmap),
            pl.BlockSpec((1, tm, 512), xmap),
            pl.BlockSpec((1, tm, 256), xmap),
            pl.BlockSpec((1, tm, 256), xmap),
            pl.BlockSpec((None, 1, N_MOD, d), modmap),
            resident((512, d)), resident((256, d), 2), resident((256, d), 3),
            resident((1, d)),
            resident((d, d_ff)), resident((d_ff, d)),
            pl.BlockSpec((1, d), lambda bb, j: (0, 0)),
        ],
        out_specs=pl.BlockSpec((1, tm, d), xmap),
        out_shape=jax.ShapeDtypeStruct((b, s, d), F32),
        compiler_params=_params("arbitrary", "arbitrary"),
        name="outproj_mlp",
    )(x, ma, mb, mc, mods, woa, wob, woc, g2, w1, w2, gf)


def _rope_tables(s):
    f32 = np.float32
    tok = np.arange(s)
    row, col = (tok // GRID_W).astype(f32), (tok % GRID_W).astype(f32)

    def cs(pos, half):
        freqs = (f32(ROPE_BASE) ** (-np.arange(half, dtype=f32) / f32(half))).astype(f32)
        ang = (pos[:, None] * freqs).astype(f32)
        return np.cos(ang).astype(f32), np.sin(ang).astype(f32)

    cr, sr = cs(row, 16)
    cc, sc = cs(col, 16)
    z = np.zeros_like(sr)
    a_c = np.tile(np.concatenate([cr, cr, cc, cc], axis=1), (1, 2))
    a_s1 = np.tile(np.concatenate([-sr, z, -sc, z], axis=1), (1, 2))
    a_s2 = np.tile(np.concatenate([z, sr, z, sc], axis=1), (1, 2))
    cr, sr = cs(row, 8)
    cc, sc = cs(col, 8)
    z = np.zeros_like(sr)
    one64, zero64 = np.ones((s, C_NOPE), f32), np.zeros((s, C_NOPE), f32)
    one32, zero32 = np.ones((s, 32), f32), np.zeros((s, 32), f32)
    c_c = np.concatenate([one64, cr, cr, cc, cc, one32], axis=1)
    c_s1 = np.concatenate([zero64, -sr, z, -sc, z, zero32], axis=1)
    c_s2 = np.concatenate([zero64, z, sr, z, sc, zero32], axis=1)
    return tuple(jnp.asarray(t) for t in (a_c, a_s1, a_s2, c_c, c_s1, c_s2))


def _identity_tables(s):
    one, zero = jnp.asarray(np.ones((s, LANES), np.float32)), jnp.asarray(np.zeros((s, LANES), np.float32))
    return (one, zero, zero, one, zero, zero)


def _pad_heads(w, offs, width):
    pad = jnp.zeros(w.shape[:2] + (LANES - width,), w.dtype)
    return jnp.concatenate([jnp.concatenate([w[:, :, o:o + width], pad], axis=2) for o in offs], axis=2)


def _weight_layouts(w_in, w_uq, w_ukv, w_out):
    depth, d, _ = w_in.shape
    w_in, w_out = w_in.astype(BF16), w_out.astype(BF16)
    qa_w = w_in[:, :, :512].reshape(depth, d, A_KV_HEADS, A_GROUP, HEAD_DIM).transpose(0, 1, 3, 2, 4)
    qa_w = qa_w.reshape(depth, d, 512)
    z = lambda n: jnp.zeros((depth, d, n), w_in.dtype)
    win = jnp.concatenate([qa_w, w_in[:, :, 512:1920], z(64), w_in[:, :, 1920:1952], z(32)], axis=2)
    hq = C_NOPE + C_ROPE
    wuq = _pad_heads(w_uq, [hq * h for h in range(C_HEADS)], hq).astype(BF16)
    wukv = w_ukv.astype(BF16)
    woa = w_out[:, :512].reshape(depth, A_KV_HEADS, A_GROUP, HEAD_DIM, d).transpose(0, 2, 1, 3, 4)
    woa = woa.reshape(depth, 512, d)
    wob = woc = w_out
    return win, wuq, wukv, woa, wob, woc


def kernel(x, c, ctx, c_ctx, w_ada, b_ada, norm1_g, norm2_g, w_in, attn_sink, na_rpb, mla_q_norm_g,
           mla_w_uq, mla_kv_norm_g, mla_w_ukv, w_out, w_mlp_in, w_mlp_out, final_norm_g):
    b, s, d = x.shape
    n_ctx = ctx.shape[1]
    depth = w_ada.shape[0]
    assert b + 1 <= MOD_ROWS and s % 1024 == 0 and n_ctx % 128 == 0
    tm_x = 512
    tm_c = min(n_ctx, 256)
    tq_a = 256
    tq_c = 512

    cvec = jnp.concatenate([c, c_ctx[None], jnp.zeros((MOD_ROWS - b - 1, d), c.dtype)], axis=0)
    mods = _modulation(cvec, w_ada, b_ada).reshape(depth, MOD_ROWS, N_MOD, d)
    tabs_x = _rope_tables(s)
    tabs_c = _identity_tables(n_ctx)
    gf = final_norm_g.reshape(1, d)
    win, wuq, wukv, woa, wob, woc = _weight_layouts(w_in, mla_w_uq, mla_w_ukv, w_out)
    w1, w2 = w_mlp_in.astype(BF16), w_mlp_out.astype(BF16)
    sinks = jnp.broadcast_to((attn_sink * LOG2E)[:, :, None], (depth, A_HEADS, LANES)).astype(F32)
    bias = _rel_bias_rows(na_rpb)
    g1, g2 = norm1_g.reshape(depth, 1, d), norm2_g.reshape(depth, 1, d)
    gq, gkv = mla_q_norm_g.reshape(depth, 1, -1), mla_kv_norm_g.reshape(depth, 1, -1)

    for l in range(depth):
        last = l == depth - 1
        xs = _inproj(x, l, None, mods, g1, win, gq, wuq, gkv, wukv, tabs_x, 2 * tm_x)
        cs = _inproj(ctx, l, b, mods, g1, win, gq, wuq, gkv, wukv, tabs_c, tm_c)
        qa, ka, va, qb, kb, vb, qc, kc, vc = xs
        cqa, cka, cva, cqb, ckb, cvb, cqc, ckc, cvc = cs

        oa = _attn_a(qa, ka, va, cka, cva, sinks, l, tq_a, 4 * tq_a)
        ob = _attn_b(qb, kb, vb, ckb, cvb, bias, l)
        oc = _attn_c(qc, kc, vc, ckc, cvc, tq_c // 2, 2 * tq_c)
        x = _outproj_mlp(x, oa, ob, oc, l, None, mods, woa, wob, woc, g2, w1, w2, gf, 2 * tm_x, last)
        if not last:
            coa, cob, coc = _attn_ctx(cs, sinks, l)
            ctx = _outproj_mlp(ctx, coa, cob, coc, l, b, mods, woa, wob, woc, g2, w1, w2, gf, tm_c, False)
    return x
```
